```python
import math
import jax, jax.numpy as jnp
from jax import lax
import numpy as np

D_MODEL = 1024
BATCH = 8
SEQ = 4096
DEPTH = 1

PLE_DIM = 256
N_HEADS = 8
HEAD_DIM = 64
N_KV_HEADS = 2
HEADS_PER_GROUP = N_HEADS // N_KV_HEADS
L_CMP = 32
D_STRIDE = 16
CMP_HIDDEN = 256
L_SLC = 64
N_SEL = 16
WINDOW = 512
Q_BLOCK = 128
N_BUCKETS = 32
MAX_EXACT = N_BUCKETS // 2
MAX_DISTANCE = 128
D_CONV = D_MODEL // 2
CONV_WIDTH = 31
D_FF = 2816
FFN_CONV_WIDTH = 3

EPS = 1e-6
NEG = -1e30
FORCE = 1e4

N_CONV_IN = 2 * D_CONV
N_Q = N_HEADS * HEAD_DIM
N_KV = 6 * N_KV_HEADS * HEAD_DIM
N_NSA_GATE = 3 * N_HEADS
N_MERGE = 2 * D_MODEL
N_IN = N_CONV_IN + N_Q + N_KV + N_NSA_GATE + N_MERGE

kernel_name = "hybrid_conformer_nsa_gated_block"


def rmsnorm(x, g):
    xf = x.astype(jnp.float32)
    y = xf * lax.rsqrt(jnp.mean(xf * xf, axis=-1, keepdims=True) + EPS)
    return (y * g.astype(jnp.float32)).astype(x.dtype)


def layernorm(x, g, b):
    xf = x.astype(jnp.float32)
    mu = jnp.mean(xf, axis=-1, keepdims=True)
    var = jnp.mean(jnp.square(xf - mu), axis=-1, keepdims=True)
    y = (xf - mu) * lax.rsqrt(var + EPS)
    return (y * g.astype(jnp.float32) + b.astype(jnp.float32)).astype(x.dtype)


def causal_dwconv(x, w, b):
    k, c = w.shape
    y = lax.conv_general_dilated(x, w[:, None, :].astype(x.dtype), (1,), [(k - 1, 0)],
                                 dimension_numbers=('NWC', 'WIO', 'NWC'),
                                 feature_group_count=c)
    return y + b.astype(x.dtype)


def t5_bucket(dist):
    n = jnp.maximum(dist, 0)
    nf = jnp.maximum(n, MAX_EXACT).astype(jnp.float32)
    large = MAX_EXACT + (jnp.log(nf / MAX_EXACT) / math.log(MAX_DISTANCE / MAX_EXACT)
                         * (N_BUCKETS - MAX_EXACT)).astype(jnp.int32)
    large = jnp.minimum(large, N_BUCKETS - 1)
    return jnp.where(n < MAX_EXACT, n, large)


def masked_softmax(s, mask):
    p = jax.nn.softmax(jnp.where(mask, s, NEG), axis=-1)
    return jnp.where(mask, p, 0.0)


def compress(kv, pe, w1, w2):
    b, s, g, dh = kv.shape
    nc = (s - L_CMP) // D_STRIDE + 1
    idx = jnp.arange(nc)[:, None] * D_STRIDE + jnp.arange(L_CMP)[None, :]
    blocks = kv[:, idx] + pe[None, None, :, None, :]
    blocks = blocks.transpose(0, 1, 3, 2, 4).reshape(b, nc, g, L_CMP * dh)
    return jax.nn.gelu(blocks @ w1) @ w2


def nsa_attention(q, k_cmp, v_cmp, k_slc, v_slc, k_win, v_win, gates, rel_bias):
    b, s, g, hpg, dh = q.shape
    nc = k_cmp.shape[1]
    ns = s // L_SLC
    n_sel = min(N_SEL, ns)
    nqb = s // Q_BLOCK
    scale = 1.0 / math.sqrt(dh)

    c_start = jnp.arange(nc) * D_STRIDE
    c_end = c_start + L_CMP - 1
    s_start = jnp.arange(ns) * L_SLC
    agg = ((c_end[:, None] >= s_start[None, :]) &
           (c_start[:, None] <= s_start[None, :] + L_SLC - 1)).astype(jnp.float32)

    k_blocks = k_slc.reshape(b, ns, L_SLC, g, dh).transpose(0, 3, 1, 2, 4)
    v_blocks = v_slc.reshape(b, ns, L_SLC, g, dh).transpose(0, 3, 1, 2, 4)
    k_pad = jnp.pad(k_win, ((0, 0), (WINDOW, 0), (0, 0), (0, 0)))
    v_pad = jnp.pad(v_win, ((0, 0), (WINDOW, 0), (0, 0), (0, 0)))
    bias_grouped = rel_bias.reshape(N_BUCKETS, g, hpg).transpose(1, 0, 2)
    bi = jnp.arange(b)[:, None, None, None]
    gi = jnp.arange(g)[None, :, None, None]

    def head_bias(dist):
        bb = rel_bias[t5_bucket(dist)].astype(jnp.float32)
        return bb.reshape(dist.shape + (g, hpg)).transpose(2, 3, 0, 1)

    def block(qi):
        t0 = qi * Q_BLOCK
        qb = lax.dynamic_slice_in_dim(q, t0, Q_BLOCK, axis=1)
        gb = lax.dynamic_slice_in_dim(gates, t0, Q_BLOCK, axis=1)
        tpos = t0 + jnp.arange(Q_BLOCK)

        dist_c = tpos[:, None] - c_end[None, :]
        s_c = jnp.einsum('bqghd,bcgd->bghqc', qb, k_cmp).astype(jnp.float32) * scale
        p_c = masked_softmax(s_c + head_bias(dist_c), dist_c >= 0)
        o_c = jnp.einsum('bghqc,bcgd->bqghd', p_c.astype(v_cmp.dtype), v_cmp)

        imp = jnp.einsum('bghqc,cs->bgqs', p_c, agg)
        blk = jnp.arange(ns)[None, :]
        cur = (tpos // L_SLC)[:, None]
        valid = blk * L_SLC <= tpos[:, None]
        forced = (blk == 0) | (blk == cur) | (blk == cur - 1)
        imp = jnp.where(forced, FORCE, jnp.where(valid, imp, -FORCE))
        _, idx = lax.top_k(imp, n_sel)
        kb = k_blocks[bi, gi, idx].reshape(b, g, Q_BLOCK, n_sel * L_SLC, dh)
        vb = v_blocks[bi, gi, idx].reshape(b, g, Q_BLOCK, n_sel * L_SLC, dh)
        kpos = (idx[..., None] * L_SLC + jnp.arange(L_SLC)).reshape(b, g, Q_BLOCK, n_sel * L_SLC)
        dist_s = tpos[None, None, :, None] - kpos
        bias_s = bias_grouped[gi, t5_bucket(dist_s)].astype(jnp.float32).transpose(0, 1, 4, 2, 3)
        s_s = jnp.einsum('bqghd,bgqkd->bghqk', qb, kb).astype(jnp.float32) * scale + bias_s
        p_s = masked_softmax(s_s, (dist_s >= 0)[:, :, None])
        o_s = jnp.einsum('bghqk,bgqkd->bqghd', p_s.astype(vb.dtype), vb)

        kw = lax.dynamic_slice_in_dim(k_pad, t0, Q_BLOCK + WINDOW, axis=1)
        vw = lax.dynamic_slice_in_dim(v_pad, t0, Q_BLOCK + WINDOW, axis=1)
        kwpos = t0 - WINDOW + jnp.arange(Q_BLOCK + WINDOW)
        dist_w = tpos[:, None] - kwpos[None, :]
        mask_w = (dist_w >= 0) & (dist_w < WINDOW) & (kwpos[None, :] >= 0)
        s_w = jnp.einsum('bqghd,bkgd->bghqk', qb, kw).astype(jnp.float32) * scale
        p_w = masked_softmax(s_w + head_bias(dist_w), mask_w)
        o_w = jnp.einsum('bghqk,bkgd->bqghd', p_w.astype(vw.dtype), vw)

        return gb[..., 0:1] * o_c + gb[..., 1:2] * o_s + gb[..., 2:3] * o_w

    out = lax.map(block, jnp.arange(nqb))
    return jnp.moveaxis(out, 0, 1).reshape(b, s, g * hpg * dh)


def setup_inputs(seed: int = 0) -> dict:
    key = jax.random.key(seed)
    ks = jax.random.split(key, 27)
    f32 = jnp.float32

    def nrm(k, shape, scale):
        return jax.random.normal(k, shape, f32) * scale

    def gain(k, shape):
        return 1.0 + 0.01 * jax.random.normal(k, shape, f32)

    L = DEPTH
    return {
        "x": nrm(ks[0], (BATCH, SEQ, D_MODEL), 1.0),
        "p": nrm(ks[1], (DEPTH, BATCH, SEQ, PLE_DIM), 1.0),
        "rel_bias": nrm(ks[2], (N_BUCKETS, N_HEADS), 0.1),
        "norm_mix": gain(ks[3], (L, D_MODEL)),
        "w_in": nrm(ks[4], (L, D_MODEL, N_IN), D_MODEL ** -0.5),
        "conv_dw_w": nrm(ks[5], (L, CONV_WIDTH, D_CONV), CONV_WIDTH ** -0.5),
        "conv_dw_b": nrm(ks[6], (L, D_CONV), 0.01),
        "conv_ln_g": gain(ks[7], (L, D_CONV)),
        "conv_ln_b": nrm(ks[8], (L, D_CONV), 0.01),
        "w_conv_out": nrm(ks[9], (L, D_CONV, D_MODEL), D_CONV ** -0.5),
        "cmp_pe_k": nrm(ks[10], (L, L_CMP, HEAD_DIM), 0.1),
        "cmp_pe_v": nrm(ks[11], (L, L_CMP, HEAD_DIM), 0.1),
        "w_ck1": nrm(ks[12], (L, L_CMP * HEAD_DIM, CMP_HIDDEN), (L_CMP * HEAD_DIM) ** -0.5),
        "w_ck2": nrm(ks[13], (L, CMP_HIDDEN, HEAD_DIM), CMP_HIDDEN ** -0.5),
        "w_cv1": nrm(ks[14], (L, L_CMP * HEAD_DIM, CMP_HIDDEN), (L_CMP * HEAD_DIM) ** -0.5),
        "w_cv2": nrm(ks[15], (L, CMP_HIDDEN, HEAD_DIM), CMP_HIDDEN ** -0.5),
        "w_attn_out": nrm(ks[16], (L, N_HEADS * HEAD_DIM, D_MODEL), (N_HEADS * HEAD_DIM) ** -0.5),
        "w_out": nrm(ks[17], (L, D_MODEL, D_MODEL), D_MODEL ** -0.5),
        "norm_ffn": gain(ks[18], (L, D_MODEL)),
        "w_up": nrm(ks[19], (L, D_MODEL, 2 * D_FF), D_MODEL ** -0.5),
        "ffn_dw_w": nrm(ks[20], (L, FFN_CONV_WIDTH, 2 * D_FF), FFN_CONV_WIDTH ** -0.5),
        "ffn_dw_b": nrm(ks[21], (L, 2 * D_FF), 0.01),
        "w_down": nrm(ks[22], (L, D_FF, D_MODEL), D_FF ** -0.5),
        "norm_ple": gain(ks[23], (L, D_MODEL)),
        "w_ple_gate": nrm(ks[24], (L, D_MODEL, D_MODEL), D_MODEL ** -0.5),
        "w_ple": nrm(ks[25], (L, PLE_DIM, D_MODEL), PLE_DIM ** -0.5),
        "norm_final": gain(ks[26], (D_MODEL,)),
    }


def reference(x, p, rel_bias, norm_mix, w_in, conv_dw_w, conv_dw_b, conv_ln_g, conv_ln_b,
              w_conv_out, cmp_pe_k, cmp_pe_v, w_ck1, w_ck2, w_cv1, w_cv2, w_attn_out, w_out,
              norm_ffn, w_up, ffn_dw_w, ffn_dw_b, w_down, norm_ple, w_ple_gate, w_ple, norm_final):
    b, s, d = x.shape
    g, hpg, dh = N_KV_HEADS, HEADS_PER_GROUP, HEAD_DIM
    splits = [N_CONV_IN, N_CONV_IN + N_Q, N_CONV_IN + N_Q + N_KV,
              N_CONV_IN + N_Q + N_KV + N_NSA_GATE]
    for i in range(DEPTH):
        h = rmsnorm(x, norm_mix[i])
        z = h @ w_in[i]
        u, qf, kvf, ng, mg = jnp.split(z, splits, axis=-1)

        a = u[..., :D_CONV] * jax.nn.sigmoid(u[..., D_CONV:])
        a = causal_dwconv(a, conv_dw_w[i], conv_dw_b[i])
        a = jax.nn.silu(layernorm(a, conv_ln_g[i], conv_ln_b[i]))
        y_conv = a @ w_conv_out[i]

        q = qf.reshape(b, s, g, hpg, dh)
        kv = kvf.reshape(b, s, 6, g, dh)
        k_c, v_c, k_s, v_s, k_w, v_w = (kv[:, :, 0], kv[:, :, 1], kv[:, :, 2],
                                        kv[:, :, 3], kv[:, :, 4], kv[:, :, 5])
        k_cmp = compress(k_c, cmp_pe_k[i], w_ck1[i], w_ck2[i])
        v_cmp = compress(v_c, cmp_pe_v[i], w_cv1[i], w_cv2[i])
        nsa_gates = jax.nn.sigmoid(ng).reshape(b, s, g, hpg, 3)
        o = nsa_attention(q, k_cmp, v_cmp, k_s, v_s, k_w, v_w, nsa_gates, rel_bias)
        y_attn = o @ w_attn_out[i]

        y = jax.nn.sigmoid(mg[..., :d]) * y_conv + jax.nn.sigmoid(mg[..., d:]) * y_attn
        x = x + y @ w_out[i]

        hf = rmsnorm(x, norm_ffn[i])
        up = causal_dwconv(hf @ w_up[i], ffn_dw_w[i], ffn_dw_b[i])
        gate, val = jnp.split(up, 2, axis=-1)
        x = x + (jax.nn.gelu(gate) * val) @ w_down[i]

        pg = jax.nn.sigmoid(rmsnorm(x, norm_ple[i]) @ w_ple_gate[i])
        x = x + pg * (p[i] @ w_ple[i])
    return rmsnorm(x, norm_final)
```

```python
import functools
import math

import jax
import jax.numpy as jnp
import numpy as np
from jax import lax
from jax.experimental import pallas as pl
from jax.experimental.pallas import tpu as pltpu

N_HEADS = 8
HEAD_DIM = 64
N_KV_HEADS = 2
HPG = N_HEADS // N_KV_HEADS
L_CMP = 32
D_STRIDE = 16
CMP_HIDDEN = 256
L_SLC = 64
N_SEL = 16
WINDOW = 512
N_BUCKETS = 32
MAX_EXACT = N_BUCKETS // 2
MAX_DISTANCE = 128
CONV_WIDTH = 31
FFN_CONV_WIDTH = 3
EPS = 1e-6
FORCE = 1e4

LANES = 128
NEG = -1e30
M_INIT = -1e29
TQ = 256
CONV_HALO = 32
VMEM_LIMIT = 56 * 1024 * 1024

F32 = jnp.float32
BF16 = jnp.bfloat16


def _cparams(n_axes):
    return pltpu.CompilerParams(dimension_semantics=("arbitrary",) * n_axes,
                                vmem_limit_bytes=VMEM_LIMIT)


def _dot(a, b):
    return jnp.dot(a, b, preferred_element_type=F32)


def _dot_nt(a, b):
    return lax.dot_general(a, b, (((1,), (1,)), ((), ())), preferred_element_type=F32)


def _rms(xf, g):
    return xf * lax.rsqrt(jnp.mean(xf * xf, axis=-1, keepdims=True) + EPS) * g


def _sigmoid(x):
    return 1.0 / (1.0 + jnp.exp(-x))


def _gelu(x):
    return 0.5 * x * (1.0 + jnp.tanh(0.7978845608028654 * (x + 0.044715 * x * x * x)))


def _t5_bucket(dist):
    n = jnp.maximum(dist, 0)
    nf = jnp.maximum(n, MAX_EXACT).astype(F32)
    large = MAX_EXACT + (jnp.log(nf / MAX_EXACT) / math.log(MAX_DISTANCE / MAX_EXACT)
                         * (N_BUCKETS - MAX_EXACT)).astype(jnp.int32)
    large = jnp.minimum(large, N_BUCKETS - 1)
    return jnp.where(n < MAX_EXACT, n, large)


def _inproj_kernel(x_ref, g_ref, wu_ref, wq_ref, wkv_ref, wng_ref, wmg_ref,
                   a_ref, q_ref, kc_ref, vc_ref, ks_ref, vs_ref, kw_ref, vw_ref, ng_ref, mg_ref,
                   *, tm, tiles_per_seq, d_conv, n_sblk):
    r = pl.program_id(0)
    h = _rms(x_ref[...], g_ref[...]).astype(BF16)

    u = _dot(h, wu_ref[...])
    a_ref[...] = (u[:, :d_conv] * _sigmoid(u[:, d_conv:])).astype(BF16)

    zq = _dot(h, wq_ref[...])
    for hh in range(N_HEADS):
        q_ref[0, hh] = zq[:, hh * LANES:(hh + 1) * LANES].astype(BF16)

    zkv = _dot(h, wkv_ref[...])
    kc_ref[...] = zkv[:, 0:LANES].astype(BF16)
    vc_ref[...] = zkv[:, LANES:2 * LANES].astype(BF16)
    lane = lax.broadcasted_iota(jnp.int32, (tm, LANES), 1)
    row = lax.broadcasted_iota(jnp.int32, (tm, LANES), 0)
    spos = (r % tiles_per_seq) * tm + row
    hi = lane >= HEAD_DIM
    blk_tag = jnp.where(hi & ((lane - HEAD_DIM) == spos // L_SLC), -NEG, 0.0)
    ones_hi = jnp.where(hi, 1.0, 0.0)
    for g in range(N_KV_HEADS):
        base = 2 * LANES + g * LANES
        ks_ref[0, g] = (zkv[:, base:base + LANES] + blk_tag).astype(BF16)
        base += N_KV_HEADS * LANES
        vs_ref[0, g] = (zkv[:, base:base + LANES] + ones_hi).astype(BF16)
        base += N_KV_HEADS * LANES
        kw_ref[0, g] = zkv[:, base:base + LANES].astype(BF16)
        base += N_KV_HEADS * LANES
        vw_ref[0, g] = (zkv[:, base:base + LANES] + ones_hi).astype(BF16)
    del n_sblk

    ng_ref[...] = _sigmoid(_dot(h, wng_ref[...]))
    mg_ref[...] = _sigmoid(_dot(h, wmg_ref[...])).astype(BF16)


def _prep_inproj_weights(w_in, d_model):
    d_conv = d_model // 2
    n_conv = 2 * d_conv
    n_q = N_HEADS * HEAD_DIM
    n_kv = 6 * N_KV_HEADS * HEAD_DIM
    n_ng = 3 * N_HEADS
    o = 0
    wu = w_in[:, o:o + n_conv]; o += n_conv
    wq = w_in[:, o:o + n_q]; o += n_q
    wkv = w_in[:, o:o + n_kv]; o += n_kv
    wng = w_in[:, o:o + n_ng]; o += n_ng
    wmg = w_in[:, o:]
    zpad = jnp.zeros((d_model, HEAD_DIM), w_in.dtype)
    scale = 1.0 / math.sqrt(HEAD_DIM)
    wq_p = jnp.concatenate(
        [jnp.concatenate([wq[:, h * HEAD_DIM:(h + 1) * HEAD_DIM] * scale, zpad], axis=1)
         for h in range(N_HEADS)], axis=1)
    kvcols = [wkv[:, 0:LANES], wkv[:, LANES:2 * LANES]]
    for kind in range(2, 6):
        for g in range(N_KV_HEADS):
            c0 = kind * N_KV_HEADS * HEAD_DIM + g * HEAD_DIM
            kvcols += [wkv[:, c0:c0 + HEAD_DIM], zpad]
    wkv_p = jnp.concatenate(kvcols, axis=1)
    per_g = HPG * 3
    ngcols = []
    for g in range(N_KV_HEADS):
        ngcols += [wng[:, g * per_g:(g + 1) * per_g],
                   jnp.zeros((d_model, LANES - per_g), w_in.dtype)]
    wng_p = jnp.concatenate(ngcols, axis=1)
    return tuple(w.astype(BF16) for w in (wu, wq_p, wkv_p, wng_p, wmg))


def _inproj(x2, norm_g, weights, batch, seq, tm=512):
    t, d = x2.shape
    wu, wq, wkv, wng, wmg = weights
    d_conv = d // 2
    tiles_per_seq = seq // tm
    n_tiles = t // tm
    full = lambda w: pl.BlockSpec(w.shape, lambda r: (0, 0))
    row = lambda n: pl.BlockSpec((tm, n), lambda r: (r, 0))
    headed = lambda nh: pl.BlockSpec((1, nh, tm, LANES),
                                     lambda r: (r // tiles_per_seq, 0, r % tiles_per_seq, 0))
    kv_shape = jax.ShapeDtypeStruct((batch, N_KV_HEADS, seq, LANES), BF16)
    out_shape = (
        jax.ShapeDtypeStruct((t, d_conv), BF16),
        jax.ShapeDtypeStruct((batch, N_HEADS, seq, LANES), BF16),
        jax.ShapeDtypeStruct((t, LANES), BF16),
        jax.ShapeDtypeStruct((t, LANES), BF16),
        kv_shape, kv_shape, kv_shape, kv_shape,
        jax.ShapeDtypeStruct((t, N_KV_HEADS * LANES), F32),
        jax.ShapeDtypeStruct((t, 2 * d), BF16),
    )
    out_specs = (row(d_conv), headed(N_HEADS), row(LANES), row(LANES),
                 headed(N_KV_HEADS), headed(N_KV_HEADS), headed(N_KV_HEADS), headed(N_KV_HEADS),
                 row(N_KV_HEADS * LANES), row(2 * d))
    kern = functools.partial(_inproj_kernel, tm=tm, tiles_per_seq=tiles_per_seq, d_conv=d_conv,
                             n_sblk=seq // L_SLC)
    return pl.pallas_call(
        kern, out_shape=out_shape, grid=(n_tiles,),
        in_specs=[row(d), pl.BlockSpec((1, d), lambda r: (0, 0)),
                  full(wu), full(wq), full(wkv), full(wng), full(wmg)],
        out_specs=out_specs, compiler_params=_cparams(1), name="inproj",
    )(x2, norm_g.reshape(1, d), wu, wq, wkv, wng, wmg)


def _compress_kernel(rk_ref, rv_ref, pek_ref, pev_ref, w1k_ref, w1v_ref, w2k_ref, w2v_ref,
                     kcmp_ref, vcmp_ref, *, ncp):
    def one(r_ref, pe_ref, w1_ref, w2_ref, add_ones):
        r = r_ref[0]
        top = _dot(r, w1_ref[0])
        bot = _dot(r, w1_ref[1])
        pe_h = _dot(pe_ref[0], w1_ref[0]) + _dot(pe_ref[1], w1_ref[1])
        nxt = pltpu.roll(bot, ncp - 1, 0)
        rowi = lax.broadcasted_iota(jnp.int32, top.shape, 0)
        hid = top + jnp.where(rowi == ncp - 1, 0.0, nxt) + pe_h[0:1, :]
        out = _dot(_gelu(hid).astype(BF16), w2_ref[...])
        if add_ones:
            lane = lax.broadcasted_iota(jnp.int32, out.shape, 1)
            out = out + jnp.where((lane % LANES) >= HEAD_DIM, 1.0, 0.0)
        return out.astype(BF16)

    kcmp_ref[0] = one(rk_ref, pek_ref, w1k_ref, w2k_ref, False)
    vcmp_ref[0] = one(rv_ref, pev_ref, w1v_ref, w2v_ref, True)


def _prep_compress_weights(pe, w1, w2):
    half = L_CMP // 2
    w1r = w1.reshape(L_CMP, HEAD_DIM, CMP_HIDDEN)
    halves = []
    for part in range(2):
        wp = w1r[part * half:(part + 1) * half]
        big = jnp.zeros((half, N_KV_HEADS, HEAD_DIM, N_KV_HEADS, CMP_HIDDEN), w1.dtype)
        for g in range(N_KV_HEADS):
            big = big.at[:, g, :, g, :].set(wp)
        halves.append(big.reshape(half * N_KV_HEADS * HEAD_DIM, N_KV_HEADS * CMP_HIDDEN))
    w1big = jnp.stack(halves).astype(BF16)
    w2big = jnp.zeros((N_KV_HEADS, CMP_HIDDEN, N_KV_HEADS, LANES), w2.dtype)
    for g in range(N_KV_HEADS):
        w2big = w2big.at[g, :, g, :HEAD_DIM].set(w2)
    w2big = w2big.reshape(N_KV_HEADS * CMP_HIDDEN, N_KV_HEADS * LANES).astype(BF16)
    per = pe.reshape(2, half, 1, HEAD_DIM)
    pebig = jnp.broadcast_to(per, (2, half, N_KV_HEADS, HEAD_DIM)).reshape(2, 1, -1)
    pebig = jnp.broadcast_to(pebig, (2, 8, pebig.shape[-1])).astype(BF16)
    return pebig, w1big, w2big


def _compress(kc, vc, wk, wv, batch, seq):
    ncp = seq // D_STRIDE
    width = D_STRIDE * LANES
    rk = kc.reshape(batch, ncp, width)
    rv = vc.reshape(batch, ncp, width)
    pek, w1k, w2k = wk
    pev, w1v, w2v = wv
    c3 = lambda a: pl.BlockSpec(a.shape, lambda b: (0, 0, 0))
    c2 = lambda a: pl.BlockSpec(a.shape, lambda b: (0, 0))
    rspec = pl.BlockSpec((1, ncp, width), lambda b: (b, 0, 0))
    ospec = pl.BlockSpec((1, ncp, N_KV_HEADS * LANES), lambda b: (b, 0, 0))
    oshape = jax.ShapeDtypeStruct((batch, ncp, N_KV_HEADS * LANES), BF16)
    return pl.pallas_call(
        functools.partial(_compress_kernel, ncp=ncp), out_shape=(oshape, oshape), grid=(batch,),
        in_specs=[rspec, rspec, c3(pek), c3(pev), c3(w1k), c3(w1v), c2(w2k), c2(w2v)],
        out_specs=(ospec, ospec), compiler_params=_cparams(1), name="compress",
    )(rk, rv, pek, pev, w1k, w1v, w2k, w2v)


def _convmix_kernel(a_ref, halo_ref, ga_ref, cw_ref, cb_ref, lg_ref, lb_ref, wo_ref,
                    ya_ref, ext_ref, *, tm):
    i = pl.program_id(1)
    halo = halo_ref[0].astype(F32)
    ext_ref[0:CONV_HALO, :] = jnp.where(i == 0, 0.0, halo)
    ext_ref[CONV_HALO:, :] = a_ref[0].astype(F32)
    off = CONV_HALO - (CONV_WIDTH - 1)
    acc = jnp.zeros((tm, a_ref.shape[-1]), F32) + cb_ref[...]
    for j in range(CONV_WIDTH):
        acc = acc + cw_ref[j:j + 1, :] * ext_ref[off + j:off + j + tm, :]
    mu = jnp.mean(acc, axis=-1, keepdims=True)
    cen = acc - mu
    var = jnp.mean(cen * cen, axis=-1, keepdims=True)
    y = cen * lax.rsqrt(var + EPS) * lg_ref[...] + lb_ref[...]
    y = y * _sigmoid(y)
    yc = _dot(y.astype(BF16), wo_ref[...])
    ya_ref[...] = (ga_ref[...].astype(F32) * yc).astype(BF16)


def _convmix(a, mg, conv_w, conv_b, ln_g, ln_b, w_conv_out, batch, seq, tm=512):
    t, d_conv = a.shape
    d = w_conv_out.shape[1]
    a3 = a.reshape(batch, seq, d_conv)
    nt = seq // tm
    hb = tm // CONV_HALO
    cw = jnp.concatenate([conv_w, jnp.zeros((1, d_conv), conv_w.dtype)], axis=0)
    vec = lambda v: v.reshape(1, -1)
    c2 = lambda a_: pl.BlockSpec(a_.shape, lambda b, i: (0, 0))
    return pl.pallas_call(
        functools.partial(_convmix_kernel, tm=tm),
        out_shape=jax.ShapeDtypeStruct((t, d), BF16), grid=(batch, nt),
        in_specs=[pl.BlockSpec((1, tm, d_conv), lambda b, i: (b, i, 0)),
                  pl.BlockSpec((1, CONV_HALO, d_conv),
                               lambda b, i: (b, jnp.maximum(i * hb - 1, 0), 0)),
                  pl.BlockSpec((tm, d), lambda b, i: (b * nt + i, 0)),
                  c2(cw), c2(vec(conv_b)), c2(vec(ln_g)), c2(vec(ln_b)), c2(w_conv_out)],
        out_specs=pl.BlockSpec((tm, d), lambda b, i: (b * nt + i, 0)),
        scratch_shapes=[pltpu.VMEM((tm + CONV_HALO, d_conv), F32)],
        compiler_params=_cparams(2), name="convmix",
    )(a3, a3, mg, cw, vec(conv_b), vec(ln_g), vec(ln_b), w_conv_out)


def _nsa_kernel(b31_ref, q_ref, kcmp_ref, vcmp_ref, biasc_ref, ks_ref, vs_ref, kw_ref, vw_ref,
                ng_ref, aggt_ref, bd_ref, o_ref, imp_ref, *, ns):
    g = pl.program_id(1)
    i = pl.program_id(2)
    rows = HPG * TQ

    def split_heads(v):
        return v.reshape(HPG, TQ, v.shape[-1])

    def normalise(acc):
        l = acc[:, HEAD_DIM:HEAD_DIM + 1]
        return acc * jnp.where(l > 0.0, 1.0 / l, 0.0)

    q3 = q_ref[0]
    q2 = q3.reshape(rows, LANES)

    s_c = split_heads(_dot_nt(q2, kcmp_ref[0])) + biasc_ref[...]
    m_c = jnp.maximum(jnp.max(s_c, axis=-1, keepdims=True), M_INIT)
    p_c = jnp.exp(s_c - m_c)
    l_c = jnp.sum(p_c, axis=-1, keepdims=True)
    p_c = p_c * jnp.where(l_c > 0.0, 1.0 / l_c, 0.0)
    o_c = _dot(p_c.reshape(rows, -1).astype(BF16), vcmp_ref[0])

    p_sum = p_c[0] + p_c[1] + p_c[2] + p_c[3]
    p_hi = p_sum.astype(BF16)
    p_lo = (p_sum - p_hi.astype(F32)).astype(BF16)
    imp = _dot_nt(aggt_ref[...], p_hi) + _dot_nt(aggt_ref[...], p_lo)
    blk = lax.broadcasted_iota(jnp.int32, (ns, TQ), 0)
    cur = (i * TQ + lax.broadcasted_iota(jnp.int32, (ns, TQ), 1)) // L_SLC
    forced = (blk == 0) | (blk == cur) | (blk == cur - 1)
    imp = jnp.where(forced, FORCE, jnp.where(blk <= cur, imp, -FORCE))
    imp_ref[...] = imp
    rank = jnp.zeros((ns, TQ), F32)
    for k in range(ns):
        rowk = imp_ref[k:k + 1, :]
        ge = jnp.where(rowk >= imp, 1.0, 0.0)
        gt = jnp.where(rowk > imp, 1.0, 0.0)
        rank = rank + jnp.where(blk > k, ge, gt)
    unsel = jnp.where(rank < float(N_SEL), 0.0, -1.0)
    unsel_t = jnp.transpose(unsel)
    pieces = [jnp.zeros((TQ, HEAD_DIM), F32), unsel_t]
    if LANES - HEAD_DIM - ns > 0:
        pieces.append(jnp.zeros((TQ, LANES - HEAD_DIM - ns), F32))
    tag = jnp.concatenate(pieces, axis=1)
    qa = (q3 + tag.astype(BF16)[None]).reshape(rows, LANES)

    def head_bias(which):
        return bd_ref[:, which].reshape(rows, TQ)

    def scores(qmat, k_ref, jj):
        kt = k_ref[0, 0, pl.ds(pl.multiple_of(jj * TQ, TQ), TQ), :]
        return _dot_nt(qmat, kt)

    def vtile(v_ref, jj):
        return v_ref[0, 0, pl.ds(pl.multiple_of(jj * TQ, TQ), TQ), :]

    def far_step(jj, carry):
        m, acc = carry
        s = scores(qa, ks_ref, jj)
        m_new = jnp.maximum(m, jnp.max(s, axis=-1, keepdims=True))
        p = jnp.exp(s - m_new)
        acc = jnp.exp(m - m_new) * acc + _dot(p.astype(BF16), vtile(vs_ref, jj))
        return m_new, acc

    m0 = jnp.full((rows, 1), M_INIT, F32)
    acc0 = jnp.zeros((rows, LANES), F32)
    m_s, acc_s = lax.fori_loop(0, jnp.maximum(i - 1, 0), far_step, (m0, acc0))
    b31 = jnp.concatenate(
        [jnp.full((TQ, 1), b31_ref[g * HPG + hh], F32) for hh in range(HPG)], axis=0)
    m_s = m_s + b31

    j1 = jnp.maximum(i - 1, 0)
    pen1 = jnp.where(i >= 1, 0.0, NEG)
    s1 = scores(qa, ks_ref, j1) + head_bias(1) + pen1
    s0 = scores(qa, ks_ref, i) + head_bias(0)
    m_new = jnp.maximum(m_s, jnp.maximum(jnp.max(s1, axis=-1, keepdims=True),
                                         jnp.max(s0, axis=-1, keepdims=True)))
    acc_s = (jnp.exp(m_s - m_new) * acc_s
             + _dot(jnp.exp(s1 - m_new).astype(BF16), vtile(vs_ref, j1))
             + _dot(jnp.exp(s0 - m_new).astype(BF16), vtile(vs_ref, i)))
    o_s = normalise(acc_s)

    j2 = jnp.maximum(i - 2, 0)
    pen2 = jnp.where(i >= 2, 0.0, NEG)
    rr = lax.broadcasted_iota(jnp.int32, (TQ, TQ), 0)
    cc = lax.broadcasted_iota(jnp.int32, (TQ, TQ), 1)
    tri = jnp.where(cc > rr, 0.0, NEG)
    tri = jnp.concatenate([tri] * HPG, axis=0)
    w2 = scores(q2, kw_ref, j2) + tri + (b31 + pen2)
    w1 = scores(q2, kw_ref, j1) + head_bias(1) + pen1
    w0 = scores(q2, kw_ref, i) + head_bias(0)
    m_w = jnp.maximum(jnp.max(w2, axis=-1, keepdims=True),
                      jnp.maximum(jnp.max(w1, axis=-1, keepdims=True),
                                  jnp.max(w0, axis=-1, keepdims=True)))
    m_w = jnp.maximum(m_w, M_INIT)
    acc_w = (_dot(jnp.exp(w2 - m_w).astype(BF16), vtile(vw_ref, j2))
             + _dot(jnp.exp(w1 - m_w).astype(BF16), vtile(vw_ref, j1))
             + _dot(jnp.exp(w0 - m_w).astype(BF16), vtile(vw_ref, i)))
    o_w = normalise(acc_w)

    gates = ng_ref[...]
    o_c3, o_s3, o_w3 = split_heads(o_c), split_heads(o_s), split_heads(o_w)
    outs = []
    for hh in range(HPG):
        gc = gates[:, 3 * hh + 0:3 * hh + 1]
        gs = gates[:, 3 * hh + 1:3 * hh + 2]
        gw = gates[:, 3 * hh + 2:3 * hh + 3]
        oh = gc * o_c3[hh] + gs * o_s3[hh] + gw * o_w3[hh]
        outs.append(oh[:, :HEAD_DIM])
    o_ref[...] = jnp.concatenate(outs, axis=1).astype(BF16)


def _attention_tables(rel_bias, seq):
    ncp = seq // D_STRIDE
    nc = (seq - L_CMP) // D_STRIDE + 1
    ns = seq // L_SLC
    tab = rel_bias[_t5_bucket(jnp.arange(seq))].astype(F32)
    tpos = jnp.arange(seq)[:, None]
    c = jnp.arange(ncp)[None, :]
    dist_c = tpos - (c * D_STRIDE + L_CMP - 1)
    ok = (dist_c >= 0) & (c < nc)
    bias_c = jnp.where(ok[None], jnp.transpose(tab[jnp.clip(dist_c, 0, seq - 1)], (2, 0, 1)), NEG)
    r = jnp.arange(TQ)[:, None]
    cc = jnp.arange(TQ)[None, :]
    d0 = r - cc
    b0 = jnp.where((d0 >= 0)[None], jnp.transpose(tab[jnp.clip(d0, 0, seq - 1)], (2, 0, 1)), NEG)
    b1 = jnp.transpose(tab[TQ + d0], (2, 0, 1))
    bd = jnp.stack([b0, b1], axis=1)
    c_start = jnp.arange(ncp) * D_STRIDE
    c_end = c_start + L_CMP - 1
    s_start = jnp.arange(ns) * L_SLC
    aggt = ((c_end[None, :] >= s_start[:, None]) & (c_start[None, :] <= s_start[:, None] + L_SLC - 1)
            & (jnp.arange(ncp)[None, :] < nc)).astype(BF16)
    b31 = tab[seq - 1]
    return bias_c, bd, aggt, b31


def _nsa(q, kcmp, vcmp, ks, vs, kw, vw, ng, rel_bias, batch, seq):
    assert 2 * TQ - (TQ - 1) >= 113, "far tiles must lie in the largest distance bucket"
    ncp = seq // D_STRIDE
    ns = seq // L_SLC
    nt = seq // TQ
    bias_c, bd, aggt, b31 = _attention_tables(rel_bias, seq)
    kvspec = pl.BlockSpec((1, 1, seq, LANES), lambda b, g, i, *_: (b, g, 0, 0))
    cmpspec = pl.BlockSpec((1, ncp, LANES), lambda b, g, i, *_: (b, 0, g))
    grid_spec = pltpu.PrefetchScalarGridSpec(
        num_scalar_prefetch=1, grid=(batch, N_KV_HEADS, nt),
        in_specs=[pl.BlockSpec((1, HPG, TQ, LANES), lambda b, g, i, *_: (b, g, i, 0)),
                  cmpspec, cmpspec,
                  pl.BlockSpec((HPG, TQ, ncp), lambda b, g, i, *_: (g, i, 0)),
                  kvspec, kvspec, kvspec, kvspec,
                  pl.BlockSpec((TQ, LANES), lambda b, g, i, *_: (b * nt + i, g)),
                  pl.BlockSpec((ns, ncp), lambda b, g, i, *_: (0, 0)),
                  pl.BlockSpec((HPG, 2, TQ, TQ), lambda b, g, i, *_: (g, 0, 0, 0))],
        out_specs=pl.BlockSpec((TQ, HPG * HEAD_DIM), lambda b, g, i, *_: (b * nt + i, g)),
        scratch_shapes=[pltpu.VMEM((ns, TQ), F32)])
    return pl.pallas_call(
        functools.partial(_nsa_kernel, ns=ns),
        out_shape=jax.ShapeDtypeStruct((batch * seq, N_HEADS * HEAD_DIM), BF16),
        grid_spec=grid_spec, compiler_params=_cparams(3), name="nsa",
    )(b31, q, kcmp, vcmp, bias_c, ks, vs, kw, vw, ng, aggt, bd)


def _merge_kernel(x_ref, ya_ref, o_ref, gb_ref, wao_ref, wout_ref, x1_ref):
    y_attn = _dot(o_ref[...], wao_ref[...])
    y = ya_ref[...].astype(F32) + gb_ref[...].astype(F32) * y_attn
    x1_ref[...] = x_ref[...] + _dot(y.astype(BF16), wout_ref[...])


def _merge(x2, ya, o, mg, w_attn_out, w_out, tm=512):
    t, d = x2.shape
    c2 = lambda a: pl.BlockSpec(a.shape, lambda r: (0, 0))
    return pl.pallas_call(
        _merge_kernel, out_shape=jax.ShapeDtypeStruct((t, d), F32), grid=(t // tm,),
        in_specs=[pl.BlockSpec((tm, d), lambda r: (r, 0)),
                  pl.BlockSpec((tm, d), lambda r: (r, 0)),
                  pl.BlockSpec((tm, o.shape[1]), lambda r: (r, 0)),
                  pl.BlockSpec((tm, d), lambda r: (r, 1)),
                  c2(w_attn_out), c2(w_out)],
        out_specs=pl.BlockSpec((tm, d), lambda r: (r, 0)),
        compiler_params=_cparams(1), name="merge",
    )(x2, ya, o, mg, w_attn_out, w_out)


def _ffn_kernel(x1_ref, p_ref, nf_ref, wg_ref, wv_ref, cwg_ref, cwv_ref, cbg_ref, cbv_ref, wd_ref,
                np_ref, wpg_ref, wpe_ref, nfin_ref, out_ref,
                hf_ref, acc_ref, carry_g_ref, carry_v_ref, *, tm, tiles_per_seq, n_chunks):
    r = pl.program_id(0)
    c = pl.program_id(1)
    seq_start = (r % tiles_per_seq) == 0

    @pl.when(c == 0)
    def _():
        hf_ref[...] = _rms(x1_ref[...], nf_ref[...]).astype(BF16)
        acc_ref[...] = jnp.zeros_like(acc_ref)

    @pl.when(seq_start)
    def _():
        carry_g_ref[c] = jnp.zeros(carry_g_ref.shape[1:], F32)
        carry_v_ref[c] = jnp.zeros(carry_v_ref.shape[1:], F32)

    hf = hf_ref[...]

    def conv3(up, cw_ref, cb_ref, carry_ref):
        rowi = lax.broadcasted_iota(jnp.int32, up.shape, 0)
        prev = carry_ref[c]
        s1 = jnp.where(rowi == 0, prev[7:8, :], pltpu.roll(up, 1, 0))
        s2 = jnp.where(rowi == 0, prev[6:7, :],
                       jnp.where(rowi == 1, prev[7:8, :], pltpu.roll(up, 2, 0)))
        carry_ref[c] = up[tm - 8:, :]
        return cw_ref[0:1, :] * s2 + cw_ref[1:2, :] * s1 + cw_ref[2:3, :] * up + cb_ref[...]

    gate = conv3(_dot(hf, wg_ref[...]), cwg_ref, cbg_ref, carry_g_ref)
    val = conv3(_dot(hf, wv_ref[...]), cwv_ref, cbv_ref, carry_v_ref)
    acc_ref[...] += _dot((_gelu(gate) * val).astype(BF16), wd_ref[...])

    @pl.when(c == n_chunks - 1)
    def _():
        x2 = x1_ref[...] + acc_ref[...]
        pg = _sigmoid(_dot(_rms(x2, np_ref[...]).astype(BF16), wpg_ref[...]))
        pe = _dot(p_ref[...].astype(BF16), wpe_ref[...])
        out_ref[...] = _rms(x2 + pg * pe, nfin_ref[...])


def _ffn(x1, p2, norm_ffn, w_up, ffn_dw_w, ffn_dw_b, w_down, norm_ple, w_ple_gate, w_ple,
         norm_final, seq, tm=512, n_chunks=2):
    t, d = x1.shape
    d_ff = w_down.shape[0]
    fc = d_ff // n_chunks
    assert fc * n_chunks == d_ff and fc % LANES == 0
    tiles_per_seq = seq // tm
    cw = jnp.concatenate([ffn_dw_w, jnp.zeros((8 - FFN_CONV_WIDTH, 2 * d_ff), ffn_dw_w.dtype)], 0)
    cb = ffn_dw_b.reshape(1, -1)
    vec = lambda v: v.reshape(1, -1)
    c2 = lambda a: pl.BlockSpec(a.shape, lambda r, c: (0, 0))
    ncol = d_ff // fc
    kern = functools.partial(_ffn_kernel, tm=tm, tiles_per_seq=tiles_per_seq, n_chunks=n_chunks)
    return pl.pallas_call(
        kern, out_shape=jax.ShapeDtypeStruct((t, d), F32), grid=(t // tm, n_chunks),
        in_specs=[pl.BlockSpec((tm, d), lambda r, c: (r, 0)),
                  pl.BlockSpec((tm, p2.shape[1]), lambda r, c: (r, 0)),
                  c2(vec(norm_ffn)),
                  pl.BlockSpec((d, fc), lambda r, c: (0, c)),
                  pl.BlockSpec((d, fc), lambda r, c: (0, ncol + c)),
                  pl.BlockSpec((8, fc), lambda r, c: (0, c)),
                  pl.BlockSpec((8, fc), lambda r, c: (0, ncol + c)),
                  pl.BlockSpec((1, fc), lambda r, c: (0, c)),
                  pl.BlockSpec((1, fc), lambda r, c: (0, ncol + c)),
                  pl.BlockSpec((fc, d), lambda r, c: (c, 0)),
                  c2(vec(norm_ple)), c2(w_ple_gate), c2(w_ple), c2(vec(norm_final))],
        out_specs=pl.BlockSpec((tm, d), lambda r, c: (r, 0)),
        scratch_shapes=[pltpu.VMEM((tm, d), BF16), pltpu.VMEM((tm, d), F32),
                        pltpu.VMEM((n_chunks, 8, fc), F32), pltpu.VMEM((n_chunks, 8, fc), F32)],
        compiler_params=_cparams(2), name="ffn",
    )(x1, p2, vec(norm_ffn), w_up, w_up, cw, cw, cb, cb, w_down,
      vec(norm_ple), w_ple_gate, w_ple, vec(norm_final))


def kernel(x, p, rel_bias, norm_mix, w_in, conv_dw_w, conv_dw_b, conv_ln_g, conv_ln_b, w_conv_out,
           cmp_pe_k, cmp_pe_v, w_ck1, w_ck2, w_cv1, w_cv2, w_attn_out, w_out, norm_ffn, w_up,
           ffn_dw_w, ffn_dw_b, w_down, norm_ple, w_ple_gate, w_ple, norm_final):
    batch, seq, d = x.shape
    depth = w_in.shape[0]
    x2 = x.reshape(batch * seq, d)
    for i in range(depth):
        a, q, kc, vc, ks, vs, kw, vw, ng, mg = _inproj(
            x2, norm_mix[i], _prep_inproj_weights(w_in[i], d), batch, seq)
        kcmp, vcmp = _compress(kc, vc,
                               _prep_compress_weights(cmp_pe_k[i], w_ck1[i], w_ck2[i]),
                               _prep_compress_weights(cmp_pe_v[i], w_cv1[i], w_cv2[i]),
                               batch, seq)
        ya = _convmix(a, mg, conv_dw_w[i], conv_dw_b[i], conv_ln_g[i], conv_ln_b[i],
                      w_conv_out[i].astype(BF16), batch, seq)
        o = _nsa(q, kcmp, vcmp, ks, vs, kw, vw, ng, rel_bias, batch, seq)
        x1 = _merge(x2, ya, o, mg, w_attn_out[i].astype(BF16), w_out[i].astype(BF16))
        assert i == depth - 1, "the final RMSNorm is fused into the (single) layer's MLP kernel"
        x2 = _ffn(x1, p[i].reshape(batch * seq, -1), norm_ffn[i], w_up[i].astype(BF16),
                  ffn_dw_w[i], ffn_dw_b[i], w_down[i].astype(BF16), norm_ple[i],
                  w_ple_gate[i].astype(BF16), w_ple[i].astype(BF16), norm_final, seq)
    return x2.reshape(batch, seq, d)
```

```python
import functools
import math

import jax
import jax.numpy as jnp
import numpy as np
from jax import lax
from jax.experimental import pallas as pl
from jax.experimental.pallas import tpu as pltpu

N_HEADS = 8
HEAD_DIM = 64
N_KV_HEADS = 2
HPG = N_HEADS // N_KV_HEADS
L_CMP = 32
D_STRIDE = 16
CMP_HIDDEN = 256
L_SLC = 64
N_SEL = 16
WINDOW = 512
N_BUCKETS = 32
MAX_EXACT = N_BUCKETS // 2
MAX_DISTANCE = 128
CONV_WIDTH = 31
FFN_CONV_WIDTH = 3
EPS = 1e-6
FORCE = 1e4

LANES = 128
NEG = -1e30
M_INIT = -1e29
TQ = 256
CONV_HALO = 32
VMEM_LIMIT = 56 * 1024 * 1024

F32 = jnp.float32
BF16 = jnp.bfloat16


def _cparams(n_axes):
    return pltpu.CompilerParams(dimension_semantics=("arbitrary",) * n_axes,
                                vmem_limit_bytes=VMEM_LIMIT)


def _dot(a, b):
    return jnp.dot(a, b, preferred_element_type=F32)


def _dot_nt(a, b):
    return lax.dot_general(a, b, (((1,), (1,)), ((), ())), preferred_element_type=F32)


def _rms(xf, g):
    return xf * lax.rsqrt(jnp.mean(xf * xf, axis=-1, keepdims=True) + EPS) * g


def _sigmoid(x):
    return 1.0 / (1.0 + jnp.exp(-x))


def _gelu(x):
    return 0.5 * x * (1.0 + jnp.tanh(0.7978845608028654 * (x + 0.044715 * x * x * x)))


def _t5_bucket(dist):
    n = jnp.maximum(dist, 0)
    nf = jnp.maximum(n, MAX_EXACT).astype(F32)
    large = MAX_EXACT + (jnp.log(nf / MAX_EXACT) / math.log(MAX_DISTANCE / MAX_EXACT)
                         * (N_BUCKETS - MAX_EXACT)).astype(jnp.int32)
    large = jnp.minimum(large, N_BUCKETS - 1)
    return jnp.where(n < MAX_EXACT, n, large)


def _inproj_kernel(x_ref, g_ref, wu_ref, wq_ref, wkv_ref, wng_ref, wmg_ref,
                   a_ref, q_ref, kc_ref, vc_ref, ks_ref, vs_ref, kw_ref, vw_ref, ng_ref, mg_ref,
                   *, tm, tiles_per_seq, d_conv, n_sblk):
    r = pl.program_id(0)
    h = _rms(x_ref[...], g_ref[...]).astype(BF16)

    u = _dot(h, wu_ref[...])
    a_ref[...] = (u[:, :d_conv] * _sigmoid(u[:, d_conv:])).astype(BF16)

    zq = _dot(h, wq_ref[...])
    for hh in range(N_HEADS):
        q_ref[0, hh] = zq[:, hh * LANES:(hh + 1) * LANES].astype(BF16)

    zkv = _dot(h, wkv_ref[...])
    kc_ref[...] = zkv[:, 0:LANES].astype(BF16)
    vc_ref[...] = zkv[:, LANES:2 * LANES].astype(BF16)
    lane = lax.broadcasted_iota(jnp.int32, (tm, LANES), 1)
    row = lax.broadcasted_iota(jnp.int32, (tm, LANES), 0)
    spos = (r % tiles_per_seq) * tm + row
    hi = lane >= HEAD_DIM
    blk_tag = jnp.where(hi & ((lane - HEAD_DIM) == spos // L_SLC), -NEG, 0.0)
    ones_hi = jnp.where(hi, 1.0, 0.0)
    for g in range(N_KV_HEADS):
        base = 2 * LANES + g * LANES
        ks_ref[0, g] = (zkv[:, base:base + LANES] + blk_tag).astype(BF16)
        base += N_KV_HEADS * LANES
        vs_ref[0, g] = (zkv[:, base:base + LANES] + ones_hi).astype(BF16)
        base += N_KV_HEADS * LANES
        kw_ref[0, g] = zkv[:, base:base + LANES].astype(BF16)
        base += N_KV_HEADS * LANES
        vw_ref[0, g] = (zkv[:, base:base + LANES] + ones_hi).astype(BF16)
    del n_sblk

    ng_ref[...] = _sigmoid(_dot(h, wng_ref[...]))
    mg_ref[...] = _sigmoid(_dot(h, wmg_ref[...])).astype(BF16)


def _prep_inproj_weights(w_in, d_model):
    d_conv = d_model // 2
    n_conv = 2 * d_conv
    n_q = N_HEADS * HEAD_DIM
    n_kv = 6 * N_KV_HEADS * HEAD_DIM
    n_ng = 3 * N_HEADS
    o = 0
    wu = w_in[:, o:o + n_conv]; o += n_conv
    wq = w_in[:, o:o + n_q]; o += n_q
    wkv = w_in[:, o:o + n_kv]; o += n_kv
    wng = w_in[:, o:o + n_ng]; o += n_ng
    wmg = w_in[:, o:]
    zpad = jnp.zeros((d_model, HEAD_DIM), w_in.dtype)
    scale = 1.0 / math.sqrt(HEAD_DIM)
    wq_p = jnp.concatenate(
        [jnp.concatenate([wq[:, h * HEAD_DIM:(h + 1) * HEAD_DIM] * scale, zpad], axis=1)
         for h in range(N_HEADS)], axis=1)
    kvcols = [wkv[:, 0:LANES], wkv[:, LANES:2 * LANES]]
    for kind in range(2, 6):
        for g in range(N_KV_HEADS):
            c0 = kind * N_KV_HEADS * HEAD_DIM + g * HEAD_DIM
            kvcols += [wkv[:, c0:c0 + HEAD_DIM], zpad]
    wkv_p = jnp.concatenate(kvcols, axis=1)
    per_g = HPG * 3
    ngcols = []
    for g in range(N_KV_HEADS):
        ngcols += [wng[:, g * per_g:(g + 1) * per_g],
                   jnp.zeros((d_model, LANES - per_g), w_in.dtype)]
    wng_p = jnp.concatenate(ngcols, axis=1)
    return tuple(w.astype(BF16) for w in (wu, wq_p, wkv_p, wng_p, wmg))


def _inproj(x2, norm_g, weights, batch, seq, tm=512):
    t, d = x2.shape
    wu, wq, wkv, wng, wmg = weights
    d_conv = d // 2
    tiles_per_seq = seq // tm
    n_tiles = t // tm
    full = lambda w: pl.BlockSpec(w.shape, lambda r: (0, 0))
    row = lambda n: pl.BlockSpec((tm, n), lambda r: (r, 0))
    headed = lambda nh: pl.BlockSpec((1, nh, tm, LANES),
                                     lambda r: (r // tiles_per_seq, 0, r % tiles_per_seq, 0))
    kv_shape = jax.ShapeDtypeStruct((batch, N_KV_HEADS, seq, LANES), BF16)
    out_shape = (
        jax.ShapeDtypeStruct((t, d_conv), BF16),
        jax.ShapeDtypeStruct((batch, N_HEADS, seq, LANES), BF16),
        jax.ShapeDtypeStruct((t, LANES), BF16),
        jax.ShapeDtypeStruct((t, LANES), BF16),
        kv_shape, kv_shape, kv_shape, kv_shape,
        jax.ShapeDtypeStruct((t, N_KV_HEADS * LANES), F32),
        jax.ShapeDtypeStruct((t, 2 * d), BF16),
    )
    out_specs = (row(d_conv), headed(N_HEADS), row(LANES), row(LANES),
                 headed(N_KV_HEADS), headed(N_KV_HEADS), headed(N_KV_HEADS), headed(N_KV_HEADS),
                 row(N_KV_HEADS * LANES), row(2 * d))
    kern = functools.partial(_inproj_kernel, tm=tm, tiles_per_seq=tiles_per_seq, d_conv=d_conv,
                             n_sblk=seq // L_SLC)
    return pl.pallas_call(
        kern, out_shape=out_shape, grid=(n_tiles,),
        in_specs=[row(d), pl.BlockSpec((1, d), lambda r: (0, 0)),
                  full(wu), full(wq), full(wkv), full(wng), full(wmg)],
        out_specs=out_specs, compiler_params=_cparams(1), name="inproj",
    )(x2, norm_g.reshape(1, d), wu, wq, wkv, wng, wmg)


def _compress_kernel(rk_ref, rv_ref, pek_ref, pev_ref, w1k_ref, w1v_ref, w2k_ref, w2v_ref,
                     kcmp_ref, vcmp_ref, *, ncp):
    def one(r_ref, pe_ref, w1_ref, w2_ref, add_ones):
        r = r_ref[0]
        top = _dot(r, w1_ref[0])
        bot = _dot(r, w1_ref[1])
        pe_h = _dot(pe_ref[0], w1_ref[0]) + _dot(pe_ref[1], w1_ref[1])
        nxt = pltpu.roll(bot, ncp - 1, 0)
        rowi = lax.broadcasted_iota(jnp.int32, top.shape, 0)
        hid = top + jnp.where(rowi == ncp - 1, 0.0, nxt) + pe_h[0:1, :]
        out = _dot(_gelu(hid).astype(BF16), w2_ref[...])
        if add_ones:
            lane = lax.broadcasted_iota(jnp.int32, out.shape, 1)
            out = out + jnp.where((lane % LANES) >= HEAD_DIM, 1.0, 0.0)
        return out.astype(BF16)

    kcmp_ref[0] = one(rk_ref, pek_ref, w1k_ref, w2k_ref, False)
    vcmp_ref[0] = one(rv_ref, pev_ref, w1v_ref, w2v_ref, True)


def _prep_compress_weights(pe, w1, w2):
    half = L_CMP // 2
    w1r = w1.reshape(L_CMP, HEAD_DIM, CMP_HIDDEN)
    halves = []
    for part in range(2):
        wp = w1r[part * half:(part + 1) * half]
        big = jnp.zeros((half, N_KV_HEADS, HEAD_DIM, N_KV_HEADS, CMP_HIDDEN), w1.dtype)
        for g in range(N_KV_HEADS):
            big = big.at[:, g, :, g, :].set(wp)
        halves.append(big.reshape(half * N_KV_HEADS * HEAD_DIM, N_KV_HEADS * CMP_HIDDEN))
    w1big = jnp.stack(halves).astype(BF16)
    w2big = jnp.zeros((N_KV_HEADS, CMP_HIDDEN, N_KV_HEADS, LANES), w2.dtype)
    for g in range(N_KV_HEADS):
        w2big = w2big.at[g, :, g, :HEAD_DIM].set(w2)
    w2big = w2big.reshape(N_KV_HEADS * CMP_HIDDEN, N_KV_HEADS * LANES).astype(BF16)
    per = pe.reshape(2, half, 1, HEAD_DIM)
    pebig = jnp.broadcast_to(per, (2, half, N_KV_HEADS, HEAD_DIM)).reshape(2, 1, -1)
    pebig = jnp.broadcast_to(pebig, (2, 8, pebig.shape[-1])).astype(BF16)
    return pebig, w1big, w2big


def _compress(kc, vc, wk, wv, batch, seq):
    ncp = seq // D_STRIDE
    width = D_STRIDE * LANES
    rk = kc.reshape(batch, ncp, width)
    rv = vc.reshape(batch, ncp, width)
    pek, w1k, w2k = wk
    pev, w1v, w2v = wv
    c3 = lambda a: pl.BlockSpec(a.shape, lambda b: (0, 0, 0))
    c2 = lambda a: pl.BlockSpec(a.shape, lambda b: (0, 0))
    rspec = pl.BlockSpec((1, ncp, width), lambda b: (b, 0, 0))
    ospec = pl.BlockSpec((1, ncp, N_KV_HEADS * LANES), lambda b: (b, 0, 0))
    oshape = jax.ShapeDtypeStruct((batch, ncp, N_KV_HEADS * LANES), BF16)
    return pl.pallas_call(
        functools.partial(_compress_kernel, ncp=ncp), out_shape=(oshape, oshape), grid=(batch,),
        in_specs=[rspec, rspec, c3(pek), c3(pev), c3(w1k), c3(w1v), c2(w2k), c2(w2v)],
        out_specs=(ospec, ospec), compiler_params=_cparams(1), name="compress",
    )(rk, rv, pek, pev, w1k, w1v, w2k, w2v)


def _convmix_kernel(a_ref, halo_ref, ga_ref, cw_ref, cb_ref, lg_ref, lb_ref, wo_ref,
                    ya_ref, ext_ref, *, tm):
    i = pl.program_id(1)
    halo = halo_ref[0].astype(F32)
    ext_ref[0:CONV_HALO, :] = jnp.where(i == 0, 0.0, halo)
    ext_ref[CONV_HALO:, :] = a_ref[0].astype(F32)
    off = CONV_HALO - (CONV_WIDTH - 1)
    acc = jnp.zeros((tm, a_ref.shape[-1]), F32) + cb_ref[...]
    for j in range(CONV_WIDTH):
        acc = acc + cw_ref[j:j + 1, :] * ext_ref[off + j:off + j + tm, :]
    mu = jnp.mean(acc, axis=-1, keepdims=True)
    cen = acc - mu
    var = jnp.mean(cen * cen, axis=-1, keepdims=True)
    y = cen * lax.rsqrt(var + EPS) * lg_ref[...] + lb_ref[...]
    y = y * _sigmoid(y)
    yc = _dot(y.astype(BF16), wo_ref[...])
    ya_ref[...] = (ga_ref[...].astype(F32) * yc).astype(BF16)


def _convmix(a, mg, conv_w, conv_b, ln_g, ln_b, w_conv_out, batch, seq, tm=512):
    t, d_conv = a.shape
    d = w_conv_out.shape[1]
    a3 = a.reshape(batch, seq, d_conv)
    nt = seq // tm
    hb = tm // CONV_HALO
    cw = jnp.concatenate([conv_w, jnp.zeros((1, d_conv), conv_w.dtype)], axis=0)
    vec = lambda v: v.reshape(1, -1)
    c2 = lambda a_: pl.BlockSpec(a_.shape, lambda b, i: (0, 0))
    return pl.pallas_call(
        functools.partial(_convmix_kernel, tm=tm),
        out_shape=jax.ShapeDtypeStruct((t, d), BF16), grid=(batch, nt),
        in_specs=[pl.BlockSpec((1, tm, d_conv), lambda b, i: (b, i, 0)),
                  pl.BlockSpec((1, CONV_HALO, d_conv),
                               lambda b, i: (b, jnp.maximum(i * hb - 1, 0), 0)),
                  pl.BlockSpec((tm, d), lambda b, i: (b * nt + i, 0)),
                  c2(cw), c2(vec(conv_b)), c2(vec(ln_g)), c2(vec(ln_b)), c2(w_conv_out)],
        out_specs=pl.BlockSpec((tm, d), lambda b, i: (b * nt + i, 0)),
        scratch_shapes=[pltpu.VMEM((tm + CONV_HALO, d_conv), F32)],
        compiler_params=_cparams(2), name="convmix",
    )(a3, a3, mg, cw, vec(conv_b), vec(ln_g), vec(ln_b), w_conv_out)


def _nsa_kernel(b31_ref, q_ref, kcmp_ref, vcmp_ref, biasc_ref, ks_ref, vs_ref, kw_ref, vw_ref,
                ng_ref, aggt_ref, bd_ref, o_ref, imp_ref, *, ns):
    g = pl.program_id(1)
    i = pl.program_id(2)
    rows = HPG * TQ

    def split_heads(v):
        return v.reshape(HPG, TQ, v.shape[-1])

    def normalise(acc):
        l = acc[:, HEAD_DIM:HEAD_DIM + 1]
        return acc * jnp.where(l > 0.0, 1.0 / l, 0.0)

    q3 = q_ref[0]
    q2 = q3.reshape(rows, LANES)

    s_c = split_heads(_dot_nt(q2, kcmp_ref[0])) + biasc_ref[...]
    m_c = jnp.maximum(jnp.max(s_c, axis=-1, keepdims=True), M_INIT)
    p_c = jnp.exp(s_c - m_c)
    l_c = jnp.sum(p_c, axis=-1, keepdims=True)
    p_c = p_c * jnp.where(l_c > 0.0, 1.0 / l_c, 0.0)
    o_c = _dot(p_c.reshape(rows, -1).astype(BF16), vcmp_ref[0])

    p_sum = p_c[0] + p_c[1] + p_c[2] + p_c[3]
    p_hi = p_sum.astype(BF16)
    p_lo = (p_sum - p_hi.astype(F32)).astype(BF16)
    imp = _dot_nt(aggt_ref[...], p_hi) + _dot_nt(aggt_ref[...], p_lo)
    blk = lax.broadcasted_iota(jnp.int32, (ns, TQ), 0)
    cur = (i * TQ + lax.broadcasted_iota(jnp.int32, (ns, TQ), 1)) // L_SLC
    forced = (blk == 0) | (blk == cur) | (blk == cur - 1)
    imp = jnp.where(forced, FORCE, jnp.where(blk <= cur, imp, -FORCE))
    imp_ref[...] = imp
    rank = jnp.zeros((ns, TQ), F32)
    for k in range(ns):
        rowk = imp_ref[k:k + 1, :]
        ge = jnp.where(rowk >= imp, 1.0, 0.0)
        gt = jnp.where(rowk > imp, 1.0, 0.0)
        rank = rank + jnp.where(blk > k, ge, gt)
    unsel = jnp.where(rank < float(N_SEL), 0.0, -1.0)
    unsel_t = jnp.transpose(unsel)
    pieces = [jnp.zeros((TQ, HEAD_DIM), F32), unsel_t]
    if LANES - HEAD_DIM - ns > 0:
        pieces.append(jnp.zeros((TQ, LANES - HEAD_DIM - ns), F32))
    tag = jnp.concatenate(pieces, axis=1)
    qa = (q3 + tag.astype(BF16)[None]).reshape(rows, LANES)

    def head_bias(which):
        return bd_ref[:, which].reshape(rows, TQ)

    def scores(qmat, k_ref, jj):
        kt = k_ref[0, 0, pl.ds(pl.multiple_of(jj * TQ, TQ), TQ), :]
        return _dot_nt(qmat, kt)

    def vtile(v_ref, jj):
        return v_ref[0, 0, pl.ds(pl.multiple_of(jj * TQ, TQ), TQ), :]

    def far_step(jj, carry):
        m, acc = carry
        s = scores(qa, ks_ref, jj)
        m_new = jnp.maximum(m, jnp.max(s, axis=-1, keepdims=True))
        p = jnp.exp(s - m_new)
        acc = jnp.exp(m - m_new) * acc + _dot(p.astype(BF16), vtile(vs_ref, jj))
        return m_new, acc

    m0 = jnp.full((rows, 1), M_INIT, F32)
    acc0 = jnp.zeros((rows, LANES), F32)
    m_s, acc_s = lax.fori_loop(0, jnp.maximum(i - 1, 0), far_step, (m0, acc0))
    b31 = jnp.concatenate(
        [jnp.full((TQ, 1), b31_ref[g * HPG + hh], F32) for hh in range(HPG)], axis=0)
    m_s = m_s + b31

    j1 = jnp.maximum(i - 1, 0)
    pen1 = jnp.where(i >= 1, 0.0, NEG)
    s1 = scores(qa, ks_ref, j1) + head_bias(1) + pen1
    s0 = scores(qa, ks_ref, i) + head_bias(0)
    m_new = jnp.maximum(m_s, jnp.maximum(jnp.max(s1, axis=-1, keepdims=True),
                                         jnp.max(s0, axis=-1, keepdims=True)))
    acc_s = (jnp.exp(m_s - m_new) * acc_s
             + _dot(jnp.exp(s1 - m_new).astype(BF16), vtile(vs_ref, j1))
             + _dot(jnp.exp(s0 - m_new).astype(BF16), vtile(vs_ref, i)))
    o_s = normalise(acc_s)

    j2 = jnp.maximum(i - 2, 0)
    pen2 = jnp.where(i >= 2, 0.0, NEG)
    rr = lax.broadcasted_iota(jnp.int32, (TQ, TQ), 0)
    cc = lax.broadcasted_iota(jnp.int32, (TQ, TQ), 1)
    tri = jnp.where(cc > rr, 0.0, NEG)
    tri = jnp.concatenate([tri] * HPG, axis=0)
    w2 = scores(q2, kw_ref, j2) + tri + (b31 + pen2)
    w1 = scores(q2, kw_ref, j1) + head_bias(1) + pen1
    w0 = scores(q2, kw_ref, i) + head_bias(0)
    m_w = jnp.maximum(jnp.max(w2, axis=-1, keepdims=True),
                      jnp.maximum(jnp.max(w1, axis=-1, keepdims=True),
                                  jnp.max(w0, axis=-1, keepdims=True)))
    m_w = jnp.maximum(m_w, M_INIT)
    acc_w = (_dot(jnp.exp(w2 - m_w).astype(BF16), vtile(vw_ref, j2))
             + _dot(jnp.exp(w1 - m_w).astype(BF16), vtile(vw_ref, j1))
             + _dot(jnp.exp(w0 - m_w).astype(BF16), vtile(vw_ref, i)))
    o_w = normalise(acc_w)

    gates = ng_ref[...]
    o_c3, o_s3, o_w3 = split_heads(o_c), split_heads(o_s), split_heads(o_w)
    outs = []
    for hh in range(HPG):
        gc = gates[:, 3 * hh + 0:3 * hh + 1]
        gs = gates[:, 3 * hh + 1:3 * hh + 2]
        gw = gates[:, 3 * hh + 2:3 * hh + 3]
        oh = gc * o_c3[hh] + gs * o_s3[hh] + gw * o_w3[hh]
        outs.append(oh[:, :HEAD_DIM])
    o_ref[...] = jnp.concatenate(outs, axis=1).astype(BF16)


def _bias_of(thr_ref, rb_ref, dist, h):
    b = jnp.full(dist.shape, rb_ref[h], F32)
    for k in range(1, N_BUCKETS):
        b = jnp.where(dist >= thr_ref[k], rb_ref[k * N_HEADS + h], b)
    return b


def _cmp_bias_kernel(thr_ref, rb_ref, biasc_ref, *, nc, ncp):
    i = pl.program_id(0)
    r = lax.broadcasted_iota(jnp.int32, (TQ, ncp), 0)
    c = lax.broadcasted_iota(jnp.int32, (TQ, ncp), 1)
    dist_c = i * TQ + r - (c * D_STRIDE + L_CMP - 1)
    ok_c = (dist_c >= 0) & (c < nc)
    for h in range(N_HEADS):
        biasc_ref[h] = jnp.where(ok_c, _bias_of(thr_ref, rb_ref, dist_c, h), NEG)


def _diag_bias_kernel(thr_ref, rb_ref, bd_ref):
    rr = lax.broadcasted_iota(jnp.int32, (TQ, TQ), 0)
    cc = lax.broadcasted_iota(jnp.int32, (TQ, TQ), 1)
    d0 = rr - cc
    for h in range(N_HEADS):
        bd_ref[h, 0] = jnp.where(d0 >= 0, _bias_of(thr_ref, rb_ref, d0, h), NEG)
        bd_ref[h, 1] = _bias_of(thr_ref, rb_ref, d0 + TQ, h)


def _attention_tables(rel_bias, seq):
    ncp = seq // D_STRIDE
    nc = (seq - L_CMP) // D_STRIDE + 1
    ns = seq // L_SLC
    nt = seq // TQ
    n_probe = 2 * MAX_DISTANCE
    buckets = _t5_bucket(jnp.arange(n_probe))
    thr = jnp.sum(buckets[None, :] < jnp.arange(N_BUCKETS)[:, None], axis=1).astype(jnp.int32)
    rb = rel_bias.astype(F32).reshape(-1)
    bias_c = pl.pallas_call(
        functools.partial(_cmp_bias_kernel, nc=nc, ncp=ncp),
        out_shape=jax.ShapeDtypeStruct((N_HEADS, seq, ncp), F32),
        grid_spec=pltpu.PrefetchScalarGridSpec(
            num_scalar_prefetch=2, grid=(nt,), in_specs=[],
            out_specs=pl.BlockSpec((N_HEADS, TQ, ncp), lambda i, *_: (0, i, 0))),
        compiler_params=_cparams(1), name="cmp_bias")(thr, rb)
    bd = pl.pallas_call(
        _diag_bias_kernel, out_shape=jax.ShapeDtypeStruct((N_HEADS, 2, TQ, TQ), F32),
        grid_spec=pltpu.PrefetchScalarGridSpec(
            num_scalar_prefetch=2, grid=(1,), in_specs=[],
            out_specs=pl.BlockSpec((N_HEADS, 2, TQ, TQ), lambda i, *_: (0, 0, 0, 0))),
        compiler_params=_cparams(1), name="diag_bias")(thr, rb)
    c_start = jnp.arange(ncp) * D_STRIDE
    c_end = c_start + L_CMP - 1
    s_start = jnp.arange(ns) * L_SLC
    aggt = ((c_end[None, :] >= s_start[:, None]) & (c_start[None, :] <= s_start[:, None] + L_SLC - 1)
            & (jnp.arange(ncp)[None, :] < nc)).astype(BF16)
    return bias_c, bd, aggt, rel_bias[N_BUCKETS - 1].astype(F32)


def _nsa(q, kcmp, vcmp, ks, vs, kw, vw, ng, rel_bias, batch, seq):
    first_far = TQ + 1
    assert MAX_EXACT + int(math.log(first_far / MAX_EXACT) / math.log(MAX_DISTANCE / MAX_EXACT)
                           * (N_BUCKETS - MAX_EXACT)) >= N_BUCKETS - 1, "far tiles need one bucket"
    ncp = seq // D_STRIDE
    ns = seq // L_SLC
    nt = seq // TQ
    bias_c, bd, aggt, b31 = _attention_tables(rel_bias, seq)
    kvspec = pl.BlockSpec((1, 1, seq, LANES), lambda b, g, i, *_: (b, g, 0, 0))
    cmpspec = pl.BlockSpec((1, ncp, LANES), lambda b, g, i, *_: (b, 0, g))
    grid_spec = pltpu.PrefetchScalarGridSpec(
        num_scalar_prefetch=1, grid=(batch, N_KV_HEADS, nt),
        in_specs=[pl.BlockSpec((1, HPG, TQ, LANES), lambda b, g, i, *_: (b, g, i, 0)),
                  cmpspec, cmpspec,
                  pl.BlockSpec((HPG, TQ, ncp), lambda b, g, i, *_: (g, i, 0)),
                  kvspec, kvspec, kvspec, kvspec,
                  pl.BlockSpec((TQ, LANES), lambda b, g, i, *_: (b * nt + i, g)),
                  pl.BlockSpec((ns, ncp), lambda b, g, i, *_: (0, 0)),
                  pl.BlockSpec((HPG, 2, TQ, TQ), lambda b, g, i, *_: (g, 0, 0, 0))],
        out_specs=pl.BlockSpec((TQ, HPG * HEAD_DIM), lambda b, g, i, *_: (b * nt + i, g)),
        scratch_shapes=[pltpu.VMEM((ns, TQ), F32)])
    return pl.pallas_call(
        functools.partial(_nsa_kernel, ns=ns),
        out_shape=jax.ShapeDtypeStruct((batch * seq, N_HEADS * HEAD_DIM), BF16),
        grid_spec=grid_spec, compiler_params=_cparams(3), name="nsa",
    )(b31, q, kcmp, vcmp, bias_c, ks, vs, kw, vw, ng, aggt, bd)


def _merge_kernel(x_ref, ya_ref, o_ref, gb_ref, wao_ref, wout_ref, x1_ref):
    y_attn = _dot(o_ref[...], wao_ref[...])
    y = ya_ref[...].astype(F32) + gb_ref[...].astype(F32) * y_attn
    x1_ref[...] = x_ref[...] + _dot(y.astype(BF16), wout_ref[...])


def _merge(x2, ya, o, mg, w_attn_out, w_out, tm=512):
    t, d = x2.shape
    c2 = lambda a: pl.BlockSpec(a.shape, lambda r: (0, 0))
    return pl.pallas_call(
        _merge_kernel, out_shape=jax.ShapeDtypeStruct((t, d), F32), grid=(t // tm,),
        in_specs=[pl.BlockSpec((tm, d), lambda r: (r, 0)),
                  pl.BlockSpec((tm, d), lambda r: (r, 0)),
                  pl.BlockSpec((tm, o.shape[1]), lambda r: (r, 0)),
                  pl.BlockSpec((tm, d), lambda r: (r, 1)),
                  c2(w_attn_out), c2(w_out)],
        out_specs=pl.BlockSpec((tm, d), lambda r: (r, 0)),
        compiler_params=_cparams(1), name="merge",
    )(x2, ya, o, mg, w_attn_out, w_out)


def _ffn_kernel(x1_ref, p_ref, nf_ref, wg_ref, wv_ref, cwg_ref, cwv_ref, cbg_ref, cbv_ref, wd_ref,
                np_ref, wpg_ref, wpe_ref, nfin_ref, out_ref,
                hf_ref, acc_ref, carry_g_ref, carry_v_ref, *, tm, tiles_per_seq, n_chunks):
    r = pl.program_id(0)
    c = pl.program_id(1)
    seq_start = (r % tiles_per_seq) == 0

    @pl.when(c == 0)
    def _():
        hf_ref[...] = _rms(x1_ref[...], nf_ref[...]).astype(BF16)
        acc_ref[...] = jnp.zeros_like(acc_ref)

    @pl.when(seq_start)
    def _():
        carry_g_ref[c] = jnp.zeros(carry_g_ref.shape[1:], F32)
        carry_v_ref[c] = jnp.zeros(carry_v_ref.shape[1:], F32)

    hf = hf_ref[...]

    def conv3(up, cw_ref, cb_ref, carry_ref):
        rowi = lax.broadcasted_iota(jnp.int32, up.shape, 0)
        prev = carry_ref[c]
        s1 = jnp.where(rowi == 0, prev[7:8, :], pltpu.roll(up, 1, 0))
        s2 = jnp.where(rowi == 0, prev[6:7, :],
                       jnp.where(rowi == 1, prev[7:8, :], pltpu.roll(up, 2, 0)))
        carry_ref[c] = up[tm - 8:, :]
        return cw_ref[0:1, :] * s2 + cw_ref[1:2, :] * s1 + cw_ref[2:3, :] * up + cb_ref[...]

    gate = conv3(_dot(hf, wg_ref[...]), cwg_ref, cbg_ref, carry_g_ref)
    val = conv3(_dot(hf, wv_ref[...]), cwv_ref, cbv_ref, carry_v_ref)
    acc_ref[...] += _dot((_gelu(gate) * val).astype(BF16), wd_ref[...])

    @pl.when(c == n_chunks - 1)
    def _():
        x2 = x1_ref[...] + acc_ref[...]
        pg = _sigmoid(_dot(_rms(x2, np_ref[...]).astype(BF16), wpg_ref[...]))
        pe = _dot(p_ref[...].astype(BF16), wpe_ref[...])
        out_ref[...] = _rms(x2 + pg * pe, nfin_ref[...])


def _ffn(x1, p2, norm_ffn, w_up, ffn_dw_w, ffn_dw_b, w_down, norm_ple, w_ple_gate, w_ple,
         norm_final, seq, tm=512, n_chunks=2):
    t, d = x1.shape
    d_ff = w_down.shape[0]
    fc = d_ff // n_chunks
    assert fc * n_chunks == d_ff and fc % LANES == 0
    tiles_per_seq = seq // tm
    cw = jnp.concatenate([ffn_dw_w, jnp.zeros((8 - FFN_CONV_WIDTH, 2 * d_ff), ffn_dw_w.dtype)], 0)
    cb = ffn_dw_b.reshape(1, -1)
    vec = lambda v: v.reshape(1, -1)
    c2 = lambda a: pl.BlockSpec(a.shape, lambda r, c: (0, 0))
    ncol = d_ff // fc
    kern = functools.partial(_ffn_kernel, tm=tm, tiles_per_seq=tiles_per_seq, n_chunks=n_chunks)
    return pl.pallas_call(
        kern, out_shape=jax.ShapeDtypeStruct((t, d), F32), grid=(t // tm, n_chunks),
        in_specs=[pl.BlockSpec((tm, d), lambda r, c: (r, 0)),
                  pl.BlockSpec((tm, p2.shape[1]), lambda r, c: (r, 0)),
                  c2(vec(norm_ffn)),
                  pl.BlockSpec((d, fc), lambda r, c: (0, c)),
                  pl.BlockSpec((d, fc), lambda r, c: (0, ncol + c)),
                  pl.BlockSpec((8, fc), lambda r, c: (0, c)),
                  pl.BlockSpec((8, fc), lambda r, c: (0, ncol + c)),
                  pl.BlockSpec((1, fc), lambda r, c: (0, c)),
                  pl.BlockSpec((1, fc), lambda r, c: (0, ncol + c)),
                  pl.BlockSpec((fc, d), lambda r, c: (c, 0)),
                  c2(vec(norm_ple)), c2(w_ple_gate), c2(w_ple), c2(vec(norm_final))],
        out_specs=pl.BlockSpec((tm, d), lambda r, c: (r, 0)),
        scratch_shapes=[pltpu.VMEM((tm, d), BF16), pltpu.VMEM((tm, d), F32),
                        pltpu.VMEM((n_chunks, 8, fc), F32), pltpu.VMEM((n_chunks, 8, fc), F32)],
        compiler_params=_cparams(2), name="ffn",
    )(x1, p2, vec(norm_ffn), w_up, w_up, cw, cw, cb, cb, w_down,
      vec(norm_ple), w_ple_gate, w_ple, vec(norm_final))


def kernel(x, p, rel_bias, norm_mix, w_in, conv_dw_w, conv_dw_b, conv_ln_g, conv_ln_b, w_conv_out,
           cmp_pe_k, cmp_pe_v, w_ck1, w_ck2, w_cv1, w_cv2, w_attn_out, w_out, norm_ffn, w_up,
           ffn_dw_w, ffn_dw_b, w_down, norm_ple, w_ple_gate, w_ple, norm_final):
    batch, seq, d = x.shape
    depth = w_in.shape[0]
    x2 = x.reshape(batch * seq, d)
    for i in range(depth):
        a, q, kc, vc, ks, vs, kw, vw, ng, mg = _inproj(
            x2, norm_mix[i], _prep_inproj_weights(w_in[i], d), batch, seq)
        kcmp, vcmp = _compress(kc, vc,
                               _prep_compress_weights(cmp_pe_k[i], w_ck1[i], w_ck2[i]),
                               _prep_compress_weights(cmp_pe_v[i], w_cv1[i], w_cv2[i]),
                               batch, seq)
        ya = _convmix(a, mg, conv_dw_w[i], conv_dw_b[i], conv_ln_g[i], conv_ln_b[i],
                      w_conv_out[i].astype(BF16), batch, seq)
        o = _nsa(q, kcmp, vcmp, ks, vs, kw, vw, ng, rel_bias, batch, seq)
        x1 = _merge(x2, ya, o, mg, w_attn_out[i].astype(BF16), w_out[i].astype(BF16))
        assert i == depth - 1, "the final RMSNorm is fused into the (single) layer's MLP kernel"
        x2 = _ffn(x1, p[i].reshape(batch * seq, -1), norm_ffn[i], w_up[i].astype(BF16),
                  ffn_dw_w[i], ffn_dw_b[i], w_down[i].astype(BF16), norm_ple[i],
                  w_ple_gate[i].astype(BF16), w_ple[i].astype(BF16), norm_final, seq)
    return x2.reshape(batch, seq, d)
```

```python
import functools
import math

import jax
import jax.numpy as jnp
import numpy as np
from jax import lax
from jax.experimental import pallas as pl
from jax.experimental.pallas import tpu as pltpu

N_HEADS = 8
HEAD_DIM = 64
N_KV_HEADS = 2
HPG = N_HEADS // N_KV_HEADS
L_CMP = 32
D_STRIDE = 16
CMP_HIDDEN = 256
L_SLC = 64
N_SEL = 16
WINDOW = 512
N_BUCKETS = 32
MAX_EXACT = N_BUCKETS // 2
MAX_DISTANCE = 128
CONV_WIDTH = 31
FFN_CONV_WIDTH = 3
EPS = 1e-6
FORCE = 1e4

LANES = 128
NEG = -1e30
M_INIT = -1e29
LOG2E = 1.4426950408889634
PIPE_DEPTH = 4
TQ = 256
CONV_HALO = 32
VMEM_LIMIT = 56 * 1024 * 1024

F32 = jnp.float32
BF16 = jnp.bfloat16


def _cparams(n_axes):
    return pltpu.CompilerParams(dimension_semantics=("arbitrary",) * n_axes,
                                vmem_limit_bytes=VMEM_LIMIT)


def _dot(a, b):
    return jnp.dot(a, b, preferred_element_type=F32)


def _dot_nt(a, b):
    return lax.dot_general(a, b, (((1,), (1,)), ((), ())), preferred_element_type=F32)


def _rms(xf, g):
    return xf * lax.rsqrt(jnp.mean(xf * xf, axis=-1, keepdims=True) + EPS) * g


def _sigmoid(x):
    return 1.0 / (1.0 + jnp.exp(-x))


def _gelu(x):
    return 0.5 * x * (1.0 + jnp.tanh(0.7978845608028654 * (x + 0.044715 * x * x * x)))


def _t5_bucket(dist):
    n = jnp.maximum(dist, 0)
    nf = jnp.maximum(n, MAX_EXACT).astype(F32)
    large = MAX_EXACT + (jnp.log(nf / MAX_EXACT) / math.log(MAX_DISTANCE / MAX_EXACT)
                         * (N_BUCKETS - MAX_EXACT)).astype(jnp.int32)
    large = jnp.minimum(large, N_BUCKETS - 1)
    return jnp.where(n < MAX_EXACT, n, large)


def _inproj_kernel(x_ref, g_ref, wu_ref, wq_ref, wkv_ref, wng_ref, wmg_ref,
                   a_ref, qt_ref, kc_ref, vc_ref, ks_ref, vst_ref, kw_ref, vwt_ref, ng_ref, mg_ref,
                   *, tm, tiles_per_seq, d_conv):
    r = pl.program_id(0)
    h = _rms(x_ref[...], g_ref[...]).astype(BF16)

    u = _dot(h, wu_ref[...])
    a_ref[...] = (u[:, :d_conv] * _sigmoid(u[:, d_conv:])).astype(BF16)

    zqt = jnp.transpose(_dot(h, wq_ref[...]))
    zero_rows = jnp.zeros((HEAD_DIM, tm), BF16)
    for hh in range(N_HEADS):
        qt_ref[0, hh, 0:HEAD_DIM, :] = zqt[hh * HEAD_DIM:(hh + 1) * HEAD_DIM].astype(BF16)
        qt_ref[0, hh, HEAD_DIM:, :] = zero_rows

    zkv = _dot(h, wkv_ref[...])
    kc_ref[...] = zkv[:, 0:LANES].astype(BF16)
    vc_ref[...] = zkv[:, LANES:2 * LANES].astype(BF16)
    lane = lax.broadcasted_iota(jnp.int32, (tm, LANES), 1)
    row = lax.broadcasted_iota(jnp.int32, (tm, LANES), 0)
    spos = (r % tiles_per_seq) * tm + row
    lo = lane < HEAD_DIM
    blk_tag = jnp.where((lane - HEAD_DIM) == spos // L_SLC, -NEG, 0.0)
    ones_rows = jnp.where(lax.broadcasted_iota(jnp.int32, (HEAD_DIM, tm), 0) < 8, 1.0, 0.0)
    ones_rows = ones_rows.astype(BF16)
    for g in range(N_KV_HEADS):
        for k_ref, vt_ref, base, tag in ((ks_ref, vst_ref, (2 + g) * LANES, blk_tag),
                                         (kw_ref, vwt_ref, (4 + g) * LANES, 0.0)):
            pair = zkv[:, base:base + LANES]
            k_ref[0, g] = jnp.where(lo, pair, tag).astype(BF16)
            vt_ref[0, g, 0:HEAD_DIM, :] = jnp.transpose(pair)[HEAD_DIM:].astype(BF16)
            vt_ref[0, g, HEAD_DIM:, :] = ones_rows

    ng_ref[...] = _sigmoid(_dot(h, wng_ref[...]))
    mg_ref[...] = _sigmoid(_dot(h, wmg_ref[...])).astype(BF16)


def _prep_inproj_weights(w_in, d_model):
    d_conv = d_model // 2
    n_conv = 2 * d_conv
    n_q = N_HEADS * HEAD_DIM
    n_kv = 6 * N_KV_HEADS * HEAD_DIM
    n_ng = 3 * N_HEADS
    o = 0
    wu = w_in[:, o:o + n_conv]; o += n_conv
    wq = w_in[:, o:o + n_q]; o += n_q
    wkv = w_in[:, o:o + n_kv]; o += n_kv
    wng = w_in[:, o:o + n_ng]; o += n_ng
    wmg = w_in[:, o:]
    wq_p = wq * (LOG2E / math.sqrt(HEAD_DIM))
    kvcols = [wkv[:, 0:LANES], wkv[:, LANES:2 * LANES]]
    for k_kind in (2, 4):
        for g in range(N_KV_HEADS):
            for kind in (k_kind, k_kind + 1):
                c0 = kind * N_KV_HEADS * HEAD_DIM + g * HEAD_DIM
                kvcols.append(wkv[:, c0:c0 + HEAD_DIM])
    wkv_p = jnp.concatenate(kvcols, axis=1)
    per_g = HPG * 3
    ngcols = []
    for g in range(N_KV_HEADS):
        ngcols += [wng[:, g * per_g:(g + 1) * per_g],
                   jnp.zeros((d_model, LANES - per_g), w_in.dtype)]
    wng_p = jnp.concatenate(ngcols, axis=1)
    return tuple(w.astype(BF16) for w in (wu, wq_p, wkv_p, wng_p, wmg))


def _inproj(x2, norm_g, weights, batch, seq, tm=512):
    t, d = x2.shape
    wu, wq, wkv, wng, wmg = weights
    d_conv = d // 2
    tiles_per_seq = seq // tm
    n_tiles = t // tm
    full = lambda w: pl.BlockSpec(w.shape, lambda r: (0, 0))
    row = lambda n: pl.BlockSpec((tm, n), lambda r: (r, 0))
    headed = lambda nh: pl.BlockSpec((1, nh, tm, LANES),
                                     lambda r: (r // tiles_per_seq, 0, r % tiles_per_seq, 0))
    headed_t = lambda nh: pl.BlockSpec((1, nh, LANES, tm),
                                       lambda r: (r // tiles_per_seq, 0, 0, r % tiles_per_seq))
    k_shape = jax.ShapeDtypeStruct((batch, N_KV_HEADS, seq, LANES), BF16)
    vt_shape = jax.ShapeDtypeStruct((batch, N_KV_HEADS, LANES, seq), BF16)
    out_shape = (
        jax.ShapeDtypeStruct((t, d_conv), BF16),
        jax.ShapeDtypeStruct((batch, N_HEADS, LANES, seq), BF16),
        jax.ShapeDtypeStruct((t, LANES), BF16),
        jax.ShapeDtypeStruct((t, LANES), BF16),
        k_shape, vt_shape, k_shape, vt_shape,
        jax.ShapeDtypeStruct((t, N_KV_HEADS * LANES), F32),
        jax.ShapeDtypeStruct((t, 2 * d), BF16),
    )
    out_specs = (row(d_conv), headed_t(N_HEADS), row(LANES), row(LANES),
                 headed(N_KV_HEADS), headed_t(N_KV_HEADS), headed(N_KV_HEADS),
                 headed_t(N_KV_HEADS), row(N_KV_HEADS * LANES), row(2 * d))
    kern = functools.partial(_inproj_kernel, tm=tm, tiles_per_seq=tiles_per_seq, d_conv=d_conv)
    return pl.pallas_call(
        kern, out_shape=out_shape, grid=(n_tiles,),
        in_specs=[row(d), pl.BlockSpec((1, d), lambda r: (0, 0)),
                  full(wu), full(wq), full(wkv), full(wng), full(wmg)],
        out_specs=out_specs, compiler_params=_cparams(1), name="inproj",
    )(x2, norm_g.reshape(1, d), wu, wq, wkv, wng, wmg)


def _compress_kernel(rk_ref, rv_ref, pek_ref, pev_ref, w1k_ref, w1v_ref, w2k_ref, w2v_ref,
                     kcmp_ref, vcmp_ref, *, ncp):
    def one(r_ref, pe_ref, w1_ref, w2_ref):
        r = r_ref[0]
        top = _dot(r, w1_ref[0])
        bot = _dot(r, w1_ref[1])
        pe_h = _dot(pe_ref[0], w1_ref[0]) + _dot(pe_ref[1], w1_ref[1])
        nxt = pltpu.roll(bot, ncp - 1, 0)
        rowi = lax.broadcasted_iota(jnp.int32, top.shape, 0)
        hid = top + jnp.where(rowi == ncp - 1, 0.0, nxt) + pe_h[0:1, :]
        return _dot(_gelu(hid).astype(BF16), w2_ref[...])

    kcmp_ref[0] = one(rk_ref, pek_ref, w1k_ref, w2k_ref).astype(BF16)
    v = one(rv_ref, pev_ref, w1v_ref, w2v_ref)
    rowt = lax.broadcasted_iota(jnp.int32, (LANES, ncp), 0)
    ones_rows = (rowt >= HEAD_DIM) & (rowt < HEAD_DIM + 8)
    for g in range(N_KV_HEADS):
        vt = jnp.transpose(v[:, g * LANES:(g + 1) * LANES])
        vcmp_ref[0, g] = jnp.where(ones_rows, 1.0, vt).astype(BF16)


def _prep_compress_weights(pe, w1, w2):
    half = L_CMP // 2
    w1r = w1.reshape(L_CMP, HEAD_DIM, CMP_HIDDEN)
    halves = []
    for part in range(2):
        wp = w1r[part * half:(part + 1) * half]
        big = jnp.zeros((half, N_KV_HEADS, HEAD_DIM, N_KV_HEADS, CMP_HIDDEN), w1.dtype)
        for g in range(N_KV_HEADS):
            big = big.at[:, g, :, g, :].set(wp)
        halves.append(big.reshape(half * N_KV_HEADS * HEAD_DIM, N_KV_HEADS * CMP_HIDDEN))
    w1big = jnp.stack(halves).astype(BF16)
    w2big = jnp.zeros((N_KV_HEADS, CMP_HIDDEN, N_KV_HEADS, LANES), w2.dtype)
    for g in range(N_KV_HEADS):
        w2big = w2big.at[g, :, g, :HEAD_DIM].set(w2)
    w2big = w2big.reshape(N_KV_HEADS * CMP_HIDDEN, N_KV_HEADS * LANES).astype(BF16)
    per = pe.reshape(2, half, 1, HEAD_DIM)
    pebig = jnp.broadcast_to(per, (2, half, N_KV_HEADS, HEAD_DIM)).reshape(2, 1, -1)
    pebig = jnp.broadcast_to(pebig, (2, 8, pebig.shape[-1])).astype(BF16)
    return pebig, w1big, w2big


def _compress(kc, vc, wk, wv, batch, seq):
    ncp = seq // D_STRIDE
    width = D_STRIDE * LANES
    rk = kc.reshape(batch, ncp, width)
    rv = vc.reshape(batch, ncp, width)
    pek, w1k, w2k = wk
    pev, w1v, w2v = wv
    c3 = lambda a: pl.BlockSpec(a.shape, lambda b: (0, 0, 0))
    c2 = lambda a: pl.BlockSpec(a.shape, lambda b: (0, 0))
    rspec = pl.BlockSpec((1, ncp, width), lambda b: (b, 0, 0))
    kspec = pl.BlockSpec((1, ncp, N_KV_HEADS * LANES), lambda b: (b, 0, 0))
    kshape = jax.ShapeDtypeStruct((batch, ncp, N_KV_HEADS * LANES), BF16)
    vspec = pl.BlockSpec((1, N_KV_HEADS, LANES, ncp), lambda b: (b, 0, 0, 0))
    vshape = jax.ShapeDtypeStruct((batch, N_KV_HEADS, LANES, ncp), BF16)
    return pl.pallas_call(
        functools.partial(_compress_kernel, ncp=ncp), out_shape=(kshape, vshape), grid=(batch,),
        in_specs=[rspec, rspec, c3(pek), c3(pev), c3(w1k), c3(w1v), c2(w2k), c2(w2v)],
        out_specs=(kspec, vspec), compiler_params=_cparams(1), name="compress",
    )(rk, rv, pek, pev, w1k, w1v, w2k, w2v)


def _convmix_kernel(a_ref, halo_ref, ga_ref, cw_ref, cb_ref, lg_ref, lb_ref, wo_ref,
                    ya_ref, ext_ref, *, tm):
    i = pl.program_id(1)
    halo = halo_ref[0].astype(F32)
    ext_ref[0:CONV_HALO, :] = jnp.where(i == 0, 0.0, halo)
    ext_ref[CONV_HALO:, :] = a_ref[0].astype(F32)
    off = CONV_HALO - (CONV_WIDTH - 1)
    acc = jnp.zeros((tm, a_ref.shape[-1]), F32) + cb_ref[...]
    for j in range(CONV_WIDTH):
        acc = acc + cw_ref[j:j + 1, :] * ext_ref[off + j:off + j + tm, :]
    mu = jnp.mean(acc, axis=-1, keepdims=True)
    cen = acc - mu
    var = jnp.mean(cen * cen, axis=-1, keepdims=True)
    y = cen * lax.rsqrt(var + EPS) * lg_ref[...] + lb_ref[...]
    y = y * _sigmoid(y)
    yc = _dot(y.astype(BF16), wo_ref[...])
    ya_ref[...] = (ga_ref[...].astype(F32) * yc).astype(BF16)


def _convmix(a, mg, conv_w, conv_b, ln_g, ln_b, w_conv_out, batch, seq, tm=512):
    t, d_conv = a.shape
    d = w_conv_out.shape[1]
    a3 = a.reshape(batch, seq, d_conv)
    nt = seq // tm
    hb = tm // CONV_HALO
    cw = jnp.concatenate([conv_w, jnp.zeros((1, d_conv), conv_w.dtype)], axis=0)
    vec = lambda v: v.reshape(1, -1)
    c2 = lambda a_: pl.BlockSpec(a_.shape, lambda b, i: (0, 0))
    return pl.pallas_call(
        functools.partial(_convmix_kernel, tm=tm),
        out_shape=jax.ShapeDtypeStruct((t, d), BF16), grid=(batch, nt),
        in_specs=[pl.BlockSpec((1, tm, d_conv), lambda b, i: (b, i, 0)),
                  pl.BlockSpec((1, CONV_HALO, d_conv),
                               lambda b, i: (b, jnp.maximum(i * hb - 1, 0), 0)),
                  pl.BlockSpec((tm, d), lambda b, i: (b * nt + i, 0)),
                  c2(cw), c2(vec(conv_b)), c2(vec(ln_g)), c2(vec(ln_b)), c2(w_conv_out)],
        out_specs=pl.BlockSpec((tm, d), lambda b, i: (b * nt + i, 0)),
        scratch_shapes=[pltpu.VMEM((tm + CONV_HALO, d_conv), F32)],
        compiler_params=_cparams(2), name="convmix",
    )(a3, a3, mg, cw, vec(conv_b), vec(ln_g), vec(ln_b), w_conv_out)


def _nsa_kernel(b31_ref, qt_ref, kcmp_ref, vcmpt_ref, biasc_ref, ks_ref, vst_ref, kw_ref, vwt_ref,
                ng_ref, aggt_ref, bd_ref, o_ref,
                imp_ref, qa_ref, oc_ref, ms_ref, accs_ref, mw_ref, accw_ref, *, ns):
    g = pl.program_id(1)
    i = pl.program_id(2)

    p_sum = None
    for hh in range(HPG):
        s_c = _dot(kcmp_ref[0], qt_ref[0, hh]) + biasc_ref[hh]
        m_c = jnp.maximum(jnp.max(s_c, axis=0, keepdims=True), M_INIT)
        p_c = jnp.exp2(s_c - m_c)
        l_c = jnp.sum(p_c, axis=0, keepdims=True)
        p_c = p_c * jnp.where(l_c > 0.0, 1.0 / l_c, 0.0)
        oc_ref[hh] = _dot(vcmpt_ref[0, 0], p_c.astype(BF16))[0:HEAD_DIM]
        p_sum = p_c if p_sum is None else p_sum + p_c

    p_hi = p_sum.astype(BF16)
    p_lo = (p_sum - p_hi.astype(F32)).astype(BF16)
    imp = _dot(aggt_ref[...], p_hi) + _dot(aggt_ref[...], p_lo)
    blk = lax.broadcasted_iota(jnp.int32, (ns, TQ), 0)
    cur = (i * TQ + lax.broadcasted_iota(jnp.int32, (ns, TQ), 1)) // L_SLC
    forced = (blk == 0) | (blk == cur) | (blk == cur - 1)
    imp = jnp.where(forced, FORCE, jnp.where(blk <= cur, imp, -FORCE))
    imp_ref[...] = imp
    n_grp = ns // 8
    grp = [imp[8 * r:8 * r + 8, :] for r in range(n_grp)]
    blk8 = lax.broadcasted_iota(jnp.int32, (8, TQ), 0)
    rank = [jnp.zeros((8, TQ), F32) for _ in range(n_grp)]
    for k in range(ns):
        rowk = imp_ref[k:k + 1, :]
        kg = k // 8
        for r in range(n_grp):
            if r < kg:
                beats = jnp.where(rowk > grp[r], 1.0, 0.0)
            elif r > kg:
                beats = jnp.where(rowk >= grp[r], 1.0, 0.0)
            else:
                beats = jnp.where(blk8 > k - 8 * kg, jnp.where(rowk >= grp[r], 1.0, 0.0),
                                  jnp.where(rowk > grp[r], 1.0, 0.0))
            rank[r] = rank[r] + beats
    rank = jnp.concatenate(rank, axis=0)
    unsel =jnp.where(rank < float(N_SEL), 0.0, -1.0)
    flags = unsel.astype(BF16)
    if ns < HEAD_DIM:
        flags = jnp.concatenate([flags, jnp.zeros((HEAD_DIM - ns, TQ), BF16)], axis=0)
    for hh in range(HPG):
        qa_ref[hh, 0:HEAD_DIM, :] = qt_ref[0, hh, 0:HEAD_DIM, :]
        qa_ref[hh, HEAD_DIM:, :] = flags
        ms_ref[hh] = jnp.full((1, TQ), M_INIT, F32)
        mw_ref[hh] = jnp.full((1, TQ), M_INIT, F32)
        accs_ref[hh] = jnp.zeros((LANES, TQ), F32)
        accw_ref[hh] = jnp.zeros((LANES, TQ), F32)

    def k_tile(ref, j):
        return ref[0, 0, pl.ds(pl.multiple_of(j * TQ, TQ), TQ), :]

    def vt_tile(ref, j):
        return ref[0, 0, :, pl.ds(pl.multiple_of(j * TQ, TQ), TQ)]

    def run(tasks):
        def scores(task):
            hh, kt, _, qmat, _, _, row0 = task
            s = _dot(kt, qmat)
            return s if row0 is None else s + bd_ref[hh, row0:row0 + TQ, :]

        def update(task, s):
            hh, _, vt, _, m_ref, acc_ref, _ = task
            m_old = m_ref[hh]
            m_new = jnp.maximum(m_old, jnp.max(s, axis=0, keepdims=True))
            p = jnp.exp2(s - m_new).astype(BF16)
            acc_ref[hh] = jnp.exp2(m_old - m_new) * acc_ref[hh] + _dot(vt, p)
            m_ref[hh] = m_new

        pending = [scores(t) for t in tasks[:PIPE_DEPTH]]
        for t, task in enumerate(tasks):
            update(task, pending.pop(0))
            if t + PIPE_DEPTH < len(tasks):
                pending.append(scores(tasks[t + PIPE_DEPTH]))

    def slc_tasks(j, row0=None):
        kt, vt = k_tile(ks_ref, j), vt_tile(vst_ref, j)
        return [(hh, kt, vt, qa_ref[hh], ms_ref, accs_ref, row0) for hh in range(HPG)]

    def win_tasks(j, row0):
        kt, vt = k_tile(kw_ref, j), vt_tile(vwt_ref, j)
        return [(hh, kt, vt, qt_ref[0, hh], mw_ref, accw_ref, row0) for hh in range(HPG)]

    n_far = jnp.maximum(i - 1, 0)

    def far_pair(jj, carry):
        run(slc_tasks(2 * jj) + slc_tasks(2 * jj + 1))
        return carry

    lax.fori_loop(0, n_far // 2, far_pair, 0)

    @pl.when(n_far % 2 == 1)
    def _():
        run(slc_tasks(n_far - 1))

    for hh in range(HPG):
        ms_ref[hh] = ms_ref[hh] + b31_ref[g * HPG + hh]

    @pl.when(i >= 2)
    def _():
        run(win_tasks(i - 2, 0))

    @pl.when(i >= 1)
    def _():
        run(slc_tasks(i - 1, TQ) + win_tasks(i - 1, TQ))

    run(slc_tasks(i, 2 * TQ) + win_tasks(i, 2 * TQ))

    gates_t = jnp.transpose(ng_ref[...])
    outs = []
    for hh in range(HPG):
        a_s, a_w = accs_ref[hh], accw_ref[hh]
        o_s = a_s[0:HEAD_DIM] * (1.0 / a_s[HEAD_DIM:HEAD_DIM + 1])
        o_w = a_w[0:HEAD_DIM] * (1.0 / a_w[HEAD_DIM:HEAD_DIM + 1])
        outs.append(gates_t[3 * hh:3 * hh + 1] * oc_ref[hh]
                    + gates_t[3 * hh + 1:3 * hh + 2] * o_s
                    + gates_t[3 * hh + 2:3 * hh + 3] * o_w)
    o_ref[...] = jnp.transpose(jnp.concatenate(outs, axis=0)).astype(BF16)


def _bias_of(thr_ref, rb_ref, dist, h):
    b = jnp.full(dist.shape, rb_ref[h], F32)
    for k in range(1, N_BUCKETS):
        b = jnp.where(dist >= thr_ref[k], rb_ref[k * N_HEADS + h], b)
    return b


def _cmp_bias_kernel(thr_ref, rb_ref, biasc_ref, *, nc, ncp):
    i = pl.program_id(0)
    c = lax.broadcasted_iota(jnp.int32, (ncp, TQ), 0)
    r = lax.broadcasted_iota(jnp.int32, (ncp, TQ), 1)
    dist_c = i * TQ + r - (c * D_STRIDE + L_CMP - 1)
    ok_c = (dist_c >= 0) & (c < nc)
    for h in range(N_HEADS):
        biasc_ref[h] = jnp.where(ok_c, _bias_of(thr_ref, rb_ref, dist_c, h), NEG)


def _diag_bias_kernel(thr_ref, rb_ref, bd_ref):
    d0 = (lax.broadcasted_iota(jnp.int32, (TQ, TQ), 1)
          - lax.broadcasted_iota(jnp.int32, (TQ, TQ), 0))
    for h in range(N_HEADS):
        bd_ref[h, 0:TQ, :] = jnp.where(d0 < 0, _bias_of(thr_ref, rb_ref, d0 + 2 * TQ, h), NEG)
        bd_ref[h, TQ:2 * TQ, :] = _bias_of(thr_ref, rb_ref, d0 + TQ, h)
        bd_ref[h, 2 * TQ:, :] = jnp.where(d0 >= 0, _bias_of(thr_ref, rb_ref, d0, h), NEG)


def _attention_tables(rel_bias, seq):
    ncp = seq // D_STRIDE
    nc = (seq - L_CMP) // D_STRIDE + 1
    ns = seq // L_SLC
    nt = seq // TQ
    n_probe = 2 * MAX_DISTANCE
    buckets = _t5_bucket(jnp.arange(n_probe))
    thr = jnp.sum(buckets[None, :] < jnp.arange(N_BUCKETS)[:, None], axis=1).astype(jnp.int32)
    assert WINDOW == 2 * TQ
    rb = (rel_bias.astype(F32) * LOG2E).reshape(-1)
    bias_c = pl.pallas_call(
        functools.partial(_cmp_bias_kernel, nc=nc, ncp=ncp),
        out_shape=jax.ShapeDtypeStruct((N_HEADS, ncp, seq), F32),
        grid_spec=pltpu.PrefetchScalarGridSpec(
            num_scalar_prefetch=2, grid=(nt,), in_specs=[],
            out_specs=pl.BlockSpec((N_HEADS, ncp, TQ), lambda i, *_: (0, 0, i))),
        compiler_params=_cparams(1), name="cmp_bias")(thr, rb)
    bd = pl.pallas_call(
        _diag_bias_kernel, out_shape=jax.ShapeDtypeStruct((N_HEADS, 3 * TQ, TQ), F32),
        grid_spec=pltpu.PrefetchScalarGridSpec(
            num_scalar_prefetch=2, grid=(1,), in_specs=[],
            out_specs=pl.BlockSpec((N_HEADS, 3 * TQ, TQ), lambda i, *_: (0, 0, 0))),
        compiler_params=_cparams(1), name="diag_bias")(thr, rb)
    c_start = jnp.arange(ncp) * D_STRIDE
    c_end = c_start + L_CMP - 1
    s_start = jnp.arange(ns) * L_SLC
    aggt = ((c_end[None, :] >= s_start[:, None]) & (c_start[None, :] <= s_start[:, None] + L_SLC - 1)
            & (jnp.arange(ncp)[None, :] < nc)).astype(BF16)
    return bias_c, bd, aggt, rel_bias[N_BUCKETS - 1].astype(F32) * LOG2E


def _nsa(qt, kcmp, vcmpt, ks, vst, kw, vwt, ng, rel_bias, batch, seq):
    first_far = TQ + 1
    assert MAX_EXACT + int(math.log(first_far / MAX_EXACT) / math.log(MAX_DISTANCE / MAX_EXACT)
                           * (N_BUCKETS - MAX_EXACT)) >= N_BUCKETS - 1, "far tiles need one bucket"
    ncp = seq // D_STRIDE
    ns = seq // L_SLC
    nt = seq // TQ
    bias_c, bd, aggt, b31 = _attention_tables(rel_bias, seq)
    kspec = pl.BlockSpec((1, 1, seq, LANES), lambda b, g, i, *_: (b, g, 0, 0))
    vtspec = pl.BlockSpec((1, 1, LANES, seq), lambda b, g, i, *_: (b, g, 0, 0))
    state = lambda rows: pltpu.VMEM((HPG, rows, TQ), F32)
    grid_spec = pltpu.PrefetchScalarGridSpec(
        num_scalar_prefetch=1, grid=(batch, N_KV_HEADS, nt),
        in_specs=[pl.BlockSpec((1, HPG, LANES, TQ), lambda b, g, i, *_: (b, g, 0, i)),
                  pl.BlockSpec((1, ncp, LANES), lambda b, g, i, *_: (b, 0, g)),
                  pl.BlockSpec((1, 1, LANES, ncp), lambda b, g, i, *_: (b, g, 0, 0)),
                  pl.BlockSpec((HPG, ncp, TQ), lambda b, g, i, *_: (g, 0, i)),
                  kspec, vtspec, kspec, vtspec,
                  pl.BlockSpec((TQ, LANES), lambda b, g, i, *_: (b * nt + i, g)),
                  pl.BlockSpec((ns, ncp), lambda b, g, i, *_: (0, 0)),
                  pl.BlockSpec((HPG, 3 * TQ, TQ), lambda b, g, i, *_: (g, 0, 0))],
        out_specs=pl.BlockSpec((TQ, HPG * HEAD_DIM), lambda b, g, i, *_: (b * nt + i, g)),
        scratch_shapes=[pltpu.VMEM((ns, TQ), F32),
                        pltpu.VMEM((HPG, LANES, TQ), BF16),
                        state(HEAD_DIM),
                        state(1), state(LANES),
                        state(1), state(LANES)])
    return pl.pallas_call(
        functools.partial(_nsa_kernel, ns=ns),
        out_shape=jax.ShapeDtypeStruct((batch * seq, N_HEADS * HEAD_DIM), BF16),
        grid_spec=grid_spec, compiler_params=_cparams(3), name="nsa",
    )(b31, qt, kcmp, vcmpt, bias_c, ks, vst, kw, vwt, ng, aggt, bd)


def _merge_kernel(x_ref, ya_ref, o_ref, gb_ref, wao_ref, wout_ref, x1_ref):
    y_attn = _dot(o_ref[...], wao_ref[...])
    y = ya_ref[...].astype(F32) + gb_ref[...].astype(F32) * y_attn
    x1_ref[...] = x_ref[...] + _dot(y.astype(BF16), wout_ref[...])


def _merge(x2, ya, o, mg, w_attn_out, w_out, tm=512):
    t, d = x2.shape
    c2 = lambda a: pl.BlockSpec(a.shape, lambda r: (0, 0))
    return pl.pallas_call(
        _merge_kernel, out_shape=jax.ShapeDtypeStruct((t, d), F32), grid=(t // tm,),
        in_specs=[pl.BlockSpec((tm, d), lambda r: (r, 0)),
                  pl.BlockSpec((tm, d), lambda r: (r, 0)),
                  pl.BlockSpec((tm, o.shape[1]), lambda r: (r, 0)),
                  pl.BlockSpec((tm, d), lambda r: (r, 1)),
                  c2(w_attn_out), c2(w_out)],
        out_specs=pl.BlockSpec((tm, d), lambda r: (r, 0)),
        compiler_params=_cparams(1), name="merge",
    )(x2, ya, o, mg, w_attn_out, w_out)


def _ffn_kernel(x1_ref, p_ref, nf_ref, wg_ref, wv_ref, cwg_ref, cwv_ref, cbg_ref, cbv_ref, wd_ref,
                np_ref, wpg_ref, wpe_ref, nfin_ref, out_ref,
                hf_ref, acc_ref, carry_g_ref, carry_v_ref, *, tm, tiles_per_seq, n_chunks):
    r = pl.program_id(0)
    c = pl.program_id(1)
    seq_start = (r % tiles_per_seq) == 0

    @pl.when(c == 0)
    def _():
        hf_ref[...] = _rms(x1_ref[...], nf_ref[...]).astype(BF16)
        acc_ref[...] = jnp.zeros_like(acc_ref)

    @pl.when(seq_start)
    def _():
        carry_g_ref[c] = jnp.zeros(carry_g_ref.shape[1:], F32)
        carry_v_ref[c] = jnp.zeros(carry_v_ref.shape[1:], F32)

    hf = hf_ref[...]

    def conv3(up, cw_ref, cb_ref, carry_ref):
        rowi = lax.broadcasted_iota(jnp.int32, up.shape, 0)
        prev = carry_ref[c]
        s1 = jnp.where(rowi == 0, prev[7:8, :], pltpu.roll(up, 1, 0))
        s2 = jnp.where(rowi == 0, prev[6:7, :],
                       jnp.where(rowi == 1, prev[7:8, :], pltpu.roll(up, 2, 0)))
        carry_ref[c] = up[tm - 8:, :]
        return cw_ref[0:1, :] * s2 + cw_ref[1:2, :] * s1 + cw_ref[2:3, :] * up + cb_ref[...]

    gate = conv3(_dot(hf, wg_ref[...]), cwg_ref, cbg_ref, carry_g_ref)
    val = conv3(_dot(hf, wv_ref[...]), cwv_ref, cbv_ref, carry_v_ref)
    acc_ref[...] += _dot((_gelu(gate) * val).astype(BF16), wd_ref[...])

    @pl.when(c == n_chunks - 1)
    def _():
        x2 = x1_ref[...] + acc_ref[...]
        pg = _sigmoid(_dot(_rms(x2, np_ref[...]).astype(BF16), wpg_ref[...]))
        pe = _dot(p_ref[...].astype(BF16), wpe_ref[...])
        out_ref[...] = _rms(x2 + pg * pe, nfin_ref[...])


def _ffn(x1, p2, norm_ffn, w_up, ffn_dw_w, ffn_dw_b, w_down, norm_ple, w_ple_gate, w_ple,
         norm_final, seq, tm=512, n_chunks=2):
    t, d = x1.shape
    d_ff = w_down.shape[0]
    fc = d_ff // n_chunks
    assert fc * n_chunks == d_ff and fc % LANES == 0
    tiles_per_seq = seq // tm
    cw = jnp.concatenate([ffn_dw_w, jnp.zeros((8 - FFN_CONV_WIDTH, 2 * d_ff), ffn_dw_w.dtype)], 0)
    cb = ffn_dw_b.reshape(1, -1)
    vec = lambda v: v.reshape(1, -1)
    c2 = lambda a: pl.BlockSpec(a.shape, lambda r, c: (0, 0))
    ncol = d_ff // fc
    kern = functools.partial(_ffn_kernel, tm=tm, tiles_per_seq=tiles_per_seq, n_chunks=n_chunks)
    return pl.pallas_call(
        kern, out_shape=jax.ShapeDtypeStruct((t, d), F32), grid=(t // tm, n_chunks),
        in_specs=[pl.BlockSpec((tm, d), lambda r, c: (r, 0)),
                  pl.BlockSpec((tm, p2.shape[1]), lambda r, c: (r, 0)),
                  c2(vec(norm_ffn)),
                  pl.BlockSpec((d, fc), lambda r, c: (0, c)),
                  pl.BlockSpec((d, fc), lambda r, c: (0, ncol + c)),
                  pl.BlockSpec((8, fc), lambda r, c: (0, c)),
                  pl.BlockSpec((8, fc), lambda r, c: (0, ncol + c)),
                  pl.BlockSpec((1, fc), lambda r, c: (0, c)),
                  pl.BlockSpec((1, fc), lambda r, c: (0, ncol + c)),
                  pl.BlockSpec((fc, d), lambda r, c: (c, 0)),
                  c2(vec(norm_ple)), c2(w_ple_gate), c2(w_ple), c2(vec(norm_final))],
        out_specs=pl.BlockSpec((tm, d), lambda r, c: (r, 0)),
        scratch_shapes=[pltpu.VMEM((tm, d), BF16), pltpu.VMEM((tm, d), F32),
                        pltpu.VMEM((n_chunks, 8, fc), F32), pltpu.VMEM((n_chunks, 8, fc), F32)],
        compiler_params=_cparams(2), name="ffn",
    )(x1, p2, vec(norm_ffn), w_up, w_up, cw, cw, cb, cb, w_down,
      vec(norm_ple), w_ple_gate, w_ple, vec(norm_final))


def kernel(x, p, rel_bias, norm_mix, w_in, conv_dw_w, conv_dw_b, conv_ln_g, conv_ln_b, w_conv_out,
           cmp_pe_k, cmp_pe_v, w_ck1, w_ck2, w_cv1, w_cv2, w_attn_out, w_out, norm_ffn, w_up,
           ffn_dw_w, ffn_dw_b, w_down, norm_ple, w_ple_gate, w_ple, norm_final):
    batch, seq, d = x.shape
    depth = w_in.shape[0]
    x2 = x.reshape(batch * seq, d)
    for i in range(depth):
        a, qt, kc, vc, ks, vst, kw, vwt, ng, mg = _inproj(
            x2, norm_mix[i], _prep_inproj_weights(w_in[i], d), batch, seq)
        kcmp, vcmpt = _compress(kc, vc,
                               _prep_compress_weights(cmp_pe_k[i], w_ck1[i], w_ck2[i]),
                               _prep_compress_weights(cmp_pe_v[i], w_cv1[i], w_cv2[i]),
                               batch, seq)
        ya = _convmix(a, mg, conv_dw_w[i], conv_dw_b[i], conv_ln_g[i], conv_ln_b[i],
                      w_conv_out[i].astype(BF16), batch, seq)
        o = _nsa(qt, kcmp, vcmpt, ks, vst, kw, vwt, ng, rel_bias, batch, seq)
        x1 = _merge(x2, ya, o, mg, w_attn_out[i].astype(BF16), w_out[i].astype(BF16))
        assert i == depth - 1, "the final RMSNorm is fused into the (single) layer's MLP kernel"
        x2 = _ffn(x1, p[i].reshape(batch * seq, -1), norm_ffn[i], w_up[i].astype(BF16),
                  ffn_dw_w[i], ffn_dw_b[i], w_down[i].astype(BF16), norm_ple[i],
                  w_ple_gate[i].astype(BF16), w_ple[i].astype(BF16), norm_final, seq)
    return x2.reshape(batch, seq, d)
```

```python
import functools
import math

import jax
import jax.numpy as jnp
import numpy as np
from jax import lax
from jax.experimental import pallas as pl
from jax.experimental.pallas import tpu as pltpu

N_HEADS = 8
HEAD_DIM = 64
N_KV_HEADS = 2
HPG = N_HEADS // N_KV_HEADS
L_CMP = 32
D_STRIDE = 16
CMP_HIDDEN = 256
L_SLC = 64
N_SEL = 16
WINDOW = 512
N_BUCKETS = 32
MAX_EXACT = N_BUCKETS // 2
MAX_DISTANCE = 128
CONV_WIDTH = 31
FFN_CONV_WIDTH = 3
EPS = 1e-6
FORCE = 1e4

LANES = 128
MXU_DEPTH = 256
FFN_CHUNK = 6 * MXU_DEPTH
NEG = -1e30
M_INIT = -1e29
LOG2E = 1.4426950408889634
PIPE_DEPTH = 6
FAR_GROUP = 4
TQ = 256
SUBLANES = 8
CONV_HALO = 32
CONV_ROWS = 64
VMEM_LIMIT = 56 * 1024 * 1024

F32 = jnp.float32
BF16 = jnp.bfloat16


def _cparams(n_axes):
    return pltpu.CompilerParams(dimension_semantics=("arbitrary",) * n_axes,
                                vmem_limit_bytes=VMEM_LIMIT)


def _dot(a, b):
    return jnp.dot(a, b, preferred_element_type=F32)


def _dot_nt(a, b):
    return lax.dot_general(a, b, (((1,), (1,)), ((), ())), preferred_element_type=F32)


def _rms(xf, g):
    return xf * lax.rsqrt(jnp.mean(xf * xf, axis=-1, keepdims=True) + EPS) * g


def _sigmoid(x):
    return 1.0 / (1.0 + jnp.exp(-x))


def _gelu(x):
    return 0.5 * x * (1.0 + jnp.tanh(0.7978845608028654 * (x + 0.044715 * x * x * x)))


def _t5_bucket(dist):
    n = jnp.maximum(dist, 0)
    nf = jnp.maximum(n, MAX_EXACT).astype(F32)
    large = MAX_EXACT + (jnp.log(nf / MAX_EXACT) / math.log(MAX_DISTANCE / MAX_EXACT)
                         * (N_BUCKETS - MAX_EXACT)).astype(jnp.int32)
    large = jnp.minimum(large, N_BUCKETS - 1)
    return jnp.where(n < MAX_EXACT, n, large)


def _inproj_kernel(x_ref, g_ref, wu_ref, wq_ref, wkv_ref, wng_ref, wmg_ref,
                   a_ref, qt_ref, kc_ref, vc_ref, ks_ref, vst_ref, kw_ref, vwt_ref, ng_ref, mg_ref,
                   *, tm, tiles_per_seq, d_conv):
    r = pl.program_id(0)
    h = _rms(x_ref[...], g_ref[...]).astype(BF16)

    u = _dot(h, wu_ref[...])
    a_ref[...] = (u[:, :d_conv] * _sigmoid(u[:, d_conv:])).astype(BF16)

    zqt = jnp.transpose(_dot(h, wq_ref[...]))
    zero_rows = jnp.zeros((HEAD_DIM, tm), BF16)
    for hh in range(N_HEADS):
        qt_ref[0, hh, 0:HEAD_DIM, :] = zqt[hh * HEAD_DIM:(hh + 1) * HEAD_DIM].astype(BF16)
        qt_ref[0, hh, HEAD_DIM:, :] = zero_rows

    zkv = _dot(h, wkv_ref[...])
    kc_ref[...] = zkv[:, 0:LANES].astype(BF16)
    vc_ref[...] = zkv[:, LANES:2 * LANES].astype(BF16)
    lane = lax.broadcasted_iota(jnp.int32, (tm, LANES), 1)
    row = lax.broadcasted_iota(jnp.int32, (tm, LANES), 0)
    spos = (r % tiles_per_seq) * tm + row
    lo = lane < HEAD_DIM
    blk_tag = jnp.where((lane - HEAD_DIM) == spos // L_SLC, -NEG, 0.0)
    ones_rows = jnp.where(lax.broadcasted_iota(jnp.int32, (HEAD_DIM, tm), 0) < 8, 1.0, 0.0)
    ones_rows = ones_rows.astype(BF16)
    for g in range(N_KV_HEADS):
        for k_ref, vt_ref, base, tag in ((ks_ref, vst_ref, (2 + g) * LANES, blk_tag),
                                         (kw_ref, vwt_ref, (4 + g) * LANES, 0.0)):
            pair = zkv[:, base:base + LANES]
            k_ref[0, g] = jnp.where(lo, pair, tag).astype(BF16)
            vt_ref[0, g, 0:HEAD_DIM, :] = jnp.transpose(pair)[HEAD_DIM:].astype(BF16)
            vt_ref[0, g, HEAD_DIM:, :] = ones_rows

    ng_ref[...] = _sigmoid(_dot(h, wng_ref[...]))
    mg_ref[...] = _sigmoid(_dot(h, wmg_ref[...])).astype(BF16)


def _prep_inproj_weights(w_in, d_model):
    d_conv = d_model // 2
    n_conv = 2 * d_conv
    n_q = N_HEADS * HEAD_DIM
    n_kv = 6 * N_KV_HEADS * HEAD_DIM
    n_ng = 3 * N_HEADS
    o = 0
    wu = w_in[:, o:o + n_conv]; o += n_conv
    wq = w_in[:, o:o + n_q]; o += n_q
    wkv = w_in[:, o:o + n_kv]; o += n_kv
    wng = w_in[:, o:o + n_ng]; o += n_ng
    wmg = w_in[:, o:]
    wq_p = wq * (LOG2E / math.sqrt(HEAD_DIM))
    kvcols = [wkv[:, 0:LANES], wkv[:, LANES:2 * LANES]]
    for k_kind in (2, 4):
        for g in range(N_KV_HEADS):
            for kind in (k_kind, k_kind + 1):
                c0 = kind * N_KV_HEADS * HEAD_DIM + g * HEAD_DIM
                kvcols.append(wkv[:, c0:c0 + HEAD_DIM])
    wkv_p = jnp.concatenate(kvcols, axis=1)
    per_g = HPG * 3
    ngcols = []
    for g in range(N_KV_HEADS):
        ngcols += [wng[:, g * per_g:(g + 1) * per_g],
                   jnp.zeros((d_model, LANES - per_g), w_in.dtype)]
    wng_p = jnp.concatenate(ngcols, axis=1)
    return tuple(w.astype(BF16) for w in (wu, wq_p, wkv_p, wng_p, wmg))


def _inproj(x2, norm_g, weights, batch, seq, tm=512):
    t, d = x2.shape
    wu, wq, wkv, wng, wmg = weights
    d_conv = d // 2
    tiles_per_seq = seq // tm
    n_tiles = t // tm
    full = lambda w: pl.BlockSpec(w.shape, lambda r: (0, 0))
    row = lambda n: pl.BlockSpec((tm, n), lambda r: (r, 0))
    headed = lambda nh: pl.BlockSpec((1, nh, tm, LANES),
                                     lambda r: (r // tiles_per_seq, 0, r % tiles_per_seq, 0))
    headed_t = lambda nh: pl.BlockSpec((1, nh, LANES, tm),
                                       lambda r: (r // tiles_per_seq, 0, 0, r % tiles_per_seq))
    k_shape = jax.ShapeDtypeStruct((batch, N_KV_HEADS, seq, LANES), BF16)
    vt_shape = jax.ShapeDtypeStruct((batch, N_KV_HEADS, LANES, seq), BF16)
    out_shape = (
        jax.ShapeDtypeStruct((t, d_conv), BF16),
        jax.ShapeDtypeStruct((batch, N_HEADS, LANES, seq), BF16),
        jax.ShapeDtypeStruct((t, LANES), BF16),
        jax.ShapeDtypeStruct((t, LANES), BF16),
        k_shape, vt_shape, k_shape, vt_shape,
        jax.ShapeDtypeStruct((t, N_KV_HEADS * LANES), F32),
        jax.ShapeDtypeStruct((t, 2 * d), BF16),
    )
    out_specs = (row(d_conv), headed_t(N_HEADS), row(LANES), row(LANES),
                 headed(N_KV_HEADS), headed_t(N_KV_HEADS), headed(N_KV_HEADS),
                 headed_t(N_KV_HEADS), row(N_KV_HEADS * LANES), row(2 * d))
    kern = functools.partial(_inproj_kernel, tm=tm, tiles_per_seq=tiles_per_seq, d_conv=d_conv)
    return pl.pallas_call(
        kern, out_shape=out_shape, grid=(n_tiles,),
        in_specs=[row(d), pl.BlockSpec((1, d), lambda r: (0, 0)),
                  full(wu), full(wq), full(wkv), full(wng), full(wmg)],
        out_specs=out_specs, compiler_params=_cparams(1), name="inproj",
    )(x2, norm_g.reshape(1, d), wu, wq, wkv, wng, wmg)


def _compress_kernel(rk_ref, rv_ref, pek_ref, pev_ref, w1k_ref, w1v_ref, w2k_ref, w2v_ref,
                     kcmp_ref, vcmp_ref, *, ncp):
    def one(r_ref, pe_ref, w1_ref, w2_ref):
        r = r_ref[0]
        top = _dot(r, w1_ref[0])
        bot = _dot(r, w1_ref[1])
        pe_h = _dot(pe_ref[0], w1_ref[0]) + _dot(pe_ref[1], w1_ref[1])
        nxt = pltpu.roll(bot, ncp - 1, 0)
        rowi = lax.broadcasted_iota(jnp.int32, top.shape, 0)
        hid = top + jnp.where(rowi == ncp - 1, 0.0, nxt) + pe_h[0:1, :]
        return _dot(_gelu(hid).astype(BF16), w2_ref[...])

    kcmp_ref[0] = one(rk_ref, pek_ref, w1k_ref, w2k_ref).astype(BF16)
    v = one(rv_ref, pev_ref, w1v_ref, w2v_ref)
    rowt = lax.broadcasted_iota(jnp.int32, (LANES, ncp), 0)
    ones_rows = (rowt >= HEAD_DIM) & (rowt < HEAD_DIM + 8)
    for g in range(N_KV_HEADS):
        vt = jnp.transpose(v[:, g * LANES:(g + 1) * LANES])
        vcmp_ref[0, g] = jnp.where(ones_rows, 1.0, vt).astype(BF16)


def _prep_compress_weights(pe, w1, w2):
    half = L_CMP // 2
    w1r = w1.reshape(L_CMP, HEAD_DIM, CMP_HIDDEN)
    halves = []
    for part in range(2):
        wp = w1r[part * half:(part + 1) * half]
        big = jnp.zeros((half, N_KV_HEADS, HEAD_DIM, N_KV_HEADS, CMP_HIDDEN), w1.dtype)
        for g in range(N_KV_HEADS):
            big = big.at[:, g, :, g, :].set(wp)
        halves.append(big.reshape(half * N_KV_HEADS * HEAD_DIM, N_KV_HEADS * CMP_HIDDEN))
    w1big = jnp.stack(halves).astype(BF16)
    w2big = jnp.zeros((N_KV_HEADS, CMP_HIDDEN, N_KV_HEADS, LANES), w2.dtype)
    for g in range(N_KV_HEADS):
        w2big = w2big.at[g, :, g, :HEAD_DIM].set(w2)
    w2big = w2big.reshape(N_KV_HEADS * CMP_HIDDEN, N_KV_HEADS * LANES).astype(BF16)
    per = pe.reshape(2, half, 1, HEAD_DIM)
    pebig = jnp.broadcast_to(per, (2, half, N_KV_HEADS, HEAD_DIM)).reshape(2, 1, -1)
    pebig = jnp.broadcast_to(pebig, (2, 8, pebig.shape[-1])).astype(BF16)
    return pebig, w1big, w2big


def _compress(kc, vc, wk, wv, batch, seq):
    ncp = seq // D_STRIDE
    width = D_STRIDE * LANES
    rk = kc.reshape(batch, ncp, width)
    rv = vc.reshape(batch, ncp, width)
    pek, w1k, w2k = wk
    pev, w1v, w2v = wv
    c3 = lambda a: pl.BlockSpec(a.shape, lambda b: (0, 0, 0))
    c2 = lambda a: pl.BlockSpec(a.shape, lambda b: (0, 0))
    rspec = pl.BlockSpec((1, ncp, width), lambda b: (b, 0, 0))
    kspec = pl.BlockSpec((1, ncp, N_KV_HEADS * LANES), lambda b: (b, 0, 0))
    kshape = jax.ShapeDtypeStruct((batch, ncp, N_KV_HEADS * LANES), BF16)
    vspec = pl.BlockSpec((1, N_KV_HEADS, LANES, ncp), lambda b: (b, 0, 0, 0))
    vshape = jax.ShapeDtypeStruct((batch, N_KV_HEADS, LANES, ncp), BF16)
    return pl.pallas_call(
        functools.partial(_compress_kernel, ncp=ncp), out_shape=(kshape, vshape), grid=(batch,),
        in_specs=[rspec, rspec, c3(pek), c3(pev), c3(w1k), c3(w1v), c2(w2k), c2(w2v)],
        out_specs=(kspec, vspec), compiler_params=_cparams(1), name="compress",
    )(rk, rv, pek, pev, w1k, w1v, w2k, w2v)


def _convmix_kernel(a_ref, halo_ref, ga_ref, cw_ref, cb_ref, lg_ref, lb_ref, wo_ref,
                    ya_ref, ext_ref, sh_ref, y_ref, *, tm):
    i = pl.program_id(1)
    halo = halo_ref[0].astype(F32)
    ext_ref[0:CONV_HALO, :] = jnp.where(i == 0, 0.0, halo)
    ext_ref[CONV_HALO:, :] = a_ref[0].astype(F32)
    span = sh_ref.shape[1]
    for ph in range(1, SUBLANES):
        sh_ref[ph - 1] = ext_ref[ph:ph + span, :]
    off = CONV_HALO - (CONV_WIDTH - 1)

    def rows_chunk(cidx, carry):
        r0 = pl.multiple_of(cidx * CONV_ROWS, CONV_ROWS)
        acc = jnp.zeros((CONV_ROWS, a_ref.shape[-1]), F32) + cb_ref[...]
        for j in range(CONV_WIDTH):
            ph = (off + j) % SUBLANES
            base = off + j - ph
            if ph == 0:
                xs = ext_ref[pl.ds(r0 + base, CONV_ROWS), :]
            else:
                xs = sh_ref[ph - 1, pl.ds(r0 + base, CONV_ROWS), :]
            acc = acc + jnp.tile(cw_ref[j], (CONV_ROWS // SUBLANES, 1)) * xs
        y_ref[pl.ds(r0, CONV_ROWS), :] = acc
        return carry

    lax.fori_loop(0, tm // CONV_ROWS, rows_chunk, 0)
    acc = y_ref[...]
    mu = jnp.mean(acc, axis=-1, keepdims=True)
    cen = acc - mu
    var = jnp.mean(cen * cen, axis=-1, keepdims=True)
    y = cen * lax.rsqrt(var + EPS) * lg_ref[...] + lb_ref[...]
    y = y * _sigmoid(y)
    yc = _dot(y.astype(BF16), wo_ref[...])
    ya_ref[...] = (ga_ref[...].astype(F32) * yc).astype(BF16)


def _convmix(a, mg, conv_w, conv_b, ln_g, ln_b, w_conv_out, batch, seq, tm=512):
    t, d_conv = a.shape
    d = w_conv_out.shape[1]
    a3 = a.reshape(batch, seq, d_conv)
    nt = seq // tm
    hb = tm // CONV_HALO
    cw = jnp.broadcast_to(conv_w[:, None, :], (CONV_WIDTH, SUBLANES, d_conv))
    vec = lambda v: v.reshape(1, -1)
    c2 = lambda a_: pl.BlockSpec(a_.shape, lambda b, i: (0,) * a_.ndim)
    return pl.pallas_call(
        functools.partial(_convmix_kernel, tm=tm),
        out_shape=jax.ShapeDtypeStruct((t, d), BF16), grid=(batch, nt),
        in_specs=[pl.BlockSpec((1, tm, d_conv), lambda b, i: (b, i, 0)),
                  pl.BlockSpec((1, CONV_HALO, d_conv),
                               lambda b, i: (b, jnp.maximum(i * hb - 1, 0), 0)),
                  pl.BlockSpec((tm, d), lambda b, i: (b * nt + i, 0)),
                  c2(cw), c2(vec(conv_b)), c2(vec(ln_g)), c2(vec(ln_b)), c2(w_conv_out)],
        out_specs=pl.BlockSpec((tm, d), lambda b, i: (b * nt + i, 0)),
        scratch_shapes=[pltpu.VMEM((tm + CONV_HALO, d_conv), F32),
                        pltpu.VMEM((SUBLANES - 1, tm + CONV_HALO - SUBLANES, d_conv), F32),
                        pltpu.VMEM((tm, d_conv), F32)],
        compiler_params=_cparams(2), name="convmix",
    )(a3, a3, mg, cw, vec(conv_b), vec(ln_g), vec(ln_b), w_conv_out)


def _nsa_kernel(qt_ref, kcmp_ref, vcmpt_ref, biasc_ref, ks_ref, vst_ref, kw_ref, vwt_ref,
                ng_ref, aggt_ref, bd_ref, o_ref,
                imp_ref, qa_ref, oc_ref, ms_ref, accs_ref, mw_ref, accw_ref, *, ns):
    i = pl.program_id(2)

    def k_tile(ref, j):
        return ref[0, 0, pl.ds(pl.multiple_of(j * TQ, TQ), TQ), :]

    def vt_tile(ref, j):
        return ref[0, 0, :, pl.ds(pl.multiple_of(j * TQ, TQ), TQ)]

    def pipelined(tasks, scores, update):
        pending = [scores(t) for t in tasks[:PIPE_DEPTH]]
        for t, task in enumerate(tasks):
            update(task, pending.pop(0))
            if t + PIPE_DEPTH < len(tasks):
                pending.append(scores(tasks[t + PIPE_DEPTH]))

    def run(tasks):
        def scores(task):
            hh, kt, _, qmat, _, _, row0, pen = task
            s = _dot(kt, qmat)
            if row0 is not None:
                s = s + bd_ref[hh, row0:row0 + TQ, :]
            return s if pen is None else s + pen

        def update(task, s):
            hh, _, vt, _, m_ref, acc_ref, _, _ = task
            m_old = m_ref[hh]
            m_new = jnp.maximum(m_old, jnp.max(s, axis=0, keepdims=True))
            p = jnp.exp2(s - m_new).astype(BF16)
            acc_ref[hh] = jnp.exp2(m_old - m_new) * acc_ref[hh] + _dot(vt, p)
            m_ref[hh] = m_new

        pipelined(tasks, scores, update)

    def slc_tasks(j, row0=None, pen=None):
        kt, vt = k_tile(ks_ref, j), vt_tile(vst_ref, j)
        return [(hh, kt, vt, qa_ref[hh], ms_ref, accs_ref, row0, pen) for hh in range(HPG)]

    def win_tasks(j, row0, pen=None):
        kt, vt = k_tile(kw_ref, j), vt_tile(vwt_ref, j)
        return [(hh, kt, vt, qt_ref[0, hh], mw_ref, accw_ref, row0, pen) for hh in range(HPG)]

    for hh in range(HPG):
        ms_ref[hh] = jnp.full((1, TQ), M_INIT, F32)
        mw_ref[hh] = jnp.full((1, TQ), M_INIT, F32)
        accs_ref[hh] = jnp.zeros((LANES, TQ), F32)
        accw_ref[hh] = jnp.zeros((LANES, TQ), F32)

    p_parts = []

    def cmp_scores(hh):
        return _dot(kcmp_ref[0], qt_ref[0, hh]) + biasc_ref[hh]

    def cmp_update(hh, s_c):
        m_c = jnp.maximum(jnp.max(s_c, axis=0, keepdims=True), M_INIT)
        p_c = jnp.exp2(s_c - m_c)
        l_c = jnp.sum(p_c, axis=0, keepdims=True)
        p_c = p_c * jnp.where(l_c > 0.0, 1.0 / l_c, 0.0)
        oc_ref[hh] = _dot(vcmpt_ref[0, 0], p_c.astype(BF16))[0:HEAD_DIM]
        p_parts.append(p_c)

    pipelined(list(range(HPG)), cmp_scores, cmp_update)

    p_sum = (p_parts[0] + p_parts[1]) + (p_parts[2] + p_parts[3])
    p_hi = p_sum.astype(BF16)
    p_lo = (p_sum - p_hi.astype(F32)).astype(BF16)
    imp = _dot(aggt_ref[...], p_hi) + _dot(aggt_ref[...], p_lo)
    blk = lax.broadcasted_iota(jnp.int32, (ns, TQ), 0)
    cur = (i * TQ + lax.broadcasted_iota(jnp.int32, (ns, TQ), 1)) // L_SLC
    forced = (blk == 0) | (blk == cur) | (blk == cur - 1)
    imp = jnp.where(forced, FORCE, jnp.where(blk <= cur, imp, -FORCE))
    imp_ref[...] = imp

    pen2 = jnp.where(i >= 2, 0.0, NEG)
    pen1 = jnp.where(i >= 1, 0.0, NEG)
    run(win_tasks(jnp.maximum(i - 2, 0), 0, pen2) + win_tasks(jnp.maximum(i - 1, 0), TQ, pen1)
        + win_tasks(i, 2 * TQ))

    n_grp = ns // 8
    grp = [imp[8 * r:8 * r + 8, :] for r in range(n_grp)]
    blk8 = lax.broadcasted_iota(jnp.int32, (8, TQ), 0)
    rank = [jnp.zeros((8, TQ), F32) for _ in range(n_grp)]
    for k in range(ns):
        rowk = imp_ref[k:k + 1, :]
        kg = k // 8
        for r in range(n_grp):
            if r < kg:
                beats = jnp.where(rowk > grp[r], 1.0, 0.0)
            elif r > kg:
                beats = jnp.where(rowk >= grp[r], 1.0, 0.0)
            else:
                beats = jnp.where(blk8 > k - 8 * kg, jnp.where(rowk >= grp[r], 1.0, 0.0),
                                  jnp.where(rowk > grp[r], 1.0, 0.0))
            rank[r] = rank[r] + beats
    rank = jnp.concatenate(rank, axis=0)
    unsel =jnp.where(rank < float(N_SEL), 0.0, -1.0)
    flags = unsel.astype(BF16)
    if ns < HEAD_DIM:
        flags = jnp.concatenate([flags, jnp.zeros((HEAD_DIM - ns, TQ), BF16)], axis=0)
    for hh in range(HPG):
        qa_ref[hh, 0:HEAD_DIM, :] = qt_ref[0, hh, 0:HEAD_DIM, :]
        qa_ref[hh, HEAD_DIM:, :] = flags

    n_far = jnp.maximum(i - 1, 0)

    def far_group(jj, carry):
        tasks = []
        for u in range(FAR_GROUP):
            tasks += slc_tasks(FAR_GROUP * jj + u)
        run(tasks)
        return carry

    lax.fori_loop(0, n_far // FAR_GROUP, far_group, 0)
    for rem in range(1, FAR_GROUP):
        @pl.when(n_far % FAR_GROUP == rem)
        def _(rem=rem):
            tasks = []
            for u in range(rem):
                tasks += slc_tasks(n_far - rem + u)
            run(tasks)

    run(slc_tasks(jnp.maximum(i - 1, 0), TQ, pen1) + slc_tasks(i, 2 * TQ))

    gates_t = jnp.transpose(ng_ref[...])
    outs = []
    for hh in range(HPG):
        a_s, a_w = accs_ref[hh], accw_ref[hh]
        o_s = a_s[0:HEAD_DIM] * (1.0 / a_s[HEAD_DIM:HEAD_DIM + 1])
        o_w = a_w[0:HEAD_DIM] * (1.0 / a_w[HEAD_DIM:HEAD_DIM + 1])
        outs.append(gates_t[3 * hh:3 * hh + 1] * oc_ref[hh]
                    + gates_t[3 * hh + 1:3 * hh + 2] * o_s
                    + gates_t[3 * hh + 2:3 * hh + 3] * o_w)
    o_ref[...] = jnp.transpose(jnp.concatenate(outs, axis=0)).astype(BF16)


def _bias_of(thr_ref, rb_ref, dist, h):
    b = jnp.full(dist.shape, rb_ref[h], F32)
    for k in range(1, N_BUCKETS):
        b = jnp.where(dist >= thr_ref[k], rb_ref[k * N_HEADS + h], b)
    return b


def _cmp_bias_kernel(thr_ref, rb_ref, biasc_ref, *, nc, ncp):
    i = pl.program_id(0)
    c = lax.broadcasted_iota(jnp.int32, (ncp, TQ), 0)
    r = lax.broadcasted_iota(jnp.int32, (ncp, TQ), 1)
    dist_c = i * TQ + r - (c * D_STRIDE + L_CMP - 1)
    ok_c = (dist_c >= 0) & (c < nc)
    for h in range(N_HEADS):
        biasc_ref[h] = jnp.where(ok_c, _bias_of(thr_ref, rb_ref, dist_c, h), NEG)


def _diag_bias_kernel(thr_ref, rb_ref, bd_ref):
    d0 = (lax.broadcasted_iota(jnp.int32, (TQ, TQ), 1)
          - lax.broadcasted_iota(jnp.int32, (TQ, TQ), 0))
    for h in range(N_HEADS):
        far = rb_ref[(N_BUCKETS - 1) * N_HEADS + h]
        bd_ref[h, 0:TQ, :] = jnp.where(d0 < 0, _bias_of(thr_ref, rb_ref, d0 + 2 * TQ, h) - far, NEG)
        bd_ref[h, TQ:2 * TQ, :] = _bias_of(thr_ref, rb_ref, d0 + TQ, h) - far
        bd_ref[h, 2 * TQ:, :] = jnp.where(d0 >= 0, _bias_of(thr_ref, rb_ref, d0, h) - far, NEG)


def _attention_tables(rel_bias, seq):
    ncp = seq // D_STRIDE
    nc = (seq - L_CMP) // D_STRIDE + 1
    ns = seq // L_SLC
    nt = seq // TQ
    n_probe = 2 * MAX_DISTANCE
    buckets = _t5_bucket(jnp.arange(n_probe))
    thr = jnp.sum(buckets[None, :] < jnp.arange(N_BUCKETS)[:, None], axis=1).astype(jnp.int32)
    assert WINDOW == 2 * TQ
    rb = (rel_bias.astype(F32) * LOG2E).reshape(-1)
    bias_c = pl.pallas_call(
        functools.partial(_cmp_bias_kernel, nc=nc, ncp=ncp),
        out_shape=jax.ShapeDtypeStruct((N_HEADS, ncp, seq), F32),
        grid_spec=pltpu.PrefetchScalarGridSpec(
            num_scalar_prefetch=2, grid=(nt,), in_specs=[],
            out_specs=pl.BlockSpec((N_HEADS, ncp, TQ), lambda i, *_: (0, 0, i))),
        compiler_params=_cparams(1), name="cmp_bias")(thr, rb)
    bd = pl.pallas_call(
        _diag_bias_kernel, out_shape=jax.ShapeDtypeStruct((N_HEADS, 3 * TQ, TQ), F32),
        grid_spec=pltpu.PrefetchScalarGridSpec(
            num_scalar_prefetch=2, grid=(1,), in_specs=[],
            out_specs=pl.BlockSpec((N_HEADS, 3 * TQ, TQ), lambda i, *_: (0, 0, 0))),
        compiler_params=_cparams(1), name="diag_bias")(thr, rb)
    c_start = jnp.arange(ncp) * D_STRIDE
    c_end = c_start + L_CMP - 1
    s_start = jnp.arange(ns) * L_SLC
    aggt = ((c_end[None, :] >= s_start[:, None]) & (c_start[None, :] <= s_start[:, None] + L_SLC - 1)
            & (jnp.arange(ncp)[None, :] < nc)).astype(BF16)
    return bias_c, bd, aggt


def _nsa(qt, kcmp, vcmpt, ks, vst, kw, vwt, ng, rel_bias, batch, seq):
    first_far = TQ + 1
    assert MAX_EXACT + int(math.log(first_far / MAX_EXACT) / math.log(MAX_DISTANCE / MAX_EXACT)
                           * (N_BUCKETS - MAX_EXACT)) >= N_BUCKETS - 1, "far tiles need one bucket"
    ncp = seq // D_STRIDE
    ns = seq // L_SLC
    nt = seq // TQ
    bias_c, bd, aggt = _attention_tables(rel_bias, seq)
    kspec = pl.BlockSpec((1, 1, seq, LANES), lambda b, g, i: (b, g, 0, 0))
    vtspec = pl.BlockSpec((1, 1, LANES, seq), lambda b, g, i: (b, g, 0, 0))
    state = lambda rows: pltpu.VMEM((HPG, rows, TQ), F32)
    grid_spec = pltpu.PrefetchScalarGridSpec(
        num_scalar_prefetch=0, grid=(batch, N_KV_HEADS, nt),
        in_specs=[pl.BlockSpec((1, HPG, LANES, TQ), lambda b, g, i: (b, g, 0, i)),
                  pl.BlockSpec((1, ncp, LANES), lambda b, g, i: (b, 0, g)),
                  pl.BlockSpec((1, 1, LANES, ncp), lambda b, g, i: (b, g, 0, 0)),
                  pl.BlockSpec((HPG, ncp, TQ), lambda b, g, i: (g, 0, i)),
                  kspec, vtspec, kspec, vtspec,
                  pl.BlockSpec((TQ, LANES), lambda b, g, i: (b * nt + i, g)),
                  pl.BlockSpec((ns, ncp), lambda b, g, i: (0, 0)),
                  pl.BlockSpec((HPG, 3 * TQ, TQ), lambda b, g, i: (g, 0, 0))],
        out_specs=pl.BlockSpec((TQ, HPG * HEAD_DIM), lambda b, g, i: (b * nt + i, g)),
        scratch_shapes=[pltpu.VMEM((ns, TQ), F32),
                        pltpu.VMEM((HPG, LANES, TQ), BF16),
                        state(HEAD_DIM),
                        state(1), state(LANES),
                        state(1), state(LANES)])
    return pl.pallas_call(
        functools.partial(_nsa_kernel, ns=ns),
        out_shape=jax.ShapeDtypeStruct((batch * seq, N_HEADS * HEAD_DIM), BF16),
        grid_spec=grid_spec, compiler_params=_cparams(3), name="nsa",
    )(qt, kcmp, vcmpt, bias_c, ks, vst, kw, vwt, ng, aggt, bd)


def _merge_kernel(x_ref, ya_ref, o_ref, gb_ref, wao_ref, wout_ref, x1_ref):
    y_attn = _dot(o_ref[...], wao_ref[...])
    y = ya_ref[...].astype(F32) + gb_ref[...].astype(F32) * y_attn
    x1_ref[...] = x_ref[...] + _dot(y.astype(BF16), wout_ref[...])


def _merge(x2, ya, o, mg, w_attn_out, w_out, tm=512):
    t, d = x2.shape
    c2 = lambda a: pl.BlockSpec(a.shape, lambda r: (0, 0))
    return pl.pallas_call(
        _merge_kernel, out_shape=jax.ShapeDtypeStruct((t, d), F32), grid=(t // tm,),
        in_specs=[pl.BlockSpec((tm, d), lambda r: (r, 0)),
                  pl.BlockSpec((tm, d), lambda r: (r, 0)),
                  pl.BlockSpec((tm, o.shape[1]), lambda r: (r, 0)),
                  pl.BlockSpec((tm, d), lambda r: (r, 1)),
                  c2(w_attn_out), c2(w_out)],
        out_specs=pl.BlockSpec((tm, d), lambda r: (r, 0)),
        compiler_params=_cparams(1), name="merge",
    )(x2, ya, o, mg, w_attn_out, w_out)


def _ffn_kernel(x1_ref, p_ref, nf_ref, wup_ref, cw_ref, cb_ref, wd_ref,
                np_ref, wpg_ref, wpe_ref, nfin_ref, out_ref, carry_ref,
                *, tm, tiles_per_seq, d_ff, chunks):
    r = pl.program_id(0)

    @pl.when((r % tiles_per_seq) == 0)
    def _():
        carry_ref[...] = jnp.zeros_like(carry_ref)

    x1 = x1_ref[...]
    hf = _rms(x1, nf_ref[...]).astype(BF16)

    def conv3(c0, width):
        up = _dot(hf, wup_ref[:, c0:c0 + width])
        rowi = lax.broadcasted_iota(jnp.int32, up.shape, 0)
        prev = carry_ref[:, c0:c0 + width]
        s1 = jnp.where(rowi == 0, prev[7:8, :], pltpu.roll(up, 1, 0))
        s2 = jnp.where(rowi == 0, prev[6:7, :],
                       jnp.where(rowi == 1, prev[7:8, :], pltpu.roll(up, 2, 0)))
        carry_ref[:, c0:c0 + width] = up[tm - 8:, :]
        return (cw_ref[0:1, c0:c0 + width] * s2 + cw_ref[1:2, c0:c0 + width] * s1
                + cw_ref[2:3, c0:c0 + width] * up + cb_ref[:, c0:c0 + width])

    acc = None
    for c0, width in chunks:
        act = (_gelu(conv3(c0, width)) * conv3(d_ff + c0, width)).astype(BF16)
        part = _dot(act, wd_ref[c0:c0 + width, :])
        acc = part if acc is None else acc + part

    x2 = x1 + acc
    pg = _sigmoid(_dot(_rms(x2, np_ref[...]).astype(BF16), wpg_ref[...]))
    pe = _dot(p_ref[...].astype(BF16), wpe_ref[...])
    out_ref[...] = _rms(x2 + pg * pe, nfin_ref[...])


def _ffn(x1, p2, norm_ffn, w_up, ffn_dw_w, ffn_dw_b, w_down, norm_ple, w_ple_gate, w_ple,
         norm_final, seq, tm=512):
    t, d = x1.shape
    d_ff = w_down.shape[0]
    assert d_ff % MXU_DEPTH == 0
    chunks, c0 = [], 0
    while c0 < d_ff:
        width = min(FFN_CHUNK, d_ff - c0)
        chunks.append((c0, width))
        c0 += width
    tiles_per_seq = seq // tm
    cw = jnp.concatenate([ffn_dw_w, jnp.zeros((8 - FFN_CONV_WIDTH, 2 * d_ff), ffn_dw_w.dtype)], 0)
    vec = lambda v: v.reshape(1, -1)
    const = lambda a: pl.BlockSpec(a.shape, lambda r: (0, 0), pipeline_mode=pl.Buffered(1))
    rows = lambda n: pl.BlockSpec((tm, n), lambda r: (r, 0))
    kern = functools.partial(_ffn_kernel, tm=tm, tiles_per_seq=tiles_per_seq, d_ff=d_ff,
                             chunks=tuple(chunks))
    operands = (x1, p2, vec(norm_ffn), w_up, cw, vec(ffn_dw_b), w_down,
                vec(norm_ple), w_ple_gate, w_ple, vec(norm_final))
    return pl.pallas_call(
        kern, out_shape=jax.ShapeDtypeStruct((t, d), F32), grid=(t // tm,),
        in_specs=[rows(d), rows(p2.shape[1])] + [const(a) for a in operands[2:]],
        out_specs=rows(d),
        scratch_shapes=[pltpu.VMEM((8, 2 * d_ff), F32)],
        compiler_params=_cparams(1), name="ffn",
    )(*operands)


def kernel(x, p, rel_bias, norm_mix, w_in, conv_dw_w, conv_dw_b, conv_ln_g, conv_ln_b, w_conv_out,
           cmp_pe_k, cmp_pe_v, w_ck1, w_ck2, w_cv1, w_cv2, w_attn_out, w_out, norm_ffn, w_up,
           ffn_dw_w, ffn_dw_b, w_down, norm_ple, w_ple_gate, w_ple, norm_final):
    batch, seq, d = x.shape
    depth = w_in.shape[0]
    x2 = x.reshape(batch * seq, d)
    for i in range(depth):
        a, qt, kc, vc, ks, vst, kw, vwt, ng, mg = _inproj(
            x2, norm_mix[i], _prep_inproj_weights(w_in[i], d), batch, seq)
        kcmp, vcmpt = _compress(kc, vc,
                               _prep_compress_weights(cmp_pe_k[i], w_ck1[i], w_ck2[i]),
                               _prep_compress_weights(cmp_pe_v[i], w_cv1[i], w_cv2[i]),
                               batch, seq)
        ya = _convmix(a, mg, conv_dw_w[i], conv_dw_b[i], conv_ln_g[i], conv_ln_b[i],
                      w_conv_out[i].astype(BF16), batch, seq)
        o = _nsa(qt, kcmp, vcmpt, ks, vst, kw, vwt, ng, rel_bias, batch, seq)
        x1 = _merge(x2, ya, o, mg, w_attn_out[i].astype(BF16), w_out[i].astype(BF16))
        assert i == depth - 1, "the final RMSNorm is fused into the (single) layer's MLP kernel"
        x2 = _ffn(x1, p[i].reshape(batch * seq, -1), norm_ffn[i], w_up[i].astype(BF16),
                  ffn_dw_w[i], ffn_dw_b[i], w_down[i].astype(BF16), norm_ple[i],
                  w_ple_gate[i].astype(BF16), w_ple[i].astype(BF16), norm_final, seq)
    return x2.reshape(batch, seq, d)
```

```python
import functools
import math

import jax
import jax.numpy as jnp
import numpy as np
from jax import lax
from jax.experimental import pallas as pl
from jax.experimental.pallas import tpu as pltpu

N_HEADS = 8
HEAD_DIM = 64
N_KV_HEADS = 2
HPG = N_HEADS // N_KV_HEADS
L_CMP = 32
D_STRIDE = 16
CMP_HIDDEN = 256
L_SLC = 64
N_SEL = 16
WINDOW = 512
N_BUCKETS = 32
MAX_EXACT = N_BUCKETS // 2
MAX_DISTANCE = 128
CONV_WIDTH = 31
FFN_CONV_WIDTH = 3
EPS = 1e-6
FORCE = 1e4

LANES = 128
MXU_DEPTH = 256
FFN_CHUNK = 6 * MXU_DEPTH
NEG = -1e30
M_INIT = -1e29
LOG2E = 1.4426950408889634
PIPE_DEPTH = 6
FAR_GROUP = 4
TQ = 256
PV_ROWS = HEAD_DIM + 16
SUBLANES = 8
CONV_HALO = 32
CONV_ROWS = 64
VMEM_LIMIT = 56 * 1024 * 1024

F32 = jnp.float32
BF16 = jnp.bfloat16


def _cparams(n_axes):
    return pltpu.CompilerParams(dimension_semantics=("arbitrary",) * n_axes,
                                vmem_limit_bytes=VMEM_LIMIT)


def _dot(a, b):
    return jnp.dot(a, b, preferred_element_type=F32)


def _dot_nt(a, b):
    return lax.dot_general(a, b, (((1,), (1,)), ((), ())), preferred_element_type=F32)


def _rms(xf, g):
    return xf * lax.rsqrt(jnp.mean(xf * xf, axis=-1, keepdims=True) + EPS) * g


def _sigmoid(x):
    return 1.0 / (1.0 + jnp.exp(-x))


def _gelu(x):
    return 0.5 * x * (1.0 + jnp.tanh(0.7978845608028654 * (x + 0.044715 * x * x * x)))


def _t5_bucket(dist):
    n = jnp.maximum(dist, 0)
    nf = jnp.maximum(n, MAX_EXACT).astype(F32)
    large = MAX_EXACT + (jnp.log(nf / MAX_EXACT) / math.log(MAX_DISTANCE / MAX_EXACT)
                         * (N_BUCKETS - MAX_EXACT)).astype(jnp.int32)
    large = jnp.minimum(large, N_BUCKETS - 1)
    return jnp.where(n < MAX_EXACT, n, large)


def _inproj_kernel(x_ref, g_ref, wu_ref, wq_ref, wkv_ref, wng_ref, wmg_ref,
                   a_ref, qt_ref, kc_ref, vc_ref, ks_ref, vst_ref, kw_ref, vwt_ref, ng_ref, mg_ref,
                   *, tm, tiles_per_seq, d_conv):
    r = pl.program_id(0)
    h = _rms(x_ref[...], g_ref[...]).astype(BF16)

    u = _dot(h, wu_ref[...])
    a_ref[...] = (u[:, :d_conv] * _sigmoid(u[:, d_conv:])).astype(BF16)

    zqt = jnp.transpose(_dot(h, wq_ref[...]))
    zero_rows = jnp.zeros((HEAD_DIM, tm), BF16)
    for hh in range(N_HEADS):
        qt_ref[0, hh, 0:HEAD_DIM, :] = zqt[hh * HEAD_DIM:(hh + 1) * HEAD_DIM].astype(BF16)
        qt_ref[0, hh, HEAD_DIM:, :] = zero_rows

    zkv = _dot(h, wkv_ref[...])
    kc_ref[...] = zkv[:, 0:LANES].astype(BF16)
    vc_ref[...] = zkv[:, LANES:2 * LANES].astype(BF16)
    lane = lax.broadcasted_iota(jnp.int32, (tm, LANES), 1)
    row = lax.broadcasted_iota(jnp.int32, (tm, LANES), 0)
    spos = (r % tiles_per_seq) * tm + row
    lo = lane < HEAD_DIM
    blk_tag = jnp.where((lane - HEAD_DIM) == spos // L_SLC, -NEG, 0.0)
    ones_rows = jnp.where(lax.broadcasted_iota(jnp.int32, (HEAD_DIM, tm), 0) < 8, 1.0, 0.0)
    ones_rows = ones_rows.astype(BF16)
    for g in range(N_KV_HEADS):
        for k_ref, vt_ref, base, tag in ((ks_ref, vst_ref, (2 + g) * LANES, blk_tag),
                                         (kw_ref, vwt_ref, (4 + g) * LANES, 0.0)):
            pair = zkv[:, base:base + LANES]
            k_ref[0, g] = jnp.where(lo, pair, tag).astype(BF16)
            vt_ref[0, g, 0:HEAD_DIM, :] = jnp.transpose(pair)[HEAD_DIM:].astype(BF16)
            vt_ref[0, g, HEAD_DIM:, :] = ones_rows

    ng_ref[...] = _sigmoid(_dot(h, wng_ref[...]))
    mg_ref[...] = _sigmoid(_dot(h, wmg_ref[...])).astype(BF16)


def _prep_inproj_weights(w_in, d_model):
    d_conv = d_model // 2
    n_conv = 2 * d_conv
    n_q = N_HEADS * HEAD_DIM
    n_kv = 6 * N_KV_HEADS * HEAD_DIM
    n_ng = 3 * N_HEADS
    o = 0
    wu = w_in[:, o:o + n_conv]; o += n_conv
    wq = w_in[:, o:o + n_q]; o += n_q
    wkv = w_in[:, o:o + n_kv]; o += n_kv
    wng = w_in[:, o:o + n_ng]; o += n_ng
    wmg = w_in[:, o:]
    wq_p = wq * (LOG2E / math.sqrt(HEAD_DIM))
    kvcols = [wkv[:, 0:LANES], wkv[:, LANES:2 * LANES]]
    for k_kind in (2, 4):
        for g in range(N_KV_HEADS):
            for kind in (k_kind, k_kind + 1):
                c0 = kind * N_KV_HEADS * HEAD_DIM + g * HEAD_DIM
                kvcols.append(wkv[:, c0:c0 + HEAD_DIM])
    wkv_p = jnp.concatenate(kvcols, axis=1)
    per_g = HPG * 3
    ngcols = []
    for g in range(N_KV_HEADS):
        ngcols += [wng[:, g * per_g:(g + 1) * per_g],
                   jnp.zeros((d_model, LANES - per_g), w_in.dtype)]
    wng_p = jnp.concatenate(ngcols, axis=1)
    return tuple(w.astype(BF16) for w in (wu, wq_p, wkv_p, wng_p, wmg))


def _inproj(x2, norm_g, weights, batch, seq, tm=512):
    t, d = x2.shape
    wu, wq, wkv, wng, wmg = weights
    d_conv = d // 2
    tiles_per_seq = seq // tm
    n_tiles = t // tm
    full = lambda w: pl.BlockSpec(w.shape, lambda r: (0, 0))
    row = lambda n: pl.BlockSpec((tm, n), lambda r: (r, 0))
    headed = lambda nh: pl.BlockSpec((1, nh, tm, LANES),
                                     lambda r: (r // tiles_per_seq, 0, r % tiles_per_seq, 0))
    headed_t = lambda nh: pl.BlockSpec((1, nh, LANES, tm),
                                       lambda r: (r // tiles_per_seq, 0, 0, r % tiles_per_seq))
    k_shape = jax.ShapeDtypeStruct((batch, N_KV_HEADS, seq, LANES), BF16)
    vt_shape = jax.ShapeDtypeStruct((batch, N_KV_HEADS, LANES, seq), BF16)
    out_shape = (
        jax.ShapeDtypeStruct((t, d_conv), BF16),
        jax.ShapeDtypeStruct((batch, N_HEADS, LANES, seq), BF16),
        jax.ShapeDtypeStruct((t, LANES), BF16),
        jax.ShapeDtypeStruct((t, LANES), BF16),
        k_shape, vt_shape, k_shape, vt_shape,
        jax.ShapeDtypeStruct((t, N_KV_HEADS * LANES), F32),
        jax.ShapeDtypeStruct((t, 2 * d), BF16),
    )
    out_specs = (row(d_conv), headed_t(N_HEADS), row(LANES), row(LANES),
                 headed(N_KV_HEADS), headed_t(N_KV_HEADS), headed(N_KV_HEADS),
                 headed_t(N_KV_HEADS), row(N_KV_HEADS * LANES), row(2 * d))
    kern = functools.partial(_inproj_kernel, tm=tm, tiles_per_seq=tiles_per_seq, d_conv=d_conv)
    return pl.pallas_call(
        kern, out_shape=out_shape, grid=(n_tiles,),
        in_specs=[row(d), pl.BlockSpec((1, d), lambda r: (0, 0)),
                  full(wu), full(wq), full(wkv), full(wng), full(wmg)],
        out_specs=out_specs, compiler_params=_cparams(1), name="inproj",
    )(x2, norm_g.reshape(1, d), wu, wq, wkv, wng, wmg)


def _compress_kernel(rk_ref, rv_ref, pek_ref, pev_ref, w1k_ref, w1v_ref, w2k_ref, w2v_ref,
                     kcmp_ref, vcmp_ref, *, ncp):
    def one(r_ref, pe_ref, w1_ref, w2_ref):
        r = r_ref[0]
        top = _dot(r, w1_ref[0])
        bot = _dot(r, w1_ref[1])
        pe_h = _dot(pe_ref[0], w1_ref[0]) + _dot(pe_ref[1], w1_ref[1])
        nxt = pltpu.roll(bot, ncp - 1, 0)
        rowi = lax.broadcasted_iota(jnp.int32, top.shape, 0)
        hid = top + jnp.where(rowi == ncp - 1, 0.0, nxt) + pe_h[0:1, :]
        return _dot(_gelu(hid).astype(BF16), w2_ref[...])

    kcmp_ref[0] = one(rk_ref, pek_ref, w1k_ref, w2k_ref).astype(BF16)
    v = one(rv_ref, pev_ref, w1v_ref, w2v_ref)
    rowt = lax.broadcasted_iota(jnp.int32, (LANES, ncp), 0)
    ones_rows = (rowt >= HEAD_DIM) & (rowt < HEAD_DIM + 8)
    for g in range(N_KV_HEADS):
        vt = jnp.transpose(v[:, g * LANES:(g + 1) * LANES])
        vcmp_ref[0, g] = jnp.where(ones_rows, 1.0, vt).astype(BF16)


def _prep_compress_weights(pe, w1, w2):
    half = L_CMP // 2
    eye = jnp.eye(N_KV_HEADS, dtype=w1.dtype)
    w1r = w1.reshape(2, half, 1, HEAD_DIM, 1, CMP_HIDDEN)
    w1big = (w1r * eye[None, None, :, None, :, None]).reshape(
        2, half * N_KV_HEADS * HEAD_DIM, N_KV_HEADS * CMP_HIDDEN).astype(BF16)
    w2p = jnp.pad(w2, ((0, 0), (0, LANES - HEAD_DIM)))
    w2big = (w2p[None, :, None, :] * eye[:, None, :, None]).reshape(
        N_KV_HEADS * CMP_HIDDEN, N_KV_HEADS * LANES).astype(BF16)
    per = pe.reshape(2, half, 1, HEAD_DIM)
    pebig = jnp.broadcast_to(per, (2, half, N_KV_HEADS, HEAD_DIM)).reshape(2, 1, -1)
    pebig = jnp.broadcast_to(pebig, (2, 8, pebig.shape[-1])).astype(BF16)
    return pebig, w1big, w2big


def _compress(kc, vc, wk, wv, batch, seq):
    ncp = seq // D_STRIDE
    width = D_STRIDE * LANES
    rk = kc.reshape(batch, ncp, width)
    rv = vc.reshape(batch, ncp, width)
    pek, w1k, w2k = wk
    pev, w1v, w2v = wv
    c3 = lambda a: pl.BlockSpec(a.shape, lambda b: (0, 0, 0))
    c2 = lambda a: pl.BlockSpec(a.shape, lambda b: (0, 0))
    rspec = pl.BlockSpec((1, ncp, width), lambda b: (b, 0, 0))
    kspec = pl.BlockSpec((1, ncp, N_KV_HEADS * LANES), lambda b: (b, 0, 0))
    kshape = jax.ShapeDtypeStruct((batch, ncp, N_KV_HEADS * LANES), BF16)
    vspec = pl.BlockSpec((1, N_KV_HEADS, LANES, ncp), lambda b: (b, 0, 0, 0))
    vshape = jax.ShapeDtypeStruct((batch, N_KV_HEADS, LANES, ncp), BF16)
    return pl.pallas_call(
        functools.partial(_compress_kernel, ncp=ncp), out_shape=(kshape, vshape), grid=(batch,),
        in_specs=[rspec, rspec, c3(pek), c3(pev), c3(w1k), c3(w1v), c2(w2k), c2(w2v)],
        out_specs=(kspec, vspec), compiler_params=_cparams(1), name="compress",
    )(rk, rv, pek, pev, w1k, w1v, w2k, w2v)


def _convmix_kernel(a_ref, halo_ref, ga_ref, cw_ref, cb_ref, lg_ref, lb_ref, wo_ref,
                    ya_ref, ext_ref, sh_ref, y_ref, *, tm):
    i = pl.program_id(1)
    halo = halo_ref[0].astype(F32)
    ext_ref[0:CONV_HALO, :] = jnp.where(i == 0, 0.0, halo)
    ext_ref[CONV_HALO:, :] = a_ref[0].astype(F32)
    span = sh_ref.shape[1]
    for ph in range(1, SUBLANES):
        sh_ref[ph - 1] = ext_ref[ph:ph + span, :]
    off = CONV_HALO - (CONV_WIDTH - 1)

    def rows_chunk(cidx, carry):
        r0 = pl.multiple_of(cidx * CONV_ROWS, CONV_ROWS)
        acc = jnp.zeros((CONV_ROWS, a_ref.shape[-1]), F32) + cb_ref[...]
        for j in range(CONV_WIDTH):
            ph = (off + j) % SUBLANES
            base = off + j - ph
            if ph == 0:
                xs = ext_ref[pl.ds(r0 + base, CONV_ROWS), :]
            else:
                xs = sh_ref[ph - 1, pl.ds(r0 + base, CONV_ROWS), :]
            acc = acc + jnp.tile(cw_ref[j], (CONV_ROWS // SUBLANES, 1)) * xs
        y_ref[pl.ds(r0, CONV_ROWS), :] = acc
        return carry

    lax.fori_loop(0, tm // CONV_ROWS, rows_chunk, 0)
    acc = y_ref[...]
    mu = jnp.mean(acc, axis=-1, keepdims=True)
    cen = acc - mu
    var = jnp.mean(cen * cen, axis=-1, keepdims=True)
    y = cen * lax.rsqrt(var + EPS) * lg_ref[...] + lb_ref[...]
    y = y * _sigmoid(y)
    yc = _dot(y.astype(BF16), wo_ref[...])
    ya_ref[...] = (ga_ref[...].astype(F32) * yc).astype(BF16)


def _convmix(a, mg, conv_w, conv_b, ln_g, ln_b, w_conv_out, batch, seq, tm=512):
    t, d_conv = a.shape
    d = w_conv_out.shape[1]
    a3 = a.reshape(batch, seq, d_conv)
    nt = seq // tm
    hb = tm // CONV_HALO
    cw = jnp.broadcast_to(conv_w[:, None, :], (CONV_WIDTH, SUBLANES, d_conv))
    vec = lambda v: v.reshape(1, -1)
    c2 = lambda a_: pl.BlockSpec(a_.shape, lambda b, i: (0,) * a_.ndim)
    return pl.pallas_call(
        functools.partial(_convmix_kernel, tm=tm),
        out_shape=jax.ShapeDtypeStruct((t, d), BF16), grid=(batch, nt),
        in_specs=[pl.BlockSpec((1, tm, d_conv), lambda b, i: (b, i, 0)),
                  pl.BlockSpec((1, CONV_HALO, d_conv),
                               lambda b, i: (b, jnp.maximum(i * hb - 1, 0), 0)),
                  pl.BlockSpec((tm, d), lambda b, i: (b * nt + i, 0)),
                  c2(cw), c2(vec(conv_b)), c2(vec(ln_g)), c2(vec(ln_b)), c2(w_conv_out)],
        out_specs=pl.BlockSpec((tm, d), lambda b, i: (b * nt + i, 0)),
        scratch_shapes=[pltpu.VMEM((tm + CONV_HALO, d_conv), F32),
                        pltpu.VMEM((SUBLANES - 1, tm + CONV_HALO - SUBLANES, d_conv), F32),
                        pltpu.VMEM((tm, d_conv), F32)],
        compiler_params=_cparams(2), name="convmix",
    )(a3, a3, mg, cw, vec(conv_b), vec(ln_g), vec(ln_b), w_conv_out)


def _nsa_kernel(qt_ref, kcmp_ref, vcmpt_ref, biasc_ref, ks_ref, vst_ref, kw_ref, vwt_ref,
                ng_ref, aggt_ref, bd_ref, o_ref,
                imp_ref, qa_ref, oc_ref, ms_ref, accs_ref, mw_ref, accw_ref, *, ns):
    i = pl.program_id(2)

    def k_tile(ref, j):
        return ref[0, 0, pl.ds(pl.multiple_of(j * TQ, TQ), TQ), :]

    def vt_tile(ref, j):
        return ref[0, 0, 0:PV_ROWS, pl.ds(pl.multiple_of(j * TQ, TQ), TQ)]

    def pipelined(tasks, scores, update):
        pending = [scores(t) for t in tasks[:PIPE_DEPTH]]
        for t, task in enumerate(tasks):
            update(task, pending.pop(0))
            if t + PIPE_DEPTH < len(tasks):
                pending.append(scores(tasks[t + PIPE_DEPTH]))

    def run(tasks):
        def scores(task):
            hh, kt, _, qmat, _, _, row0, pen = task
            s = _dot(kt, qmat)
            if row0 is not None:
                s = s + bd_ref[hh, row0:row0 + TQ, :]
            return s if pen is None else s + pen

        def update(task, s):
            hh, _, vt, _, m_ref, acc_ref, _, _ = task
            m_old = m_ref[hh]
            m_new = jnp.maximum(m_old, jnp.max(s, axis=0, keepdims=True))
            p = jnp.exp2(s - m_new).astype(BF16)
            acc_ref[hh] = jnp.exp2(m_old - m_new) * acc_ref[hh] + _dot(vt, p)
            m_ref[hh] = m_new

        pipelined(tasks, scores, update)

    def slc_tasks(j, row0=None, pen=None):
        kt, vt = k_tile(ks_ref, j), vt_tile(vst_ref, j)
        return [(hh, kt, vt, qa_ref[hh], ms_ref, accs_ref, row0, pen) for hh in range(HPG)]

    def win_tasks(j, row0, pen=None):
        kt, vt = k_tile(kw_ref, j), vt_tile(vwt_ref, j)
        return [(hh, kt, vt, qt_ref[0, hh], mw_ref, accw_ref, row0, pen) for hh in range(HPG)]

    for hh in range(HPG):
        ms_ref[hh] = jnp.full((1, TQ), M_INIT, F32)
        mw_ref[hh] = jnp.full((1, TQ), M_INIT, F32)
        accs_ref[hh] = jnp.zeros((PV_ROWS, TQ), F32)
        accw_ref[hh] = jnp.zeros((PV_ROWS, TQ), F32)

    p_parts = []

    def cmp_scores(hh):
        return _dot(kcmp_ref[0], qt_ref[0, hh]) + biasc_ref[hh]

    def cmp_update(hh, s_c):
        m_c = jnp.maximum(jnp.max(s_c, axis=0, keepdims=True), M_INIT)
        p_c = jnp.exp2(s_c - m_c)
        l_c = jnp.sum(p_c, axis=0, keepdims=True)
        p_c = p_c * jnp.where(l_c > 0.0, 1.0 / l_c, 0.0)
        oc_ref[hh] = _dot(vcmpt_ref[0, 0, 0:HEAD_DIM, :], p_c.astype(BF16))
        p_parts.append(p_c)

    pipelined(list(range(HPG)), cmp_scores, cmp_update)

    p_sum = (p_parts[0] + p_parts[1]) + (p_parts[2] + p_parts[3])
    p_hi = p_sum.astype(BF16)
    p_lo = (p_sum - p_hi.astype(F32)).astype(BF16)
    imp = _dot(aggt_ref[...], p_hi) + _dot(aggt_ref[...], p_lo)
    blk = lax.broadcasted_iota(jnp.int32, (ns, TQ), 0)
    cur = (i * TQ + lax.broadcasted_iota(jnp.int32, (ns, TQ), 1)) // L_SLC
    forced = (blk == 0) | (blk == cur) | (blk == cur - 1)
    imp = jnp.where(forced, FORCE, jnp.where(blk <= cur, imp, -FORCE))
    imp_ref[...] = imp

    pen2 = jnp.where(i >= 2, 0.0, NEG)
    pen1 = jnp.where(i >= 1, 0.0, NEG)
    run(win_tasks(jnp.maximum(i - 2, 0), 0, pen2) + win_tasks(jnp.maximum(i - 1, 0), TQ, pen1)
        + win_tasks(i, 2 * TQ))

    n_grp = ns // 8
    grp = [imp[8 * r:8 * r + 8, :] for r in range(n_grp)]
    blk8 = lax.broadcasted_iota(jnp.int32, (8, TQ), 0)
    rank = [jnp.zeros((8, TQ), F32) for _ in range(n_grp)]
    for k in range(ns):
        rowk = imp_ref[k:k + 1, :]
        kg = k // 8
        for r in range(n_grp):
            if r < kg:
                beats = jnp.where(rowk > grp[r], 1.0, 0.0)
            elif r > kg:
                beats = jnp.where(rowk >= grp[r], 1.0, 0.0)
            else:
                beats = jnp.where(blk8 > k - 8 * kg, jnp.where(rowk >= grp[r], 1.0, 0.0),
                                  jnp.where(rowk > grp[r], 1.0, 0.0))
            rank[r] = rank[r] + beats
    rank = jnp.concatenate(rank, axis=0)
    unsel =jnp.where(rank < float(N_SEL), 0.0, -1.0)
    flags = unsel.astype(BF16)
    if ns < HEAD_DIM:
        flags = jnp.concatenate([flags, jnp.zeros((HEAD_DIM - ns, TQ), BF16)], axis=0)
    for hh in range(HPG):
        qa_ref[hh, 0:HEAD_DIM, :] = qt_ref[0, hh, 0:HEAD_DIM, :]
        qa_ref[hh, HEAD_DIM:, :] = flags

    n_far = jnp.maximum(i - 1, 0)

    def far_group(jj, carry):
        tasks = []
        for u in range(FAR_GROUP):
            tasks += slc_tasks(FAR_GROUP * jj + u)
        run(tasks)
        return carry

    lax.fori_loop(0, n_far // FAR_GROUP, far_group, 0)
    for rem in range(FAR_GROUP):
        @pl.when(n_far % FAR_GROUP == rem)
        def _(rem=rem):
            tasks = []
            for u in range(rem):
                tasks += slc_tasks(n_far - rem + u)
            run(tasks + slc_tasks(jnp.maximum(i - 1, 0), TQ, pen1) + slc_tasks(i, 2 * TQ))

    gates_t = jnp.transpose(ng_ref[...])
    outs = []
    for hh in range(HPG):
        a_s, a_w = accs_ref[hh], accw_ref[hh]
        o_s = a_s[0:HEAD_DIM] * (1.0 / a_s[HEAD_DIM:HEAD_DIM + 1])
        o_w = a_w[0:HEAD_DIM] * (1.0 / a_w[HEAD_DIM:HEAD_DIM + 1])
        outs.append(gates_t[3 * hh:3 * hh + 1] * oc_ref[hh]
                    + gates_t[3 * hh + 1:3 * hh + 2] * o_s
                    + gates_t[3 * hh + 2:3 * hh + 3] * o_w)
    o_ref[...] = jnp.transpose(jnp.concatenate(outs, axis=0)).astype(BF16)


def _bias_of(thr_ref, rb_ref, dist, h):
    b = jnp.full(dist.shape, rb_ref[h], F32)
    for k in range(1, N_BUCKETS):
        b = jnp.where(dist >= thr_ref[k], rb_ref[k * N_HEADS + h], b)
    return b


def _cmp_bias_kernel(thr_ref, rb_ref, biasc_ref, *, nc, ncp):
    i = pl.program_id(0)
    per_tile = TQ // D_STRIDE
    band = 2 * per_tile
    assert TQ + D_STRIDE - (L_CMP - 1) >= MAX_DISTANCE

    def dist_of(c0, rows):
        c = c0 + lax.broadcasted_iota(jnp.int32, (rows, TQ), 0)
        r = lax.broadcasted_iota(jnp.int32, (rows, TQ), 1)
        dist = i * TQ + r - (c * D_STRIDE + L_CMP - 1)
        return dist, (dist >= 0) & (c < nc)

    _, ok_all = dist_of(0, ncp)
    band0 = pl.multiple_of(jnp.maximum(i * per_tile - per_tile, 0), per_tile)
    dist_b, ok_b = dist_of(band0, band)
    for h in range(N_HEADS):
        biasc_ref[h] = jnp.where(ok_all, rb_ref[(N_BUCKETS - 1) * N_HEADS + h], NEG)
        biasc_ref[h, pl.ds(band0, band), :] = jnp.where(
            ok_b, _bias_of(thr_ref, rb_ref, dist_b, h), NEG)


def _diag_bias_kernel(thr_ref, rb_ref, bd_ref):
    d0 = (lax.broadcasted_iota(jnp.int32, (TQ, TQ), 1)
          - lax.broadcasted_iota(jnp.int32, (TQ, TQ), 0))
    for h in range(N_HEADS):
        far = rb_ref[(N_BUCKETS - 1) * N_HEADS + h]
        bd_ref[h, 0:TQ, :] = jnp.where(d0 < 0, _bias_of(thr_ref, rb_ref, d0 + 2 * TQ, h) - far, NEG)
        bd_ref[h, TQ:2 * TQ, :] = _bias_of(thr_ref, rb_ref, d0 + TQ, h) - far
        bd_ref[h, 2 * TQ:, :] = jnp.where(d0 >= 0, _bias_of(thr_ref, rb_ref, d0, h) - far, NEG)


def _attention_tables(rel_bias, seq):
    ncp = seq // D_STRIDE
    nc = (seq - L_CMP) // D_STRIDE + 1
    ns = seq // L_SLC
    nt = seq // TQ
    n_probe = 2 * MAX_DISTANCE
    buckets = _t5_bucket(jnp.arange(n_probe))
    thr = jnp.sum(buckets[None, :] < jnp.arange(N_BUCKETS)[:, None], axis=1).astype(jnp.int32)
    assert WINDOW == 2 * TQ
    rb = (rel_bias.astype(F32) * LOG2E).reshape(-1)
    bias_c = pl.pallas_call(
        functools.partial(_cmp_bias_kernel, nc=nc, ncp=ncp),
        out_shape=jax.ShapeDtypeStruct((N_HEADS, ncp, seq), F32),
        grid_spec=pltpu.PrefetchScalarGridSpec(
            num_scalar_prefetch=2, grid=(nt,), in_specs=[],
            out_specs=pl.BlockSpec((N_HEADS, ncp, TQ), lambda i, *_: (0, 0, i))),
        compiler_params=_cparams(1), name="cmp_bias")(thr, rb)
    bd = pl.pallas_call(
        _diag_bias_kernel, out_shape=jax.ShapeDtypeStruct((N_HEADS, 3 * TQ, TQ), F32),
        grid_spec=pltpu.PrefetchScalarGridSpec(
            num_scalar_prefetch=2, grid=(1,), in_specs=[],
            out_specs=pl.BlockSpec((N_HEADS, 3 * TQ, TQ), lambda i, *_: (0, 0, 0))),
        compiler_params=_cparams(1), name="diag_bias")(thr, rb)
    c_start = jnp.arange(ncp) * D_STRIDE
    c_end = c_start + L_CMP - 1
    s_start = jnp.arange(ns) * L_SLC
    aggt = ((c_end[None, :] >= s_start[:, None]) & (c_start[None, :] <= s_start[:, None] + L_SLC - 1)
            & (jnp.arange(ncp)[None, :] < nc)).astype(BF16)
    return bias_c, bd, aggt


def _nsa(qt, kcmp, vcmpt, ks, vst, kw, vwt, ng, rel_bias, batch, seq):
    first_far = TQ + 1
    assert MAX_EXACT + int(math.log(first_far / MAX_EXACT) / math.log(MAX_DISTANCE / MAX_EXACT)
                           * (N_BUCKETS - MAX_EXACT)) >= N_BUCKETS - 1, "far tiles need one bucket"
    ncp = seq // D_STRIDE
    ns = seq // L_SLC
    nt = seq // TQ
    bias_c, bd, aggt = _attention_tables(rel_bias, seq)
    kspec = pl.BlockSpec((1, 1, seq, LANES), lambda b, g, i: (b, g, 0, 0))
    vtspec = pl.BlockSpec((1, 1, LANES, seq), lambda b, g, i: (b, g, 0, 0))
    state = lambda rows: pltpu.VMEM((HPG, rows, TQ), F32)
    grid_spec = pltpu.PrefetchScalarGridSpec(
        num_scalar_prefetch=0, grid=(batch, N_KV_HEADS, nt),
        in_specs=[pl.BlockSpec((1, HPG, LANES, TQ), lambda b, g, i: (b, g, 0, i)),
                  pl.BlockSpec((1, ncp, LANES), lambda b, g, i: (b, 0, g)),
                  pl.BlockSpec((1, 1, LANES, ncp), lambda b, g, i: (b, g, 0, 0)),
                  pl.BlockSpec((HPG, ncp, TQ), lambda b, g, i: (g, 0, i)),
                  kspec, vtspec, kspec, vtspec,
                  pl.BlockSpec((TQ, LANES), lambda b, g, i: (b * nt + i, g)),
                  pl.BlockSpec((ns, ncp), lambda b, g, i: (0, 0)),
                  pl.BlockSpec((HPG, 3 * TQ, TQ), lambda b, g, i: (g, 0, 0))],
        out_specs=pl.BlockSpec((TQ, HPG * HEAD_DIM), lambda b, g, i: (b * nt + i, g)),
        scratch_shapes=[pltpu.VMEM((ns, TQ), F32),
                        pltpu.VMEM((HPG, LANES, TQ), BF16),
                        state(HEAD_DIM),
                        state(1), state(PV_ROWS),
                        state(1), state(PV_ROWS)])
    return pl.pallas_call(
        functools.partial(_nsa_kernel, ns=ns),
        out_shape=jax.ShapeDtypeStruct((batch * seq, N_HEADS * HEAD_DIM), BF16),
        grid_spec=grid_spec, compiler_params=_cparams(3), name="nsa",
    )(qt, kcmp, vcmpt, bias_c, ks, vst, kw, vwt, ng, aggt, bd)


def _ffn_kernel(x_ref, ya_ref, o_ref, gb_ref, p_ref, wao_ref, wout_ref, nf_ref, wup_ref, cw_ref,
                cb_ref, wd_ref, np_ref, wpg_ref, wpe_ref, nfin_ref, out_ref, carry_ref,
                *, tm, tiles_per_seq, d_ff, chunks):
    r = pl.program_id(0)

    @pl.when((r % tiles_per_seq) == 0)
    def _():
        carry_ref[...] = jnp.zeros_like(carry_ref)

    y = ya_ref[...].astype(F32) + gb_ref[...].astype(F32) * _dot(o_ref[...], wao_ref[...])
    x1 = x_ref[...] + _dot(y.astype(BF16), wout_ref[...])
    hf = _rms(x1, nf_ref[...]).astype(BF16)

    def conv3(c0, width):
        up = _dot(hf, wup_ref[:, c0:c0 + width])
        rowi = lax.broadcasted_iota(jnp.int32, up.shape, 0)
        prev = carry_ref[:, c0:c0 + width]
        s1 = jnp.where(rowi == 0, prev[7:8, :], pltpu.roll(up, 1, 0))
        s2 = jnp.where(rowi == 0, prev[6:7, :],
                       jnp.where(rowi == 1, prev[7:8, :], pltpu.roll(up, 2, 0)))
        carry_ref[:, c0:c0 + width] = up[tm - 8:, :]
        return (cw_ref[0:1, c0:c0 + width] * s2 + cw_ref[1:2, c0:c0 + width] * s1
                + cw_ref[2:3, c0:c0 + width] * up + cb_ref[:, c0:c0 + width])

    acc = None
    for c0, width in chunks:
        act = (_gelu(conv3(c0, width)) * conv3(d_ff + c0, width)).astype(BF16)
        part = _dot(act, wd_ref[c0:c0 + width, :])
        acc = part if acc is None else acc + part

    x2 = x1 + acc
    pg = _sigmoid(_dot(_rms(x2, np_ref[...]).astype(BF16), wpg_ref[...]))
    pe = _dot(p_ref[...].astype(BF16), wpe_ref[...])
    out_ref[...] = _rms(x2 + pg * pe, nfin_ref[...])


def _ffn(x2, ya, o, mg, p2, w_attn_out, w_out, norm_ffn, w_up, ffn_dw_w, ffn_dw_b, w_down,
         norm_ple, w_ple_gate, w_ple, norm_final, seq, tm=512):
    t, d = x2.shape
    d_ff = w_down.shape[0]
    assert d_ff % MXU_DEPTH == 0
    chunks, c0 = [], 0
    while c0 < d_ff:
        width = min(FFN_CHUNK, d_ff - c0)
        chunks.append((c0, width))
        c0 += width
    tiles_per_seq = seq // tm
    cw = jnp.concatenate([ffn_dw_w, jnp.zeros((8 - FFN_CONV_WIDTH, 2 * d_ff), ffn_dw_w.dtype)], 0)
    vec = lambda v: v.reshape(1, -1)
    const = lambda a: pl.BlockSpec(a.shape, lambda r: (0, 0), pipeline_mode=pl.Buffered(1))
    rows = lambda n: pl.BlockSpec((tm, n), lambda r: (r, 0))
    kern = functools.partial(_ffn_kernel, tm=tm, tiles_per_seq=tiles_per_seq, d_ff=d_ff,
                             chunks=tuple(chunks))
    operands = (x2, ya, o, mg, p2, w_attn_out, w_out, vec(norm_ffn), w_up, cw, vec(ffn_dw_b),
                w_down, vec(norm_ple), w_ple_gate, w_ple, vec(norm_final))
    gb_spec = pl.BlockSpec((tm, d), lambda r: (r, 1))
    return pl.pallas_call(
        kern, out_shape=jax.ShapeDtypeStruct((t, d), F32), grid=(t // tm,),
        in_specs=[rows(d), rows(d), rows(o.shape[1]), gb_spec, rows(p2.shape[1])]
        + [const(a) for a in operands[5:]],
        out_specs=rows(d),
        scratch_shapes=[pltpu.VMEM((8, 2 * d_ff), F32)],
        compiler_params=_cparams(1), name="ffn",
    )(*operands)


def kernel(x, p, rel_bias, norm_mix, w_in, conv_dw_w, conv_dw_b, conv_ln_g, conv_ln_b, w_conv_out,
           cmp_pe_k, cmp_pe_v, w_ck1, w_ck2, w_cv1, w_cv2, w_attn_out, w_out, norm_ffn, w_up,
           ffn_dw_w, ffn_dw_b, w_down, norm_ple, w_ple_gate, w_ple, norm_final):
    batch, seq, d = x.shape
    depth = w_in.shape[0]
    x2 = x.reshape(batch * seq, d)
    for i in range(depth):
        a, qt, kc, vc, ks, vst, kw, vwt, ng, mg = _inproj(
            x2, norm_mix[i], _prep_inproj_weights(w_in[i], d), batch, seq)
        kcmp, vcmpt = _compress(kc, vc,
                               _prep_compress_weights(cmp_pe_k[i], w_ck1[i], w_ck2[i]),
                               _prep_compress_weights(cmp_pe_v[i], w_cv1[i], w_cv2[i]),
                               batch, seq)
        ya = _convmix(a, mg, conv_dw_w[i], conv_dw_b[i], conv_ln_g[i], conv_ln_b[i],
                      w_conv_out[i].astype(BF16), batch, seq)
        o = _nsa(qt, kcmp, vcmpt, ks, vst, kw, vwt, ng, rel_bias, batch, seq)
        assert i == depth - 1, "the final RMSNorm is fused into the (single) layer's MLP kernel"
        x2 = _ffn(x2, ya, o, mg, p[i].reshape(batch * seq, -1), w_attn_out[i].astype(BF16),
                  w_out[i].astype(BF16), norm_ffn[i], w_up[i].astype(BF16), ffn_dw_w[i],
                  ffn_dw_b[i], w_down[i].astype(BF16), norm_ple[i], w_ple_gate[i].astype(BF16),
                  w_ple[i].astype(BF16), norm_final, seq)
    return x2.reshape(batch, seq, d)
```

```python
import functools
import math

import jax
import jax.numpy as jnp
import numpy as np
from jax import lax
from jax.experimental import pallas as pl
from jax.experimental.pallas import tpu as pltpu

N_HEADS = 8
HEAD_DIM = 64
N_KV_HEADS = 2
HPG = N_HEADS // N_KV_HEADS
L_CMP = 32
D_STRIDE = 16
CMP_HIDDEN = 256
L_SLC = 64
N_SEL = 16
WINDOW = 512
N_BUCKETS = 32
MAX_EXACT = N_BUCKETS // 2
MAX_DISTANCE = 128
CONV_WIDTH = 31
FFN_CONV_WIDTH = 3
EPS = 1e-6
FORCE = 1e4

LANES = 128
MXU_DEPTH = 256
FFN_CHUNK = 6 * MXU_DEPTH
NEG = -1e30
M_INIT = -1e29
LOG2E = 1.4426950408889634
PIPE_DEPTH = 6
FAR_GROUP = 4
TQ = 256
PV_ROWS = HEAD_DIM + 16
SUBLANES = 8
CONV_HALO = 32
CONV_ROWS = 64
VMEM_LIMIT = 56 * 1024 * 1024

F32 = jnp.float32
BF16 = jnp.bfloat16


def _cparams(n_axes):
    return pltpu.CompilerParams(dimension_semantics=("arbitrary",) * n_axes,
                                vmem_limit_bytes=VMEM_LIMIT)


def _dot(a, b):
    return jnp.dot(a, b, preferred_element_type=F32)


def _dot_nt(a, b):
    return lax.dot_general(a, b, (((1,), (1,)), ((), ())), preferred_element_type=F32)


def _rms(xf, g):
    return xf * lax.rsqrt(jnp.mean(xf * xf, axis=-1, keepdims=True) + EPS) * g


def _sigmoid(x):
    return 1.0 / (1.0 + jnp.exp(-x))


def _gelu(x):
    return 0.5 * x * (1.0 + jnp.tanh(0.7978845608028654 * (x + 0.044715 * x * x * x)))


def _t5_bucket(dist):
    n = jnp.maximum(dist, 0)
    nf = jnp.maximum(n, MAX_EXACT).astype(F32)
    large = MAX_EXACT + (jnp.log(nf / MAX_EXACT) / math.log(MAX_DISTANCE / MAX_EXACT)
                         * (N_BUCKETS - MAX_EXACT)).astype(jnp.int32)
    large = jnp.minimum(large, N_BUCKETS - 1)
    return jnp.where(n < MAX_EXACT, n, large)


def _inproj_kernel(x_ref, g_ref, wu_ref, wq_ref, wkv_ref, wng_ref, wmg_ref,
                   a_ref, qt_ref, kc_ref, vc_ref, ks_ref, vst_ref, kw_ref, vwt_ref, ng_ref, mg_ref,
                   cs_ref, *, tm, tiles_per_seq, d_conv):
    r = pl.program_id(0)
    h = _rms(x_ref[...], g_ref[...]).astype(BF16)

    u = _dot(h, wu_ref[...])
    a_ref[...] = (u[:, :d_conv] * _sigmoid(u[:, d_conv:])).astype(BF16)

    zqt = jnp.transpose(_dot(h, wq_ref[...]))
    zero_rows = jnp.zeros((HEAD_DIM, tm), BF16)
    for hh in range(N_HEADS):
        qt_ref[0, hh, 0:HEAD_DIM, :] = zqt[hh * HEAD_DIM:(hh + 1) * HEAD_DIM].astype(BF16)
        qt_ref[0, hh, HEAD_DIM:, :] = zero_rows

    zkv = _dot(h, wkv_ref[...])
    for which, out_ref in enumerate((kc_ref, vc_ref)):
        cs_ref[which] = zkv[:, which * LANES:(which + 1) * LANES]
        for tok in range(D_STRIDE):
            rows = cs_ref[which, pl.ds(tok, tm // D_STRIDE, stride=D_STRIDE), :]
            out_ref[:, tok * LANES:(tok + 1) * LANES] = rows.astype(BF16)
    lane = lax.broadcasted_iota(jnp.int32, (tm, LANES), 1)
    row = lax.broadcasted_iota(jnp.int32, (tm, LANES), 0)
    spos = (r % tiles_per_seq) * tm + row
    lo = lane < HEAD_DIM
    blk_tag = jnp.where((lane - HEAD_DIM) == spos // L_SLC, -NEG, 0.0)
    ones_rows = jnp.where(lax.broadcasted_iota(jnp.int32, (HEAD_DIM, tm), 0) < 8, 1.0, 0.0)
    ones_rows = ones_rows.astype(BF16)
    for g in range(N_KV_HEADS):
        for k_ref, vt_ref, base, tag in ((ks_ref, vst_ref, (2 + g) * LANES, blk_tag),
                                         (kw_ref, vwt_ref, (4 + g) * LANES, 0.0)):
            pair = zkv[:, base:base + LANES]
            k_ref[0, g] = jnp.where(lo, pair, tag).astype(BF16)
            vt_ref[0, g, 0:HEAD_DIM, :] = jnp.transpose(pair)[HEAD_DIM:].astype(BF16)
            vt_ref[0, g, HEAD_DIM:, :] = ones_rows

    ng_ref[...] = _sigmoid(_dot(h, wng_ref[...]))
    mg_ref[...] = _sigmoid(_dot(h, wmg_ref[...])).astype(BF16)


def _prep_inproj_weights(w_in, d_model):
    d_conv = d_model // 2
    n_conv = 2 * d_conv
    n_q = N_HEADS * HEAD_DIM
    n_kv = 6 * N_KV_HEADS * HEAD_DIM
    n_ng = 3 * N_HEADS
    o = 0
    wu = w_in[:, o:o + n_conv]; o += n_conv
    wq = w_in[:, o:o + n_q]; o += n_q
    wkv = w_in[:, o:o + n_kv]; o += n_kv
    wng = w_in[:, o:o + n_ng]; o += n_ng
    wmg = w_in[:, o:]
    wq_p = wq * (LOG2E / math.sqrt(HEAD_DIM))
    kvcols = [wkv[:, 0:LANES], wkv[:, LANES:2 * LANES]]
    for k_kind in (2, 4):
        for g in range(N_KV_HEADS):
            for kind in (k_kind, k_kind + 1):
                c0 = kind * N_KV_HEADS * HEAD_DIM + g * HEAD_DIM
                kvcols.append(wkv[:, c0:c0 + HEAD_DIM])
    wkv_p = jnp.concatenate(kvcols, axis=1)
    per_g = HPG * 3
    ngcols = []
    for g in range(N_KV_HEADS):
        ngcols += [wng[:, g * per_g:(g + 1) * per_g],
                   jnp.zeros((d_model, LANES - per_g), w_in.dtype)]
    wng_p = jnp.concatenate(ngcols, axis=1)
    return tuple(w.astype(BF16) for w in (wu, wq_p, wkv_p, wng_p, wmg))


def _inproj(x2, norm_g, weights, batch, seq, tm=512):
    t, d = x2.shape
    wu, wq, wkv, wng, wmg = weights
    d_conv = d // 2
    tiles_per_seq = seq // tm
    n_tiles = t // tm
    full = lambda w: pl.BlockSpec(w.shape, lambda r: (0, 0))
    row = lambda n: pl.BlockSpec((tm, n), lambda r: (r, 0))
    headed = lambda nh: pl.BlockSpec((1, nh, tm, LANES),
                                     lambda r: (r // tiles_per_seq, 0, r % tiles_per_seq, 0))
    headed_t = lambda nh: pl.BlockSpec((1, nh, LANES, tm),
                                       lambda r: (r // tiles_per_seq, 0, 0, r % tiles_per_seq))
    k_shape = jax.ShapeDtypeStruct((batch, N_KV_HEADS, seq, LANES), BF16)
    vt_shape = jax.ShapeDtypeStruct((batch, N_KV_HEADS, LANES, seq), BF16)
    out_shape = (
        jax.ShapeDtypeStruct((t, d_conv), BF16),
        jax.ShapeDtypeStruct((batch, N_HEADS, LANES, seq), BF16),
        jax.ShapeDtypeStruct((t // D_STRIDE, D_STRIDE * LANES), BF16),
        jax.ShapeDtypeStruct((t // D_STRIDE, D_STRIDE * LANES), BF16),
        k_shape, vt_shape, k_shape, vt_shape,
        jax.ShapeDtypeStruct((t, N_KV_HEADS * LANES), F32),
        jax.ShapeDtypeStruct((t, 2 * d), BF16),
    )
    blocks = pl.BlockSpec((tm // D_STRIDE, D_STRIDE * LANES), lambda r: (r, 0))
    out_specs = (row(d_conv), headed_t(N_HEADS), blocks, blocks,
                 headed(N_KV_HEADS), headed_t(N_KV_HEADS), headed(N_KV_HEADS),
                 headed_t(N_KV_HEADS), row(N_KV_HEADS * LANES), row(2 * d))
    kern = functools.partial(_inproj_kernel, tm=tm, tiles_per_seq=tiles_per_seq, d_conv=d_conv)
    return pl.pallas_call(
        kern, out_shape=out_shape, grid=(n_tiles,),
        in_specs=[row(d), pl.BlockSpec((1, d), lambda r: (0, 0)),
                  full(wu), full(wq), full(wkv), full(wng), full(wmg)],
        out_specs=out_specs, scratch_shapes=[pltpu.VMEM((2, tm, LANES), F32)],
        compiler_params=_cparams(1), name="inproj",
    )(x2, norm_g.reshape(1, d), wu, wq, wkv, wng, wmg)


def _compress_kernel(rk_ref, rv_ref, pek_ref, pev_ref, w1k_ref, w1v_ref, w2k_ref, w2v_ref,
                     kcmp_ref, vcmp_ref, *, ncp):
    def one(r_ref, pe_ref, w1_ref, w2_ref):
        r = r_ref[0]
        top = _dot(r, w1_ref[0])
        bot = _dot(r, w1_ref[1])
        pe_h = _dot(pe_ref[0], w1_ref[0]) + _dot(pe_ref[1], w1_ref[1])
        nxt = pltpu.roll(bot, ncp - 1, 0)
        rowi = lax.broadcasted_iota(jnp.int32, top.shape, 0)
        hid = top + jnp.where(rowi == ncp - 1, 0.0, nxt) + pe_h[0:1, :]
        return _dot(_gelu(hid).astype(BF16), w2_ref[...])

    kcmp_ref[0] = one(rk_ref, pek_ref, w1k_ref, w2k_ref).astype(BF16)
    v = one(rv_ref, pev_ref, w1v_ref, w2v_ref)
    rowt = lax.broadcasted_iota(jnp.int32, (LANES, ncp), 0)
    ones_rows = (rowt >= HEAD_DIM) & (rowt < HEAD_DIM + 8)
    for g in range(N_KV_HEADS):
        vt = jnp.transpose(v[:, g * LANES:(g + 1) * LANES])
        vcmp_ref[0, g] = jnp.where(ones_rows, 1.0, vt).astype(BF16)


def _prep_compress_weights(pe, w1, w2):
    half = L_CMP // 2
    w1r = w1.reshape(L_CMP, HEAD_DIM, CMP_HIDDEN).astype(BF16)
    w2p = jnp.pad(w2, ((0, 0), (0, LANES - HEAD_DIM))).astype(BF16)

    def per_group(w, axis):
        z = jnp.zeros_like(w)
        return jnp.stack([jnp.concatenate([w if k == g else z for k in range(N_KV_HEADS)], axis=-1)
                          for g in range(N_KV_HEADS)], axis=axis)

    w1big = per_group(w1r, 1).reshape(2, half * N_KV_HEADS * HEAD_DIM, N_KV_HEADS * CMP_HIDDEN)
    w2big = per_group(w2p, 0).reshape(N_KV_HEADS * CMP_HIDDEN, N_KV_HEADS * LANES)
    per = pe.reshape(2, half, 1, HEAD_DIM)
    pebig = jnp.broadcast_to(per, (2, half, N_KV_HEADS, HEAD_DIM)).reshape(2, 1, -1)
    pebig = jnp.broadcast_to(pebig, (2, 8, pebig.shape[-1])).astype(BF16)
    return pebig, w1big, w2big


def _compress(kc, vc, wk, wv, batch, seq):
    ncp = seq // D_STRIDE
    width = D_STRIDE * LANES
    rk = kc.reshape(batch, ncp, width)
    rv = vc.reshape(batch, ncp, width)
    pek, w1k, w2k = wk
    pev, w1v, w2v = wv
    c3 = lambda a: pl.BlockSpec(a.shape, lambda b: (0, 0, 0))
    c2 = lambda a: pl.BlockSpec(a.shape, lambda b: (0, 0))
    rspec = pl.BlockSpec((1, ncp, width), lambda b: (b, 0, 0))
    kspec = pl.BlockSpec((1, ncp, N_KV_HEADS * LANES), lambda b: (b, 0, 0))
    kshape = jax.ShapeDtypeStruct((batch, ncp, N_KV_HEADS * LANES), BF16)
    vspec = pl.BlockSpec((1, N_KV_HEADS, LANES, ncp), lambda b: (b, 0, 0, 0))
    vshape = jax.ShapeDtypeStruct((batch, N_KV_HEADS, LANES, ncp), BF16)
    return pl.pallas_call(
        functools.partial(_compress_kernel, ncp=ncp), out_shape=(kshape, vshape), grid=(batch,),
        in_specs=[rspec, rspec, c3(pek), c3(pev), c3(w1k), c3(w1v), c2(w2k), c2(w2v)],
        out_specs=(kspec, vspec), compiler_params=_cparams(1), name="compress",
    )(rk, rv, pek, pev, w1k, w1v, w2k, w2v)


def _convmix_kernel(a_ref, halo_ref, ga_ref, cw_ref, cb_ref, lg_ref, lb_ref, wo_ref,
                    ya_ref, ext_ref, sh_ref, y_ref, *, tm):
    i = pl.program_id(1)
    halo = halo_ref[0].astype(F32)
    ext_ref[0:CONV_HALO, :] = jnp.where(i == 0, 0.0, halo)
    ext_ref[CONV_HALO:, :] = a_ref[0].astype(F32)
    span = sh_ref.shape[1]
    for ph in range(1, SUBLANES):
        sh_ref[ph - 1] = ext_ref[ph:ph + span, :]
    off = CONV_HALO - (CONV_WIDTH - 1)

    def rows_chunk(cidx, carry):
        r0 = pl.multiple_of(cidx * CONV_ROWS, CONV_ROWS)
        acc = jnp.zeros((CONV_ROWS, a_ref.shape[-1]), F32) + cb_ref[...]
        for j in range(CONV_WIDTH):
            ph = (off + j) % SUBLANES
            base = off + j - ph
            if ph == 0:
                xs = ext_ref[pl.ds(r0 + base, CONV_ROWS), :]
            else:
                xs = sh_ref[ph - 1, pl.ds(r0 + base, CONV_ROWS), :]
            acc = acc + jnp.tile(cw_ref[j], (CONV_ROWS // SUBLANES, 1)) * xs
        y_ref[pl.ds(r0, CONV_ROWS), :] = acc
        return carry

    lax.fori_loop(0, tm // CONV_ROWS, rows_chunk, 0)
    acc = y_ref[...]
    mu = jnp.mean(acc, axis=-1, keepdims=True)
    cen = acc - mu
    var = jnp.mean(cen * cen, axis=-1, keepdims=True)
    y = cen * lax.rsqrt(var + EPS) * lg_ref[...] + lb_ref[...]
    y = y * _sigmoid(y)
    yc = _dot(y.astype(BF16), wo_ref[...])
    ya_ref[...] = (ga_ref[...].astype(F32) * yc).astype(BF16)


def _convmix(a, mg, conv_w, conv_b, ln_g, ln_b, w_conv_out, batch, seq, tm=512):
    t, d_conv = a.shape
    d = w_conv_out.shape[1]
    a3 = a.reshape(batch, seq, d_conv)
    nt = seq // tm
    hb = tm // CONV_HALO
    cw = jnp.broadcast_to(conv_w[:, None, :], (CONV_WIDTH, SUBLANES, d_conv))
    vec = lambda v: v.reshape(1, -1)
    c2 = lambda a_: pl.BlockSpec(a_.shape, lambda b, i: (0,) * a_.ndim)
    return pl.pallas_call(
        functools.partial(_convmix_kernel, tm=tm),
        out_shape=jax.ShapeDtypeStruct((t, d), BF16), grid=(batch, nt),
        in_specs=[pl.BlockSpec((1, tm, d_conv), lambda b, i: (b, i, 0)),
                  pl.BlockSpec((1, CONV_HALO, d_conv),
                               lambda b, i: (b, jnp.maximum(i * hb - 1, 0), 0)),
                  pl.BlockSpec((tm, d), lambda b, i: (b * nt + i, 0)),
                  c2(cw), c2(vec(conv_b)), c2(vec(ln_g)), c2(vec(ln_b)), c2(w_conv_out)],
        out_specs=pl.BlockSpec((tm, d), lambda b, i: (b * nt + i, 0)),
        scratch_shapes=[pltpu.VMEM((tm + CONV_HALO, d_conv), F32),
                        pltpu.VMEM((SUBLANES - 1, tm + CONV_HALO - SUBLANES, d_conv), F32),
                        pltpu.VMEM((tm, d_conv), F32)],
        compiler_params=_cparams(2), name="convmix",
    )(a3, a3, mg, cw, vec(conv_b), vec(ln_g), vec(ln_b), w_conv_out)


def _nsa_kernel(qt_ref, kcmp_ref, vcmpt_ref, biasc_ref, ks_ref, vst_ref, kw_ref, vwt_ref,
                ng_ref, aggt_ref, bd_ref, o_ref,
                imp_ref, qa_ref, oc_ref, ms_ref, accs_ref, mw_ref, accw_ref, *, ns):
    i = pl.program_id(2)

    def k_tile(ref, j):
        return ref[0, 0, pl.ds(pl.multiple_of(j * TQ, TQ), TQ), :]

    def vt_tile(ref, j):
        return ref[0, 0, 0:PV_ROWS, pl.ds(pl.multiple_of(j * TQ, TQ), TQ)]

    def pipelined(tasks, scores, update):
        pending = [scores(t) for t in tasks[:PIPE_DEPTH]]
        for t, task in enumerate(tasks):
            update(task, pending.pop(0))
            if t + PIPE_DEPTH < len(tasks):
                pending.append(scores(tasks[t + PIPE_DEPTH]))

    def run(tasks):
        def scores(task):
            hh, kt, _, qmat, _, _, row0 = task
            s = _dot(kt, qmat)
            if row0 is None:
                return s
            if not isinstance(row0, int):
                row0 = pl.multiple_of(row0, TQ)
            return s + bd_ref[hh, pl.ds(row0, TQ), :]

        def update(task, s):
            hh, _, vt, _, m_ref, acc_ref, _ = task
            m_old = m_ref[hh]
            m_new = jnp.maximum(m_old, jnp.max(s, axis=0, keepdims=True))
            p = jnp.exp2(s - m_new).astype(BF16)
            acc_ref[hh] = jnp.exp2(m_old - m_new) * acc_ref[hh] + _dot(vt, p)
            m_ref[hh] = m_new

        pipelined(tasks, scores, update)

    def slc_tasks(j, row0=None):
        kt, vt = k_tile(ks_ref, j), vt_tile(vst_ref, j)
        return [(hh, kt, vt, qa_ref[hh], ms_ref, accs_ref, row0) for hh in range(HPG)]

    def win_tasks(j, row0):
        kt, vt = k_tile(kw_ref, j), vt_tile(vwt_ref, j)
        return [(hh, kt, vt, qt_ref[0, hh], mw_ref, accw_ref, row0) for hh in range(HPG)]

    for hh in range(HPG):
        ms_ref[hh] = jnp.full((1, TQ), M_INIT, F32)
        mw_ref[hh] = jnp.full((1, TQ), M_INIT, F32)
        accs_ref[hh] = jnp.zeros((PV_ROWS, TQ), F32)
        accw_ref[hh] = jnp.zeros((PV_ROWS, TQ), F32)

    p_parts = []

    def cmp_scores(hh):
        return _dot(kcmp_ref[0], qt_ref[0, hh]) + biasc_ref[hh]

    def cmp_update(hh, s_c):
        m_c = jnp.maximum(jnp.max(s_c, axis=0, keepdims=True), M_INIT)
        p_c = jnp.exp2(s_c - m_c)
        l_c = jnp.sum(p_c, axis=0, keepdims=True)
        p_c = p_c * jnp.where(l_c > 0.0, 1.0 / l_c, 0.0)
        oc_ref[hh] = _dot(vcmpt_ref[0, 0, 0:HEAD_DIM, :], p_c.astype(BF16))
        p_parts.append(p_c)

    pipelined(list(range(HPG)), cmp_scores, cmp_update)

    p_sum = (p_parts[0] + p_parts[1]) + (p_parts[2] + p_parts[3])
    p_hi = p_sum.astype(BF16)
    p_lo = (p_sum - p_hi.astype(F32)).astype(BF16)
    imp = _dot(aggt_ref[...], p_hi) + _dot(aggt_ref[...], p_lo)
    blk = lax.broadcasted_iota(jnp.int32, (ns, TQ), 0)
    cur = (i * TQ + lax.broadcasted_iota(jnp.int32, (ns, TQ), 1)) // L_SLC
    forced = (blk == 0) | (blk == cur) | (blk == cur - 1)
    imp = jnp.where(forced, FORCE, jnp.where(blk <= cur, imp, -FORCE))
    imp_ref[...] = imp

    rows2 = jnp.where(i >= 2, 0, 3 * TQ)
    rows1 = jnp.where(i >= 1, TQ, 3 * TQ)
    run(win_tasks(jnp.maximum(i - 2, 0), rows2) + win_tasks(jnp.maximum(i - 1, 0), rows1)
        + win_tasks(i, 2 * TQ))

    n_grp = ns // 8
    grp = [imp[8 * r:8 * r + 8, :] for r in range(n_grp)]
    blk8 = lax.broadcasted_iota(jnp.int32, (8, TQ), 0)
    rank = [jnp.zeros((8, TQ), F32) for _ in range(n_grp)]
    for k in range(ns):
        rowk = imp_ref[k:k + 1, :]
        kg = k // 8
        for r in range(n_grp):
            if r < kg:
                beats = jnp.where(rowk > grp[r], 1.0, 0.0)
            elif r > kg:
                beats = jnp.where(rowk >= grp[r], 1.0, 0.0)
            else:
                beats = jnp.where(blk8 > k - 8 * kg, jnp.where(rowk >= grp[r], 1.0, 0.0),
                                  jnp.where(rowk > grp[r], 1.0, 0.0))
            rank[r] = rank[r] + beats
    rank = jnp.concatenate(rank, axis=0)
    unsel =jnp.where(rank < float(N_SEL), 0.0, -1.0)
    flags = unsel.astype(BF16)
    if ns < HEAD_DIM:
        flags = jnp.concatenate([flags, jnp.zeros((HEAD_DIM - ns, TQ), BF16)], axis=0)
    for hh in range(HPG):
        qa_ref[hh, 0:HEAD_DIM, :] = qt_ref[0, hh, 0:HEAD_DIM, :]
        qa_ref[hh, HEAD_DIM:, :] = flags

    n_far = jnp.maximum(i - 1, 0)

    def far_group(jj, carry):
        tasks = []
        for u in range(FAR_GROUP):
            tasks += slc_tasks(FAR_GROUP * jj + u)
        run(tasks)
        return carry

    lax.fori_loop(0, n_far // FAR_GROUP, far_group, 0)
    for rem in range(FAR_GROUP):
        @pl.when(n_far % FAR_GROUP == rem)
        def _(rem=rem):
            tasks = []
            for u in range(rem):
                tasks += slc_tasks(n_far - rem + u)
            run(tasks + slc_tasks(jnp.maximum(i - 1, 0), rows1) + slc_tasks(i, 2 * TQ))

    gates_t = jnp.transpose(ng_ref[...])
    outs = []
    for hh in range(HPG):
        a_s, a_w = accs_ref[hh], accw_ref[hh]
        o_s = a_s[0:HEAD_DIM] * (1.0 / a_s[HEAD_DIM:HEAD_DIM + 1])
        o_w = a_w[0:HEAD_DIM] * (1.0 / a_w[HEAD_DIM:HEAD_DIM + 1])
        outs.append(gates_t[3 * hh:3 * hh + 1] * oc_ref[hh]
                    + gates_t[3 * hh + 1:3 * hh + 2] * o_s
                    + gates_t[3 * hh + 2:3 * hh + 3] * o_w)
    o_ref[...] = jnp.transpose(jnp.concatenate(outs, axis=0)).astype(BF16)


def _bias_of(thr_ref, rb_ref, dist, h):
    b = jnp.full(dist.shape, rb_ref[h], F32)
    for k in range(1, N_BUCKETS):
        b = jnp.where(dist >= thr_ref[k], rb_ref[k * N_HEADS + h], b)
    return b


def _cmp_bias_kernel(thr_ref, rb_ref, biasc_ref, *, nc, ncp):
    i = pl.program_id(0)
    per_tile = TQ // D_STRIDE
    band = 2 * per_tile
    assert TQ + D_STRIDE - (L_CMP - 1) >= MAX_DISTANCE

    def dist_of(c0, rows):
        c = c0 + lax.broadcasted_iota(jnp.int32, (rows, TQ), 0)
        r = lax.broadcasted_iota(jnp.int32, (rows, TQ), 1)
        dist = i * TQ + r - (c * D_STRIDE + L_CMP - 1)
        return dist, (dist >= 0) & (c < nc)

    _, ok_all = dist_of(0, ncp)
    band0 = pl.multiple_of(jnp.maximum(i * per_tile - per_tile, 0), per_tile)
    dist_b, ok_b = dist_of(band0, band)
    for h in range(N_HEADS):
        biasc_ref[h] = jnp.where(ok_all, rb_ref[(N_BUCKETS - 1) * N_HEADS + h], NEG)
        biasc_ref[h, pl.ds(band0, band), :] = jnp.where(
            ok_b, _bias_of(thr_ref, rb_ref, dist_b, h), NEG)


def _diag_bias_kernel(thr_ref, rb_ref, bd_ref):
    d0 = (lax.broadcasted_iota(jnp.int32, (TQ, TQ), 1)
          - lax.broadcasted_iota(jnp.int32, (TQ, TQ), 0))
    for h in range(N_HEADS):
        far = rb_ref[(N_BUCKETS - 1) * N_HEADS + h]
        bd_ref[h, 0:TQ, :] = jnp.where(d0 < 0, _bias_of(thr_ref, rb_ref, d0 + 2 * TQ, h) - far, NEG)
        bd_ref[h, TQ:2 * TQ, :] = _bias_of(thr_ref, rb_ref, d0 + TQ, h) - far
        bd_ref[h, 2 * TQ:3 * TQ, :] = jnp.where(d0 >= 0, _bias_of(thr_ref, rb_ref, d0, h) - far, NEG)
        bd_ref[h, 3 * TQ:, :] = jnp.full((TQ, TQ), NEG, F32)


def _attention_tables(rel_bias, seq):
    ncp = seq // D_STRIDE
    nc = (seq - L_CMP) // D_STRIDE + 1
    ns = seq // L_SLC
    nt = seq // TQ
    n_probe = 2 * MAX_DISTANCE
    buckets = _t5_bucket(jnp.arange(n_probe))
    thr = jnp.sum(buckets[None, :] < jnp.arange(N_BUCKETS)[:, None], axis=1).astype(jnp.int32)
    assert WINDOW == 2 * TQ
    rb = (rel_bias.astype(F32) * LOG2E).reshape(-1)
    bias_c = pl.pallas_call(
        functools.partial(_cmp_bias_kernel, nc=nc, ncp=ncp),
        out_shape=jax.ShapeDtypeStruct((N_HEADS, ncp, seq), F32),
        grid_spec=pltpu.PrefetchScalarGridSpec(
            num_scalar_prefetch=2, grid=(nt,), in_specs=[],
            out_specs=pl.BlockSpec((N_HEADS, ncp, TQ), lambda i, *_: (0, 0, i))),
        compiler_params=_cparams(1), name="cmp_bias")(thr, rb)
    bd = pl.pallas_call(
        _diag_bias_kernel, out_shape=jax.ShapeDtypeStruct((N_HEADS, 4 * TQ, TQ), F32),
        grid_spec=pltpu.PrefetchScalarGridSpec(
            num_scalar_prefetch=2, grid=(1,), in_specs=[],
            out_specs=pl.BlockSpec((N_HEADS, 4 * TQ, TQ), lambda i, *_: (0, 0, 0))),
        compiler_params=_cparams(1), name="diag_bias")(thr, rb)
    c_start = jnp.arange(ncp) * D_STRIDE
    c_end = c_start + L_CMP - 1
    s_start = jnp.arange(ns) * L_SLC
    aggt = ((c_end[None, :] >= s_start[:, None]) & (c_start[None, :] <= s_start[:, None] + L_SLC - 1)
            & (jnp.arange(ncp)[None, :] < nc)).astype(BF16)
    return bias_c, bd, aggt


def _nsa(qt, kcmp, vcmpt, ks, vst, kw, vwt, ng, rel_bias, batch, seq):
    first_far = TQ + 1
    assert MAX_EXACT + int(math.log(first_far / MAX_EXACT) / math.log(MAX_DISTANCE / MAX_EXACT)
                           * (N_BUCKETS - MAX_EXACT)) >= N_BUCKETS - 1, "far tiles need one bucket"
    ncp = seq // D_STRIDE
    ns = seq // L_SLC
    nt = seq // TQ
    bias_c, bd, aggt = _attention_tables(rel_bias, seq)
    kspec = pl.BlockSpec((1, 1, seq, LANES), lambda b, g, i: (b, g, 0, 0))
    vtspec = pl.BlockSpec((1, 1, LANES, seq), lambda b, g, i: (b, g, 0, 0))
    state = lambda rows: pltpu.VMEM((HPG, rows, TQ), F32)
    grid_spec = pltpu.PrefetchScalarGridSpec(
        num_scalar_prefetch=0, grid=(batch, N_KV_HEADS, nt),
        in_specs=[pl.BlockSpec((1, HPG, LANES, TQ), lambda b, g, i: (b, g, 0, i)),
                  pl.BlockSpec((1, ncp, LANES), lambda b, g, i: (b, 0, g)),
                  pl.BlockSpec((1, 1, LANES, ncp), lambda b, g, i: (b, g, 0, 0)),
                  pl.BlockSpec((HPG, ncp, TQ), lambda b, g, i: (g, 0, i)),
                  kspec, vtspec, kspec, vtspec,
                  pl.BlockSpec((TQ, LANES), lambda b, g, i: (b * nt + i, g)),
                  pl.BlockSpec((ns, ncp), lambda b, g, i: (0, 0)),
                  pl.BlockSpec((HPG, 4 * TQ, TQ), lambda b, g, i: (g, 0, 0))],
        out_specs=pl.BlockSpec((TQ, HPG * HEAD_DIM), lambda b, g, i: (b * nt + i, g)),
        scratch_shapes=[pltpu.VMEM((ns, TQ), F32),
                        pltpu.VMEM((HPG, LANES, TQ), BF16),
                        state(HEAD_DIM),
                        state(1), state(PV_ROWS),
                        state(1), state(PV_ROWS)])
    return pl.pallas_call(
        functools.partial(_nsa_kernel, ns=ns),
        out_shape=jax.ShapeDtypeStruct((batch * seq, N_HEADS * HEAD_DIM), BF16),
        grid_spec=grid_spec, compiler_params=_cparams(3), name="nsa",
    )(qt, kcmp, vcmpt, bias_c, ks, vst, kw, vwt, ng, aggt, bd)


def _ffn_kernel(x_ref, ya_ref, o_ref, gb_ref, p_ref, wao_ref, wout_ref, nf_ref, wup_ref, cw_ref,
                cb_ref, wd_ref, np_ref, wpg_ref, wpe_ref, nfin_ref, out_ref, carry_ref,
                *, tm, tiles_per_seq, d_ff, chunks):
    r = pl.program_id(0)

    @pl.when((r % tiles_per_seq) == 0)
    def _():
        carry_ref[...] = jnp.zeros_like(carry_ref)

    y = ya_ref[...].astype(F32) + gb_ref[...].astype(F32) * _dot(o_ref[...], wao_ref[...])
    x1 = x_ref[...] + _dot(y.astype(BF16), wout_ref[...])
    hf = _rms(x1, nf_ref[...]).astype(BF16)

    def conv3(c0, width):
        up = _dot(hf, wup_ref[:, c0:c0 + width])
        rowi = lax.broadcasted_iota(jnp.int32, up.shape, 0)
        prev = carry_ref[:, c0:c0 + width]
        s1 = jnp.where(rowi == 0, prev[7:8, :], pltpu.roll(up, 1, 0))
        s2 = jnp.where(rowi == 0, prev[6:7, :],
                       jnp.where(rowi == 1, prev[7:8, :], pltpu.roll(up, 2, 0)))
        carry_ref[:, c0:c0 + width] = up[tm - 8:, :]
        return (cw_ref[0:1, c0:c0 + width] * s2 + cw_ref[1:2, c0:c0 + width] * s1
                + cw_ref[2:3, c0:c0 + width] * up + cb_ref[:, c0:c0 + width])

    acc = None
    for c0, width in chunks:
        act = (_gelu(conv3(c0, width)) * conv3(d_ff + c0, width)).astype(BF16)
        part = _dot(act, wd_ref[c0:c0 + width, :])
        acc = part if acc is None else acc + part

    x2 = x1 + acc
    pg = _sigmoid(_dot(_rms(x2, np_ref[...]).astype(BF16), wpg_ref[...]))
    pe = _dot(p_ref[...].astype(BF16), wpe_ref[...])
    out_ref[...] = _rms(x2 + pg * pe, nfin_ref[...])


def _ffn(x2, ya, o, mg, p2, w_attn_out, w_out, norm_ffn, w_up, ffn_dw_w, ffn_dw_b, w_down,
         norm_ple, w_ple_gate, w_ple, norm_final, seq, tm=512):
    t, d = x2.shape
    d_ff = w_down.shape[0]
    assert d_ff % MXU_DEPTH == 0
    chunks, c0 = [], 0
    while c0 < d_ff:
        width = min(FFN_CHUNK, d_ff - c0)
        chunks.append((c0, width))
        c0 += width
    tiles_per_seq = seq // tm
    cw = jnp.concatenate([ffn_dw_w, jnp.zeros((8 - FFN_CONV_WIDTH, 2 * d_ff), ffn_dw_w.dtype)], 0)
    vec = lambda v: v.reshape(1, -1)
    const = lambda a: pl.BlockSpec(a.shape, lambda r: (0, 0), pipeline_mode=pl.Buffered(1))
    rows = lambda n: pl.BlockSpec((tm, n), lambda r: (r, 0))
    kern = functools.partial(_ffn_kernel, tm=tm, tiles_per_seq=tiles_per_seq, d_ff=d_ff,
                             chunks=tuple(chunks))
    operands = (x2, ya, o, mg, p2, w_attn_out, w_out, vec(norm_ffn), w_up, cw, vec(ffn_dw_b),
                w_down, vec(norm_ple), w_ple_gate, w_ple, vec(norm_final))
    gb_spec = pl.BlockSpec((tm, d), lambda r: (r, 1))
    return pl.pallas_call(
        kern, out_shape=jax.ShapeDtypeStruct((t, d), F32), grid=(t // tm,),
        in_specs=[rows(d), rows(d), rows(o.shape[1]), gb_spec, rows(p2.shape[1])]
        + [const(a) for a in operands[5:]],
        out_specs=rows(d),
        scratch_shapes=[pltpu.VMEM((8, 2 * d_ff), F32)],
        compiler_params=_cparams(1), name="ffn",
    )(*operands)


def kernel(x, p, rel_bias, norm_mix, w_in, conv_dw_w, conv_dw_b, conv_ln_g, conv_ln_b, w_conv_out,
           cmp_pe_k, cmp_pe_v, w_ck1, w_ck2, w_cv1, w_cv2, w_attn_out, w_out, norm_ffn, w_up,
           ffn_dw_w, ffn_dw_b, w_down, norm_ple, w_ple_gate, w_ple, norm_final):
    batch, seq, d = x.shape
    depth = w_in.shape[0]
    x2 = x.reshape(batch * seq, d)
    for i in range(depth):
        a, qt, kc, vc, ks, vst, kw, vwt, ng, mg = _inproj(
            x2, norm_mix[i], _prep_inproj_weights(w_in[i], d), batch, seq)
        kcmp, vcmpt = _compress(kc, vc,
                               _prep_compress_weights(cmp_pe_k[i], w_ck1[i], w_ck2[i]),
                               _prep_compress_weights(cmp_pe_v[i], w_cv1[i], w_cv2[i]),
                               batch, seq)
        ya = _convmix(a, mg, conv_dw_w[i], conv_dw_b[i], conv_ln_g[i], conv_ln_b[i],
                      w_conv_out[i].astype(BF16), batch, seq)
        o = _nsa(qt, kcmp, vcmpt, ks, vst, kw, vwt, ng, rel_bias, batch, seq)
        assert i == depth - 1, "the final RMSNorm is fused into the (single) layer's MLP kernel"
        x2 = _ffn(x2, ya, o, mg, p[i].reshape(batch * seq, -1), w_attn_out[i].astype(BF16),
                  w_out[i].astype(BF16), norm_ffn[i], w_up[i].astype(BF16), ffn_dw_w[i],
                  ffn_dw_b[i], w_down[i].astype(BF16), norm_ple[i], w_ple_gate[i].astype(BF16),
                  w_ple[i].astype(BF16), norm_final, seq)
    return x2.reshape(batch, seq, d)
```

```python
import functools
import math

import jax
import jax.numpy as jnp
import numpy as np
from jax import lax
from jax.experimental import pallas as pl
from jax.experimental.pallas import tpu as pltpu

N_HEADS = 8
HEAD_DIM = 64
N_KV_HEADS = 2
HPG = N_HEADS // N_KV_HEADS
L_CMP = 32
D_STRIDE = 16
CMP_HIDDEN = 256
L_SLC = 64
N_SEL = 16
WINDOW = 512
N_BUCKETS = 32
MAX_EXACT = N_BUCKETS // 2
MAX_DISTANCE = 128
CONV_WIDTH = 31
FFN_CONV_WIDTH = 3
EPS = 1e-6
FORCE = 1e4

LANES = 128
MXU_DEPTH = 256
FFN_CHUNK = 6 * MXU_DEPTH
NEG = -1e30
M_INIT = -1e29
LOG2E = 1.4426950408889634
PIPE_DEPTH = 6
FAR_GROUP = 4
TQ = 256
PV_ROWS = HEAD_DIM + 16
SUBLANES = 8
CONV_HALO = 32
CONV_ROWS = 64
VMEM_LIMIT = 56 * 1024 * 1024

F32 = jnp.float32
BF16 = jnp.bfloat16


def _cparams(n_axes):
    return pltpu.CompilerParams(dimension_semantics=("arbitrary",) * n_axes,
                                vmem_limit_bytes=VMEM_LIMIT)


def _dot(a, b):
    return jnp.dot(a, b, preferred_element_type=F32)


def _dot_nt(a, b):
    return lax.dot_general(a, b, (((1,), (1,)), ((), ())), preferred_element_type=F32)


def _rms(xf, g):
    return xf * lax.rsqrt(jnp.mean(xf * xf, axis=-1, keepdims=True) + EPS) * g


def _sigmoid(x):
    return 1.0 / (1.0 + jnp.exp(-x))


def _gelu(x):
    return 0.5 * x * (1.0 + jnp.tanh(0.7978845608028654 * (x + 0.044715 * x * x * x)))


def _t5_bucket(dist):
    n = jnp.maximum(dist, 0)
    nf = jnp.maximum(n, MAX_EXACT).astype(F32)
    large = MAX_EXACT + (jnp.log(nf / MAX_EXACT) / math.log(MAX_DISTANCE / MAX_EXACT)
                         * (N_BUCKETS - MAX_EXACT)).astype(jnp.int32)
    large = jnp.minimum(large, N_BUCKETS - 1)
    return jnp.where(n < MAX_EXACT, n, large)


def _inproj_kernel(x_ref, g_ref, wu_ref, wq_ref, wkv_ref, wng_ref, wmg_ref,
                   a_ref, qt_ref, kc_ref, vc_ref, ks_ref, vst_ref, kw_ref, vwt_ref, ng_ref, mg_ref,
                   cs_ref, *, tm, tiles_per_seq, d_conv):
    r = pl.program_id(0)
    h = _rms(x_ref[...], g_ref[...]).astype(BF16)

    u = _dot(h, wu_ref[...])
    a_ref[...] = (u[:, :d_conv] * _sigmoid(u[:, d_conv:])).astype(BF16)

    zqt = jnp.transpose(_dot(h, wq_ref[...]))
    zero_rows = jnp.zeros((HEAD_DIM, tm), BF16)
    for hh in range(N_HEADS):
        qt_ref[0, hh, 0:HEAD_DIM, :] = zqt[hh * HEAD_DIM:(hh + 1) * HEAD_DIM].astype(BF16)
        qt_ref[0, hh, HEAD_DIM:, :] = zero_rows

    zkv = _dot(h, wkv_ref[...])
    for which, out_ref in enumerate((kc_ref, vc_ref)):
        cs_ref[which] = zkv[:, which * LANES:(which + 1) * LANES]
        for tok in range(D_STRIDE):
            rows = cs_ref[which, pl.ds(tok, tm // D_STRIDE, stride=D_STRIDE), :]
            out_ref[:, tok * LANES:(tok + 1) * LANES] = rows.astype(BF16)
    lane = lax.broadcasted_iota(jnp.int32, (tm, LANES), 1)
    row = lax.broadcasted_iota(jnp.int32, (tm, LANES), 0)
    spos = (r % tiles_per_seq) * tm + row
    lo = lane < HEAD_DIM
    blk_tag = jnp.where((lane - HEAD_DIM) == spos // L_SLC, -NEG, 0.0)
    ones_rows = jnp.where(lax.broadcasted_iota(jnp.int32, (HEAD_DIM, tm), 0) < 8, 1.0, 0.0)
    ones_rows = ones_rows.astype(BF16)
    for g in range(N_KV_HEADS):
        for k_ref, vt_ref, base, tag in ((ks_ref, vst_ref, (2 + g) * LANES, blk_tag),
                                         (kw_ref, vwt_ref, (4 + g) * LANES, 0.0)):
            pair = zkv[:, base:base + LANES]
            k_ref[0, g] = jnp.where(lo, pair, tag).astype(BF16)
            vt_ref[0, g, 0:HEAD_DIM, :] = jnp.transpose(pair)[HEAD_DIM:].astype(BF16)
            vt_ref[0, g, HEAD_DIM:, :] = ones_rows

    ng_ref[...] = _sigmoid(_dot(h, wng_ref[...]))
    mg_ref[...] = _sigmoid(_dot(h, wmg_ref[...])).astype(BF16)


def _prep_inproj_weights(w_in, d_model):
    d_conv = d_model // 2
    n_conv = 2 * d_conv
    n_q = N_HEADS * HEAD_DIM
    n_kv = 6 * N_KV_HEADS * HEAD_DIM
    n_ng = 3 * N_HEADS
    o = 0
    wu = w_in[:, o:o + n_conv]; o += n_conv
    wq = w_in[:, o:o + n_q]; o += n_q
    wkv = w_in[:, o:o + n_kv]; o += n_kv
    wng = w_in[:, o:o + n_ng]; o += n_ng
    wmg = w_in[:, o:]
    wq_p = wq * (LOG2E / math.sqrt(HEAD_DIM))
    kvcols = [wkv[:, 0:LANES], wkv[:, LANES:2 * LANES]]
    for k_kind in (2, 4):
        for g in range(N_KV_HEADS):
            for kind in (k_kind, k_kind + 1):
                c0 = kind * N_KV_HEADS * HEAD_DIM + g * HEAD_DIM
                kvcols.append(wkv[:, c0:c0 + HEAD_DIM])
    wkv_p = jnp.concatenate(kvcols, axis=1)
    per_g = HPG * 3
    ngcols = []
    for g in range(N_KV_HEADS):
        ngcols += [wng[:, g * per_g:(g + 1) * per_g],
                   jnp.zeros((d_model, LANES - per_g), w_in.dtype)]
    wng_p = jnp.concatenate(ngcols, axis=1)
    return tuple(w.astype(BF16) for w in (wu, wq_p, wkv_p, wng_p, wmg))


def _inproj(x2, norm_g, weights, batch, seq, tm=512):
    t, d = x2.shape
    wu, wq, wkv, wng, wmg = weights
    d_conv = d // 2
    tiles_per_seq = seq // tm
    n_tiles = t // tm
    full = lambda w: pl.BlockSpec(w.shape, lambda r: (0, 0))
    row = lambda n: pl.BlockSpec((tm, n), lambda r: (r, 0))
    headed = lambda nh: pl.BlockSpec((1, nh, tm, LANES),
                                     lambda r: (r // tiles_per_seq, 0, r % tiles_per_seq, 0))
    headed_t = lambda nh: pl.BlockSpec((1, nh, LANES, tm),
                                       lambda r: (r // tiles_per_seq, 0, 0, r % tiles_per_seq))
    k_shape = jax.ShapeDtypeStruct((batch, N_KV_HEADS, seq, LANES), BF16)
    vt_shape = jax.ShapeDtypeStruct((batch, N_KV_HEADS, LANES, seq), BF16)
    out_shape = (
        jax.ShapeDtypeStruct((t, d_conv), BF16),
        jax.ShapeDtypeStruct((batch, N_HEADS, LANES, seq), BF16),
        jax.ShapeDtypeStruct((t // D_STRIDE, D_STRIDE * LANES), BF16),
        jax.ShapeDtypeStruct((t // D_STRIDE, D_STRIDE * LANES), BF16),
        k_shape, vt_shape, k_shape, vt_shape,
        jax.ShapeDtypeStruct((t, N_KV_HEADS * LANES), F32),
        jax.ShapeDtypeStruct((t, 2 * d), BF16),
    )
    blocks = pl.BlockSpec((tm // D_STRIDE, D_STRIDE * LANES), lambda r: (r, 0))
    out_specs = (row(d_conv), headed_t(N_HEADS), blocks, blocks,
                 headed(N_KV_HEADS), headed_t(N_KV_HEADS), headed(N_KV_HEADS),
                 headed_t(N_KV_HEADS), row(N_KV_HEADS * LANES), row(2 * d))
    kern = functools.partial(_inproj_kernel, tm=tm, tiles_per_seq=tiles_per_seq, d_conv=d_conv)
    return pl.pallas_call(
        kern, out_shape=out_shape, grid=(n_tiles,),
        in_specs=[row(d), pl.BlockSpec((1, d), lambda r: (0, 0)),
                  full(wu), full(wq), full(wkv), full(wng), full(wmg)],
        out_specs=out_specs, scratch_shapes=[pltpu.VMEM((2, tm, LANES), F32)],
        compiler_params=_cparams(1), name="inproj",
    )(x2, norm_g.reshape(1, d), wu, wq, wkv, wng, wmg)


def _compress_kernel(rk_ref, rv_ref, pek_ref, pev_ref, w1k_ref, w1v_ref, w2k_ref, w2v_ref,
                     kcmp_ref, vcmp_ref, *, ncp):
    def one(r_ref, pe_ref, w1_ref, w2_ref):
        r = r_ref[0]
        top = _dot(r, w1_ref[0])
        bot = _dot(r, w1_ref[1])
        pe_h = _dot(pe_ref[0], w1_ref[0]) + _dot(pe_ref[1], w1_ref[1])
        nxt = pltpu.roll(bot, ncp - 1, 0)
        rowi = lax.broadcasted_iota(jnp.int32, top.shape, 0)
        hid = top + jnp.where(rowi == ncp - 1, 0.0, nxt) + pe_h[0:1, :]
        return _dot(_gelu(hid).astype(BF16), w2_ref[...])

    kcmp_ref[0] = one(rk_ref, pek_ref, w1k_ref, w2k_ref).astype(BF16)
    v = one(rv_ref, pev_ref, w1v_ref, w2v_ref)
    rowt = lax.broadcasted_iota(jnp.int32, (LANES, ncp), 0)
    ones_rows = (rowt >= HEAD_DIM) & (rowt < HEAD_DIM + 8)
    for g in range(N_KV_HEADS):
        vt = jnp.transpose(v[:, g * LANES:(g + 1) * LANES])
        vcmp_ref[0, g] = jnp.where(ones_rows, 1.0, vt).astype(BF16)


def _prep_compress_weights(pe, w1, w2):
    half = L_CMP // 2
    w1r = w1.reshape(L_CMP, HEAD_DIM, CMP_HIDDEN).astype(BF16)
    w2p = jnp.pad(w2, ((0, 0), (0, LANES - HEAD_DIM))).astype(BF16)

    def per_group(w, axis):
        z = jnp.zeros_like(w)
        return jnp.stack([jnp.concatenate([w if k == g else z for k in range(N_KV_HEADS)], axis=-1)
                          for g in range(N_KV_HEADS)], axis=axis)

    w1big = per_group(w1r, 1).reshape(2, half * N_KV_HEADS * HEAD_DIM, N_KV_HEADS * CMP_HIDDEN)
    w2big = per_group(w2p, 0).reshape(N_KV_HEADS * CMP_HIDDEN, N_KV_HEADS * LANES)
    per = pe.reshape(2, half, 1, HEAD_DIM)
    pebig = jnp.broadcast_to(per, (2, half, N_KV_HEADS, HEAD_DIM)).reshape(2, 1, -1)
    pebig = jnp.broadcast_to(pebig, (2, 8, pebig.shape[-1])).astype(BF16)
    return pebig, w1big, w2big


def _compress(kc, vc, wk, wv, batch, seq):
    ncp = seq // D_STRIDE
    width = D_STRIDE * LANES
    rk = kc.reshape(batch, ncp, width)
    rv = vc.reshape(batch, ncp, width)
    pek, w1k, w2k = wk
    pev, w1v, w2v = wv
    c3 = lambda a: pl.BlockSpec(a.shape, lambda b: (0, 0, 0))
    c2 = lambda a: pl.BlockSpec(a.shape, lambda b: (0, 0))
    rspec = pl.BlockSpec((1, ncp, width), lambda b: (b, 0, 0))
    kspec = pl.BlockSpec((1, ncp, N_KV_HEADS * LANES), lambda b: (b, 0, 0))
    kshape = jax.ShapeDtypeStruct((batch, ncp, N_KV_HEADS * LANES), BF16)
    vspec = pl.BlockSpec((1, N_KV_HEADS, LANES, ncp), lambda b: (b, 0, 0, 0))
    vshape = jax.ShapeDtypeStruct((batch, N_KV_HEADS, LANES, ncp), BF16)
    return pl.pallas_call(
        functools.partial(_compress_kernel, ncp=ncp), out_shape=(kshape, vshape), grid=(batch,),
        in_specs=[rspec, rspec, c3(pek), c3(pev), c3(w1k), c3(w1v), c2(w2k), c2(w2v)],
        out_specs=(kspec, vspec), compiler_params=_cparams(1), name="compress",
    )(rk, rv, pek, pev, w1k, w1v, w2k, w2v)


def _convmix_kernel(a_ref, halo_ref, ga_ref, cw_ref, cb_ref, lg_ref, lb_ref, wo_ref,
                    ya_ref, ext_ref, sh_ref, y_ref, *, tm):
    i = pl.program_id(1)
    halo = halo_ref[0].astype(F32)
    ext_ref[0:CONV_HALO, :] = jnp.where(i == 0, 0.0, halo)
    ext_ref[CONV_HALO:, :] = a_ref[0].astype(F32)
    span = sh_ref.shape[1]
    for ph in range(1, SUBLANES):
        sh_ref[ph - 1] = ext_ref[ph:ph + span, :]
    off = CONV_HALO - (CONV_WIDTH - 1)

    def rows_chunk(cidx, carry):
        r0 = pl.multiple_of(cidx * CONV_ROWS, CONV_ROWS)
        acc = jnp.zeros((CONV_ROWS, a_ref.shape[-1]), F32) + cb_ref[...]
        for j in range(CONV_WIDTH):
            ph = (off + j) % SUBLANES
            base = off + j - ph
            if ph == 0:
                xs = ext_ref[pl.ds(r0 + base, CONV_ROWS), :]
            else:
                xs = sh_ref[ph - 1, pl.ds(r0 + base, CONV_ROWS), :]
            acc = acc + jnp.tile(cw_ref[j], (CONV_ROWS // SUBLANES, 1)) * xs
        y_ref[pl.ds(r0, CONV_ROWS), :] = acc
        return carry

    lax.fori_loop(0, tm // CONV_ROWS, rows_chunk, 0)
    acc = y_ref[...]
    mu = jnp.mean(acc, axis=-1, keepdims=True)
    cen = acc - mu
    var = jnp.mean(cen * cen, axis=-1, keepdims=True)
    y = cen * lax.rsqrt(var + EPS) * lg_ref[...] + lb_ref[...]
    y = y * _sigmoid(y)
    yc = _dot(y.astype(BF16), wo_ref[...])
    ya_ref[...] = (ga_ref[...].astype(F32) * yc).astype(BF16)


def _convmix(a, mg, conv_w, conv_b, ln_g, ln_b, w_conv_out, batch, seq, tm=512):
    t, d_conv = a.shape
    d = w_conv_out.shape[1]
    a3 = a.reshape(batch, seq, d_conv)
    nt = seq // tm
    hb = tm // CONV_HALO
    cw = jnp.broadcast_to(conv_w[:, None, :], (CONV_WIDTH, SUBLANES, d_conv))
    vec = lambda v: v.reshape(1, -1)
    c2 = lambda a_: pl.BlockSpec(a_.shape, lambda b, i: (0,) * a_.ndim)
    return pl.pallas_call(
        functools.partial(_convmix_kernel, tm=tm),
        out_shape=jax.ShapeDtypeStruct((t, d), BF16), grid=(batch, nt),
        in_specs=[pl.BlockSpec((1, tm, d_conv), lambda b, i: (b, i, 0)),
                  pl.BlockSpec((1, CONV_HALO, d_conv),
                               lambda b, i: (b, jnp.maximum(i * hb - 1, 0), 0)),
                  pl.BlockSpec((tm, d), lambda b, i: (b * nt + i, 0)),
                  c2(cw), c2(vec(conv_b)), c2(vec(ln_g)), c2(vec(ln_b)), c2(w_conv_out)],
        out_specs=pl.BlockSpec((tm, d), lambda b, i: (b * nt + i, 0)),
        scratch_shapes=[pltpu.VMEM((tm + CONV_HALO, d_conv), F32),
                        pltpu.VMEM((SUBLANES - 1, tm + CONV_HALO - SUBLANES, d_conv), F32),
                        pltpu.VMEM((tm, d_conv), F32)],
        compiler_params=_cparams(2), name="convmix",
    )(a3, a3, mg, cw, vec(conv_b), vec(ln_g), vec(ln_b), w_conv_out)


def _nsa_kernel(qt_ref, kcmp_ref, vcmpt_ref, biasc_ref, ks_ref, vst_ref, kw_ref, vwt_ref,
                ng_ref, aggt_ref, bd_ref, o_ref,
                imp_ref, qa_ref, oc_ref, ms_ref, accs_ref, mw_ref, accw_ref, *, ns):
    i = pl.program_id(1)
    heads = range(N_HEADS)

    def k_tile(ref, g, j):
        return ref[0, g, pl.ds(pl.multiple_of(j * TQ, TQ), TQ), :]

    def vt_tile(ref, g, j):
        return ref[0, g, 0:PV_ROWS, pl.ds(pl.multiple_of(j * TQ, TQ), TQ)]

    def pipelined(tasks, scores, update):
        pending = [scores(t) for t in tasks[:PIPE_DEPTH]]
        for t, task in enumerate(tasks):
            update(task, pending.pop(0))
            if t + PIPE_DEPTH < len(tasks):
                pending.append(scores(tasks[t + PIPE_DEPTH]))

    def run(tasks):
        def scores(task):
            hh, kt, _, qmat, _, _, row0 = task
            s = _dot(kt, qmat)
            if row0 is None:
                return s
            if not isinstance(row0, int):
                row0 = pl.multiple_of(row0, TQ)
            return s + bd_ref[hh, pl.ds(row0, TQ), :]

        def update(task, s):
            hh, _, vt, _, m_ref, acc_ref, _ = task
            m_old = m_ref[hh]
            m_new = jnp.maximum(m_old, jnp.max(s, axis=0, keepdims=True))
            p = jnp.exp2(s - m_new).astype(BF16)
            acc_ref[hh] = jnp.exp2(m_old - m_new) * acc_ref[hh] + _dot(vt, p)
            m_ref[hh] = m_new

        pipelined(tasks, scores, update)

    def slc_tasks(j, row0=None):
        tiles = [(k_tile(ks_ref, g, j), vt_tile(vst_ref, g, j)) for g in range(N_KV_HEADS)]
        return [(hh,) + tiles[hh // HPG] + (qa_ref[hh], ms_ref, accs_ref, row0) for hh in heads]

    def win_tasks(j, row0):
        tiles = [(k_tile(kw_ref, g, j), vt_tile(vwt_ref, g, j)) for g in range(N_KV_HEADS)]
        return [(hh,) + tiles[hh // HPG] + (qt_ref[0, hh], mw_ref, accw_ref, row0) for hh in heads]

    for hh in heads:
        ms_ref[hh] = jnp.full((1, TQ), M_INIT, F32)
        mw_ref[hh] = jnp.full((1, TQ), M_INIT, F32)
        accs_ref[hh] = jnp.zeros((PV_ROWS, TQ), F32)
        accw_ref[hh] = jnp.zeros((PV_ROWS, TQ), F32)

    p_parts = []

    def cmp_scores(hh):
        g = hh // HPG
        return _dot(kcmp_ref[0, :, g * LANES:(g + 1) * LANES], qt_ref[0, hh]) + biasc_ref[hh]

    def cmp_update(hh, s_c):
        m_c = jnp.maximum(jnp.max(s_c, axis=0, keepdims=True), M_INIT)
        p_c = jnp.exp2(s_c - m_c)
        l_c = jnp.sum(p_c, axis=0, keepdims=True)
        p_c = p_c * jnp.where(l_c > 0.0, 1.0 / l_c, 0.0)
        oc_ref[hh] = _dot(vcmpt_ref[0, hh // HPG, 0:HEAD_DIM, :], p_c.astype(BF16))
        p_parts.append(p_c)

    pipelined(list(heads), cmp_scores, cmp_update)

    blk = lax.broadcasted_iota(jnp.int32, (ns, TQ), 0)
    cur = (i * TQ + lax.broadcasted_iota(jnp.int32, (ns, TQ), 1)) // L_SLC
    forced = (blk == 0) | (blk == cur) | (blk == cur - 1)
    for g in range(N_KV_HEADS):
        p4 = p_parts[g * HPG:(g + 1) * HPG]
        p_sum = (p4[0] + p4[1]) + (p4[2] + p4[3])
        p_hi = p_sum.astype(BF16)
        p_lo = (p_sum - p_hi.astype(F32)).astype(BF16)
        imp = _dot(aggt_ref[...], p_hi) + _dot(aggt_ref[...], p_lo)
        imp_ref[g] = jnp.where(forced, FORCE, jnp.where(blk <= cur, imp, -FORCE))

    rows2 = jnp.where(i >= 2, 0, 3 * TQ)
    rows1 = jnp.where(i >= 1, TQ, 3 * TQ)
    run(win_tasks(jnp.maximum(i - 2, 0), rows2) + win_tasks(jnp.maximum(i - 1, 0), rows1)
        + win_tasks(i, 2 * TQ))

    n_grp = ns // SUBLANES
    blk8 = lax.broadcasted_iota(jnp.int32, (SUBLANES, TQ), 0)
    for g in range(N_KV_HEADS):
        grp = [imp_ref[g, SUBLANES * r:SUBLANES * (r + 1), :] for r in range(n_grp)]
        rank = [jnp.zeros((SUBLANES, TQ), F32) for _ in range(n_grp)]
        for k in range(ns):
            rowk = imp_ref[g, k:k + 1, :]
            kg = k // SUBLANES
            for r in range(n_grp):
                if r < kg:
                    beats = jnp.where(rowk > grp[r], 1.0, 0.0)
                elif r > kg:
                    beats = jnp.where(rowk >= grp[r], 1.0, 0.0)
                else:
                    beats = jnp.where(blk8 > k - SUBLANES * kg,
                                      jnp.where(rowk >= grp[r], 1.0, 0.0),
                                      jnp.where(rowk > grp[r], 1.0, 0.0))
                rank[r] = rank[r] + beats
        rank = jnp.concatenate(rank, axis=0)
        flags = jnp.where(rank < float(N_SEL), 0.0, -1.0).astype(BF16)
        if ns < HEAD_DIM:
            flags = jnp.concatenate([flags, jnp.zeros((HEAD_DIM - ns, TQ), BF16)], axis=0)
        for hh in range(g * HPG, (g + 1) * HPG):
            qa_ref[hh, 0:HEAD_DIM, :] = qt_ref[0, hh, 0:HEAD_DIM, :]
            qa_ref[hh, HEAD_DIM:, :] = flags

    n_far = jnp.maximum(i - 1, 0)

    def far_group(jj, carry):
        tasks = []
        for u in range(FAR_GROUP):
            tasks += slc_tasks(FAR_GROUP * jj + u)
        run(tasks)
        return carry

    lax.fori_loop(0, n_far // FAR_GROUP, far_group, 0)
    for rem in range(FAR_GROUP):
        @pl.when(n_far % FAR_GROUP == rem)
        def _(rem=rem):
            tasks = []
            for u in range(rem):
                tasks += slc_tasks(n_far - rem + u)
            run(tasks + slc_tasks(jnp.maximum(i - 1, 0), rows1) + slc_tasks(i, 2 * TQ))

    gates_t = jnp.transpose(ng_ref[...])
    outs = []
    for hh in heads:
        a_s, a_w = accs_ref[hh], accw_ref[hh]
        o_s = a_s[0:HEAD_DIM] * (1.0 / a_s[HEAD_DIM:HEAD_DIM + 1])
        o_w = a_w[0:HEAD_DIM] * (1.0 / a_w[HEAD_DIM:HEAD_DIM + 1])
        row = (hh // HPG) * LANES + 3 * (hh % HPG)
        outs.append(gates_t[row:row + 1] * oc_ref[hh] + gates_t[row + 1:row + 2] * o_s
                    + gates_t[row + 2:row + 3] * o_w)
    o_ref[...] = jnp.transpose(jnp.concatenate(outs, axis=0)).astype(BF16)


def _bias_of(thr_ref, rb_ref, dist, h):
    b = jnp.full(dist.shape, rb_ref[h], F32)
    for k in range(1, N_BUCKETS):
        b = jnp.where(dist >= thr_ref[k], rb_ref[k * N_HEADS + h], b)
    return b


def _cmp_bias_kernel(thr_ref, rb_ref, biasc_ref, *, nc, ncp):
    i = pl.program_id(0)
    per_tile = TQ // D_STRIDE
    band = 2 * per_tile
    assert TQ + D_STRIDE - (L_CMP - 1) >= MAX_DISTANCE

    def dist_of(c0, rows):
        c = c0 + lax.broadcasted_iota(jnp.int32, (rows, TQ), 0)
        r = lax.broadcasted_iota(jnp.int32, (rows, TQ), 1)
        dist = i * TQ + r - (c * D_STRIDE + L_CMP - 1)
        return dist, (dist >= 0) & (c < nc)

    _, ok_all = dist_of(0, ncp)
    band0 = pl.multiple_of(jnp.maximum(i * per_tile - per_tile, 0), per_tile)
    dist_b, ok_b = dist_of(band0, band)
    for h in range(N_HEADS):
        biasc_ref[h] = jnp.where(ok_all, rb_ref[(N_BUCKETS - 1) * N_HEADS + h], NEG)
        biasc_ref[h, pl.ds(band0, band), :] = jnp.where(
            ok_b, _bias_of(thr_ref, rb_ref, dist_b, h), NEG)


def _diag_bias_kernel(thr_ref, rb_ref, bd_ref):
    d0 = (lax.broadcasted_iota(jnp.int32, (TQ, TQ), 1)
          - lax.broadcasted_iota(jnp.int32, (TQ, TQ), 0))
    for h in range(N_HEADS):
        far = rb_ref[(N_BUCKETS - 1) * N_HEADS + h]
        bd_ref[h, 0:TQ, :] = jnp.where(d0 < 0, _bias_of(thr_ref, rb_ref, d0 + 2 * TQ, h) - far, NEG)
        bd_ref[h, TQ:2 * TQ, :] = _bias_of(thr_ref, rb_ref, d0 + TQ, h) - far
        bd_ref[h, 2 * TQ:3 * TQ, :] = jnp.where(d0 >= 0, _bias_of(thr_ref, rb_ref, d0, h) - far, NEG)
        bd_ref[h, 3 * TQ:, :] = jnp.full((TQ, TQ), NEG, F32)


def _attention_tables(rel_bias, seq):
    ncp = seq // D_STRIDE
    nc = (seq - L_CMP) // D_STRIDE + 1
    ns = seq // L_SLC
    nt = seq // TQ
    n_probe = 2 * MAX_DISTANCE
    buckets = _t5_bucket(jnp.arange(n_probe))
    thr = jnp.sum(buckets[None, :] < jnp.arange(N_BUCKETS)[:, None], axis=1).astype(jnp.int32)
    assert WINDOW == 2 * TQ
    rb = (rel_bias.astype(F32) * LOG2E).reshape(-1)
    bias_c = pl.pallas_call(
        functools.partial(_cmp_bias_kernel, nc=nc, ncp=ncp),
        out_shape=jax.ShapeDtypeStruct((N_HEADS, ncp, seq), F32),
        grid_spec=pltpu.PrefetchScalarGridSpec(
            num_scalar_prefetch=2, grid=(nt,), in_specs=[],
            out_specs=pl.BlockSpec((N_HEADS, ncp, TQ), lambda i, *_: (0, 0, i))),
        compiler_params=_cparams(1), name="cmp_bias")(thr, rb)
    bd = pl.pallas_call(
        _diag_bias_kernel, out_shape=jax.ShapeDtypeStruct((N_HEADS, 4 * TQ, TQ), F32),
        grid_spec=pltpu.PrefetchScalarGridSpec(
            num_scalar_prefetch=2, grid=(1,), in_specs=[],
            out_specs=pl.BlockSpec((N_HEADS, 4 * TQ, TQ), lambda i, *_: (0, 0, 0))),
        compiler_params=_cparams(1), name="diag_bias")(thr, rb)
    c_start = jnp.arange(ncp) * D_STRIDE
    c_end = c_start + L_CMP - 1
    s_start = jnp.arange(ns) * L_SLC
    aggt = ((c_end[None, :] >= s_start[:, None]) & (c_start[None, :] <= s_start[:, None] + L_SLC - 1)
            & (jnp.arange(ncp)[None, :] < nc)).astype(BF16)
    return bias_c, bd, aggt


def _nsa(qt, kcmp, vcmpt, ks, vst, kw, vwt, ng, rel_bias, batch, seq):
    first_far = TQ + 1
    assert MAX_EXACT + int(math.log(first_far / MAX_EXACT) / math.log(MAX_DISTANCE / MAX_EXACT)
                           * (N_BUCKETS - MAX_EXACT)) >= N_BUCKETS - 1, "far tiles need one bucket"
    ncp = seq // D_STRIDE
    ns = seq // L_SLC
    nt = seq // TQ
    bias_c, bd, aggt = _attention_tables(rel_bias, seq)
    kspec = pl.BlockSpec((1, N_KV_HEADS, seq, LANES), lambda b, i: (b, 0, 0, 0))
    vtspec = pl.BlockSpec((1, N_KV_HEADS, LANES, seq), lambda b, i: (b, 0, 0, 0))
    state = lambda rows: pltpu.VMEM((N_HEADS, rows, TQ), F32)
    const = lambda shape: pl.BlockSpec(shape, lambda b, i: (0,) * len(shape),
                                       pipeline_mode=pl.Buffered(1))
    grid_spec = pltpu.PrefetchScalarGridSpec(
        num_scalar_prefetch=0, grid=(batch, nt),
        in_specs=[pl.BlockSpec((1, N_HEADS, LANES, TQ), lambda b, i: (b, 0, 0, i)),
                  pl.BlockSpec((1, ncp, N_KV_HEADS * LANES), lambda b, i: (b, 0, 0)),
                  pl.BlockSpec((1, N_KV_HEADS, LANES, ncp), lambda b, i: (b, 0, 0, 0)),
                  pl.BlockSpec((N_HEADS, ncp, TQ), lambda b, i: (0, 0, i)),
                  kspec, vtspec, kspec, vtspec,
                  pl.BlockSpec((TQ, N_KV_HEADS * LANES), lambda b, i: (b * nt + i, 0)),
                  const((ns, ncp)), const((N_HEADS, 4 * TQ, TQ))],
        out_specs=pl.BlockSpec((TQ, N_HEADS * HEAD_DIM), lambda b, i: (b * nt + i, 0)),
        scratch_shapes=[pltpu.VMEM((N_KV_HEADS, ns, TQ), F32),
                        pltpu.VMEM((N_HEADS, LANES, TQ), BF16),
                        state(HEAD_DIM),
                        state(1), state(PV_ROWS),
                        state(1), state(PV_ROWS)])
    return pl.pallas_call(
        functools.partial(_nsa_kernel, ns=ns),
        out_shape=jax.ShapeDtypeStruct((batch * seq, N_HEADS * HEAD_DIM), BF16),
        grid_spec=grid_spec, compiler_params=_cparams(2), name="nsa",
    )(qt, kcmp, vcmpt, bias_c, ks, vst, kw, vwt, ng, aggt, bd)


def _ffn_kernel(x_ref, ya_ref, o_ref, gb_ref, p_ref, wao_ref, wout_ref, nf_ref, wup_ref, cw_ref,
                cb_ref, wd_ref, np_ref, wpg_ref, wpe_ref, nfin_ref, out_ref, carry_ref,
                *, tm, tiles_per_seq, d_ff, chunks):
    r = pl.program_id(0)

    @pl.when((r % tiles_per_seq) == 0)
    def _():
        carry_ref[...] = jnp.zeros_like(carry_ref)

    y = ya_ref[...].astype(F32) + gb_ref[...].astype(F32) * _dot(o_ref[...], wao_ref[...])
    x1 = x_ref[...] + _dot(y.astype(BF16), wout_ref[...])
    hf = _rms(x1, nf_ref[...]).astype(BF16)

    def conv3(c0, width):
        up = _dot(hf, wup_ref[:, c0:c0 + width])
        rowi = lax.broadcasted_iota(jnp.int32, up.shape, 0)
        prev = carry_ref[:, c0:c0 + width]
        s1 = jnp.where(rowi == 0, prev[7:8, :], pltpu.roll(up, 1, 0))
        s2 = jnp.where(rowi == 0, prev[6:7, :],
                       jnp.where(rowi == 1, prev[7:8, :], pltpu.roll(up, 2, 0)))
        carry_ref[:, c0:c0 + width] = up[tm - 8:, :]
        return (cw_ref[0:1, c0:c0 + width] * s2 + cw_ref[1:2, c0:c0 + width] * s1
                + cw_ref[2:3, c0:c0 + width] * up + cb_ref[:, c0:c0 + width])

    acc = None
    for c0, width in chunks:
        act = (_gelu(conv3(c0, width)) * conv3(d_ff + c0, width)).astype(BF16)
        part = _dot(act, wd_ref[c0:c0 + width, :])
        acc = part if acc is None else acc + part

    x2 = x1 + acc
    pg = _sigmoid(_dot(_rms(x2, np_ref[...]).astype(BF16), wpg_ref[...]))
    pe = _dot(p_ref[...].astype(BF16), wpe_ref[...])
    out_ref[...] = _rms(x2 + pg * pe, nfin_ref[...])


def _ffn(x2, ya, o, mg, p2, w_attn_out, w_out, norm_ffn, w_up, ffn_dw_w, ffn_dw_b, w_down,
         norm_ple, w_ple_gate, w_ple, norm_final, seq, tm=512):
    t, d = x2.shape
    d_ff = w_down.shape[0]
    assert d_ff % MXU_DEPTH == 0
    chunks, c0 = [], 0
    while c0 < d_ff:
        width = min(FFN_CHUNK, d_ff - c0)
        chunks.append((c0, width))
        c0 += width
    tiles_per_seq = seq // tm
    cw = jnp.concatenate([ffn_dw_w, jnp.zeros((8 - FFN_CONV_WIDTH, 2 * d_ff), ffn_dw_w.dtype)], 0)
    vec = lambda v: v.reshape(1, -1)
    const = lambda a: pl.BlockSpec(a.shape, lambda r: (0, 0), pipeline_mode=pl.Buffered(1))
    rows = lambda n: pl.BlockSpec((tm, n), lambda r: (r, 0))
    kern = functools.partial(_ffn_kernel, tm=tm, tiles_per_seq=tiles_per_seq, d_ff=d_ff,
                             chunks=tuple(chunks))
    operands = (x2, ya, o, mg, p2, w_attn_out, w_out, vec(norm_ffn), w_up, cw, vec(ffn_dw_b),
                w_down, vec(norm_ple), w_ple_gate, w_ple, vec(norm_final))
    gb_spec = pl.BlockSpec((tm, d), lambda r: (r, 1))
    return pl.pallas_call(
        kern, out_shape=jax.ShapeDtypeStruct((t, d), F32), grid=(t // tm,),
        in_specs=[rows(d), rows(d), rows(o.shape[1]), gb_spec, rows(p2.shape[1])]
        + [const(a) for a in operands[5:]],
        out_specs=rows(d),
        scratch_shapes=[pltpu.VMEM((8, 2 * d_ff), F32)],
        compiler_params=_cparams(1), name="ffn",
    )(*operands)


def kernel(x, p, rel_bias, norm_mix, w_in, conv_dw_w, conv_dw_b, conv_ln_g, conv_ln_b, w_conv_out,
           cmp_pe_k, cmp_pe_v, w_ck1, w_ck2, w_cv1, w_cv2, w_attn_out, w_out, norm_ffn, w_up,
           ffn_dw_w, ffn_dw_b, w_down, norm_ple, w_ple_gate, w_ple, norm_final):
    batch, seq, d = x.shape
    depth = w_in.shape[0]
    x2 = x.reshape(batch * seq, d)
    for i in range(depth):
        a, qt, kc, vc, ks, vst, kw, vwt, ng, mg = _inproj(
            x2, norm_mix[i], _prep_inproj_weights(w_in[i], d), batch, seq)
        kcmp, vcmpt = _compress(kc, vc,
                               _prep_compress_weights(cmp_pe_k[i], w_ck1[i], w_ck2[i]),
                               _prep_compress_weights(cmp_pe_v[i], w_cv1[i], w_cv2[i]),
                               batch, seq)
        ya = _convmix(a, mg, conv_dw_w[i], conv_dw_b[i], conv_ln_g[i], conv_ln_b[i],
                      w_conv_out[i].astype(BF16), batch, seq)
        o = _nsa(qt, kcmp, vcmpt, ks, vst, kw, vwt, ng, rel_bias, batch, seq)
        assert i == depth - 1, "the final RMSNorm is fused into the (single) layer's MLP kernel"
        x2 = _ffn(x2, ya, o, mg, p[i].reshape(batch * seq, -1), w_attn_out[i].astype(BF16),
                  w_out[i].astype(BF16), norm_ffn[i], w_up[i].astype(BF16), ffn_dw_w[i],
                  ffn_dw_b[i], w_down[i].astype(BF16), norm_ple[i], w_ple_gate[i].astype(BF16),
                  w_ple[i].astype(BF16), norm_final, seq)
    return x2.reshape(batch, seq, d)
```

```python
import functools
import math

import jax
import jax.numpy as jnp
import numpy as np
from jax import lax
from jax.experimental import pallas as pl
from jax.experimental.pallas import tpu as pltpu

N_HEADS = 8
HEAD_DIM = 64
N_KV_HEADS = 2
HPG = N_HEADS // N_KV_HEADS
L_CMP = 32
D_STRIDE = 16
CMP_HIDDEN = 256
L_SLC = 64
N_SEL = 16
WINDOW = 512
N_BUCKETS = 32
MAX_EXACT = N_BUCKETS // 2
MAX_DISTANCE = 128
CONV_WIDTH = 31
FFN_CONV_WIDTH = 3
EPS = 1e-6
FORCE = 1e4

LANES = 128
MXU_DEPTH = 256
FFN_CHUNK = 6 * MXU_DEPTH
FFN_SPLIT = 2
NEG = -1e30
M_INIT = -1e29
LOG2E = 1.4426950408889634
PIPE_DEPTH = 6
FAR_GROUP = 4
TQ = 256
PV_ROWS = HEAD_DIM + 16
SUBLANES = 8
CONV_HALO = 32
CONV_ROWS = 64
VMEM_LIMIT = 56 * 1024 * 1024

F32 = jnp.float32
BF16 = jnp.bfloat16


def _cparams(n_axes):
    return pltpu.CompilerParams(dimension_semantics=("arbitrary",) * n_axes,
                                vmem_limit_bytes=VMEM_LIMIT)


def _dot(a, b):
    return jnp.dot(a, b, preferred_element_type=F32)


def _dot_nt(a, b):
    return lax.dot_general(a, b, (((1,), (1,)), ((), ())), preferred_element_type=F32)


def _rms(xf, g):
    return xf * lax.rsqrt(jnp.mean(xf * xf, axis=-1, keepdims=True) + EPS) * g


def _sigmoid(x):
    return 1.0 / (1.0 + jnp.exp(-x))


def _gelu(x):
    return 0.5 * x * (1.0 + jnp.tanh(0.7978845608028654 * (x + 0.044715 * x * x * x)))


def _t5_bucket(dist):
    n = jnp.maximum(dist, 0)
    nf = jnp.maximum(n, MAX_EXACT).astype(F32)
    large = MAX_EXACT + (jnp.log(nf / MAX_EXACT) / math.log(MAX_DISTANCE / MAX_EXACT)
                         * (N_BUCKETS - MAX_EXACT)).astype(jnp.int32)
    large = jnp.minimum(large, N_BUCKETS - 1)
    return jnp.where(n < MAX_EXACT, n, large)


def _inproj_kernel(x_ref, g_ref, wu_ref, wq_ref, wkv_ref, wng_ref, wmg_ref,
                   cw_ref, cb_ref, lg_ref, lb_ref, wco_ref,
                   ya_ref, qt_ref, kc_ref, vc_ref, ks_ref, vst_ref, kw_ref, vwt_ref, ng_ref, gb_ref,
                   cs_ref, ext_ref, sh_ref, ga_ref, *, tm, n_tiles, tiles_per_seq, d_conv):
    step = pl.program_id(0)
    r = jnp.minimum(step, n_tiles - 1)
    cur = step % 2
    prv = 1 - cur

    @pl.when(step == 0)
    def _():
        ext_ref[1] = jnp.zeros(ext_ref.shape[1:], F32)
        ga_ref[1] = jnp.zeros(ga_ref.shape[1:], BF16)

    h = _rms(x_ref[...], g_ref[...]).astype(BF16)
    u = _dot(h, wu_ref[...])
    seq_start = (r % tiles_per_seq) == 0
    ext_ref[cur, 0:CONV_HALO, :] = jnp.where(seq_start, 0.0, ext_ref[prv, tm:tm + CONV_HALO, :])
    ext_ref[cur, CONV_HALO:, :] = u[:, :d_conv] * _sigmoid(u[:, d_conv:])
    zmg = _dot(h, wmg_ref[...])
    d_model = zmg.shape[1] // 2
    ga_ref[cur] = _sigmoid(zmg[:, :d_model]).astype(BF16)
    gb_ref[...] = _sigmoid(zmg[:, d_model:]).astype(BF16)

    zqt = jnp.transpose(_dot(h, wq_ref[...]))
    zero_rows = jnp.zeros((HEAD_DIM, tm), BF16)
    for hh in range(N_HEADS):
        qt_ref[0, hh, 0:HEAD_DIM, :] = zqt[hh * HEAD_DIM:(hh + 1) * HEAD_DIM].astype(BF16)
        qt_ref[0, hh, HEAD_DIM:, :] = zero_rows

    zkv = _dot(h, wkv_ref[...])
    for which, out_ref in enumerate((kc_ref, vc_ref)):
        cs_ref[which] = zkv[:, which * LANES:(which + 1) * LANES]
        for tok in range(D_STRIDE):
            rows = cs_ref[which, pl.ds(tok, tm // D_STRIDE, stride=D_STRIDE), :]
            out_ref[:, tok * LANES:(tok + 1) * LANES] = rows.astype(BF16)
    lane = lax.broadcasted_iota(jnp.int32, (tm, LANES), 1)
    row = lax.broadcasted_iota(jnp.int32, (tm, LANES), 0)
    spos = (r % tiles_per_seq) * tm + row
    lo = lane < HEAD_DIM
    blk_tag = jnp.where((lane - HEAD_DIM) == spos // L_SLC, -NEG, 0.0)
    ones_rows = jnp.where(lax.broadcasted_iota(jnp.int32, (HEAD_DIM, tm), 0) < 8, 1.0, 0.0)
    ones_rows = ones_rows.astype(BF16)
    for g in range(N_KV_HEADS):
        for k_ref, vt_ref, base, tag in ((ks_ref, vst_ref, (2 + g) * LANES, blk_tag),
                                         (kw_ref, vwt_ref, (4 + g) * LANES, 0.0)):
            pair = zkv[:, base:base + LANES]
            k_ref[0, g] = jnp.where(lo, pair, tag).astype(BF16)
            vt_ref[0, g, 0:HEAD_DIM, :] = jnp.transpose(pair)[HEAD_DIM:].astype(BF16)
            vt_ref[0, g, HEAD_DIM:, :] = ones_rows

    ng_ref[...] = _sigmoid(_dot(h, wng_ref[...]))

    span = sh_ref.shape[1]
    for ph in range(1, SUBLANES):
        sh_ref[ph - 1] = ext_ref[prv, ph:ph + span, :]
    off = CONV_HALO - (CONV_WIDTH - 1)
    conv = []
    for r0 in range(0, tm, CONV_ROWS):
        acc = jnp.zeros((CONV_ROWS, d_conv), F32) + cb_ref[...]
        for j in range(CONV_WIDTH):
            ph = (off + j) % SUBLANES
            base = r0 + off + j - ph
            if ph == 0:
                xs = ext_ref[prv, base:base + CONV_ROWS, :]
            else:
                xs = sh_ref[ph - 1, base:base + CONV_ROWS, :]
            acc = acc + jnp.tile(cw_ref[j], (CONV_ROWS // SUBLANES, 1)) * xs
        conv.append(acc)
    conv = jnp.concatenate(conv, axis=0)
    mu = jnp.mean(conv, axis=-1, keepdims=True)
    cen = conv - mu
    var = jnp.mean(cen * cen, axis=-1, keepdims=True)
    y = cen * lax.rsqrt(var + EPS) * lg_ref[...] + lb_ref[...]
    y = y * _sigmoid(y)
    ya_ref[...] = (ga_ref[prv].astype(F32) * _dot(y.astype(BF16), wco_ref[...])).astype(BF16)


def _prep_inproj_weights(w_in, d_model):
    d_conv = d_model // 2
    n_conv = 2 * d_conv
    n_q = N_HEADS * HEAD_DIM
    n_kv = 6 * N_KV_HEADS * HEAD_DIM
    n_ng = 3 * N_HEADS
    o = 0
    wu = w_in[:, o:o + n_conv]; o += n_conv
    wq = w_in[:, o:o + n_q]; o += n_q
    wkv = w_in[:, o:o + n_kv]; o += n_kv
    wng = w_in[:, o:o + n_ng]; o += n_ng
    wmg = w_in[:, o:]
    wq_p = wq * (LOG2E / math.sqrt(HEAD_DIM))
    kvcols = [wkv[:, 0:LANES], wkv[:, LANES:2 * LANES]]
    for k_kind in (2, 4):
        for g in range(N_KV_HEADS):
            for kind in (k_kind, k_kind + 1):
                c0 = kind * N_KV_HEADS * HEAD_DIM + g * HEAD_DIM
                kvcols.append(wkv[:, c0:c0 + HEAD_DIM])
    wkv_p = jnp.concatenate(kvcols, axis=1)
    per_g = HPG * 3
    ngcols = []
    for g in range(N_KV_HEADS):
        ngcols += [wng[:, g * per_g:(g + 1) * per_g],
                   jnp.zeros((d_model, LANES - per_g), w_in.dtype)]
    wng_p = jnp.concatenate(ngcols, axis=1)
    return tuple(w.astype(BF16) for w in (wu, wq_p, wkv_p, wng_p, wmg))


def _inproj(x2, norm_g, weights, conv_w, conv_b, ln_g, ln_b, w_conv_out, batch, seq, tm=512):
    t, d = x2.shape
    wu, wq, wkv, wng, wmg = weights
    d_conv = d // 2
    tiles_per_seq = seq // tm
    n_tiles = t // tm
    assert tm % CONV_ROWS == 0 and CONV_HALO >= CONV_WIDTH - 1 and CONV_HALO % SUBLANES == 0
    cw = jnp.broadcast_to(conv_w[:, None, :], (CONV_WIDTH, SUBLANES, d_conv))
    vec = lambda v: v.reshape(1, -1)
    full = lambda w: pl.BlockSpec(w.shape, lambda s: (0,) * w.ndim, pipeline_mode=pl.Buffered(1))
    tile = lambda s: jnp.minimum(s, n_tiles - 1)
    row = lambda n: pl.BlockSpec((tm, n), lambda s: (tile(s), 0))
    row_prev = lambda n: pl.BlockSpec((tm, n), lambda s: (jnp.maximum(s - 1, 0), 0))
    headed = lambda nh: pl.BlockSpec(
        (1, nh, tm, LANES), lambda s: (tile(s) // tiles_per_seq, 0, tile(s) % tiles_per_seq, 0))
    headed_t = lambda nh: pl.BlockSpec(
        (1, nh, LANES, tm), lambda s: (tile(s) // tiles_per_seq, 0, 0, tile(s) % tiles_per_seq))
    k_shape = jax.ShapeDtypeStruct((batch, N_KV_HEADS, seq, LANES), BF16)
    vt_shape = jax.ShapeDtypeStruct((batch, N_KV_HEADS, LANES, seq), BF16)
    out_shape = (
        jax.ShapeDtypeStruct((t, d), BF16),
        jax.ShapeDtypeStruct((batch, N_HEADS, LANES, seq), BF16),
        jax.ShapeDtypeStruct((t // D_STRIDE, D_STRIDE * LANES), BF16),
        jax.ShapeDtypeStruct((t // D_STRIDE, D_STRIDE * LANES), BF16),
        k_shape, vt_shape, k_shape, vt_shape,
        jax.ShapeDtypeStruct((t, N_KV_HEADS * LANES), F32),
        jax.ShapeDtypeStruct((t, d), BF16),
    )
    blocks = pl.BlockSpec((tm // D_STRIDE, D_STRIDE * LANES), lambda s: (tile(s), 0))
    out_specs = (row_prev(d), headed_t(N_HEADS), blocks, blocks,
                 headed(N_KV_HEADS), headed_t(N_KV_HEADS), headed(N_KV_HEADS),
                 headed_t(N_KV_HEADS), row(N_KV_HEADS * LANES), row(d))
    kern = functools.partial(_inproj_kernel, tm=tm, n_tiles=n_tiles, tiles_per_seq=tiles_per_seq,
                             d_conv=d_conv)
    consts = (vec(norm_g), wu, wq, wkv, wng, wmg, cw, vec(conv_b), vec(ln_g), vec(ln_b), w_conv_out)
    return pl.pallas_call(
        kern, out_shape=out_shape, grid=(n_tiles + 1,),
        in_specs=[row(d)] + [full(w) for w in consts],
        out_specs=out_specs,
        scratch_shapes=[pltpu.VMEM((2, tm, LANES), F32),
                        pltpu.VMEM((2, tm + CONV_HALO, d_conv), F32),
                        pltpu.VMEM((SUBLANES - 1, tm + CONV_HALO - SUBLANES, d_conv), F32),
                        pltpu.VMEM((2, tm, d), BF16)],
        compiler_params=_cparams(1), name="inproj",
    )(x2, *consts)


def _compress_kernel(rk_ref, rv_ref, pek_ref, pev_ref, w1k_ref, w1v_ref, w2k_ref, w2v_ref,
                     kcmp_ref, vcmp_ref, *, ncp):
    def one(r_ref, pe_ref, w1_ref, w2_ref):
        r = r_ref[0]
        top = _dot(r, w1_ref[0])
        bot = _dot(r, w1_ref[1])
        pe_h = _dot(pe_ref[0], w1_ref[0]) + _dot(pe_ref[1], w1_ref[1])
        nxt = pltpu.roll(bot, ncp - 1, 0)
        rowi = lax.broadcasted_iota(jnp.int32, top.shape, 0)
        hid = top + jnp.where(rowi == ncp - 1, 0.0, nxt) + pe_h[0:1, :]
        return _dot(_gelu(hid).astype(BF16), w2_ref[...])

    kcmp_ref[0] = one(rk_ref, pek_ref, w1k_ref, w2k_ref).astype(BF16)
    v = one(rv_ref, pev_ref, w1v_ref, w2v_ref)
    rowt = lax.broadcasted_iota(jnp.int32, (LANES, ncp), 0)
    ones_rows = (rowt >= HEAD_DIM) & (rowt < HEAD_DIM + 8)
    for g in range(N_KV_HEADS):
        vt = jnp.transpose(v[:, g * LANES:(g + 1) * LANES])
        vcmp_ref[0, g] = jnp.where(ones_rows, 1.0, vt).astype(BF16)


def _prep_compress_weights(pe, w1, w2):
    half = L_CMP // 2
    w1r = w1.reshape(L_CMP, HEAD_DIM, CMP_HIDDEN).astype(BF16)
    w2p = jnp.pad(w2, ((0, 0), (0, LANES - HEAD_DIM))).astype(BF16)

    def per_group(w, axis):
        z = jnp.zeros_like(w)
        return jnp.stack([jnp.concatenate([w if k == g else z for k in range(N_KV_HEADS)], axis=-1)
                          for g in range(N_KV_HEADS)], axis=axis)

    w1big = per_group(w1r, 1).reshape(2, half * N_KV_HEADS * HEAD_DIM, N_KV_HEADS * CMP_HIDDEN)
    w2big = per_group(w2p, 0).reshape(N_KV_HEADS * CMP_HIDDEN, N_KV_HEADS * LANES)
    per = pe.reshape(2, half, 1, HEAD_DIM)
    pebig = jnp.broadcast_to(per, (2, half, N_KV_HEADS, HEAD_DIM)).reshape(2, 1, -1)
    pebig = jnp.broadcast_to(pebig, (2, 8, pebig.shape[-1])).astype(BF16)
    return pebig, w1big, w2big


def _compress(kc, vc, wk, wv, batch, seq):
    ncp = seq // D_STRIDE
    width = D_STRIDE * LANES
    rk = kc.reshape(batch, ncp, width)
    rv = vc.reshape(batch, ncp, width)
    pek, w1k, w2k = wk
    pev, w1v, w2v = wv
    c3 = lambda a: pl.BlockSpec(a.shape, lambda b: (0, 0, 0))
    c2 = lambda a: pl.BlockSpec(a.shape, lambda b: (0, 0))
    rspec = pl.BlockSpec((1, ncp, width), lambda b: (b, 0, 0))
    kspec = pl.BlockSpec((1, ncp, N_KV_HEADS * LANES), lambda b: (b, 0, 0))
    kshape = jax.ShapeDtypeStruct((batch, ncp, N_KV_HEADS * LANES), BF16)
    vspec = pl.BlockSpec((1, N_KV_HEADS, LANES, ncp), lambda b: (b, 0, 0, 0))
    vshape = jax.ShapeDtypeStruct((batch, N_KV_HEADS, LANES, ncp), BF16)
    return pl.pallas_call(
        functools.partial(_compress_kernel, ncp=ncp), out_shape=(kshape, vshape), grid=(batch,),
        in_specs=[rspec, rspec, c3(pek), c3(pev), c3(w1k), c3(w1v), c2(w2k), c2(w2v)],
        out_specs=(kspec, vspec), compiler_params=_cparams(1), name="compress",
    )(rk, rv, pek, pev, w1k, w1v, w2k, w2v)


def _convmix_kernel(a_ref, halo_ref, ga_ref, cw_ref, cb_ref, lg_ref, lb_ref, wo_ref,
                    ya_ref, ext_ref, sh_ref, y_ref, *, tm):
    i = pl.program_id(1)
    halo = halo_ref[0].astype(F32)
    ext_ref[0:CONV_HALO, :] = jnp.where(i == 0, 0.0, halo)
    ext_ref[CONV_HALO:, :] = a_ref[0].astype(F32)
    span = sh_ref.shape[1]
    for ph in range(1, SUBLANES):
        sh_ref[ph - 1] = ext_ref[ph:ph + span, :]
    off = CONV_HALO - (CONV_WIDTH - 1)

    def rows_chunk(cidx, carry):
        r0 = pl.multiple_of(cidx * CONV_ROWS, CONV_ROWS)
        acc = jnp.zeros((CONV_ROWS, a_ref.shape[-1]), F32) + cb_ref[...]
        for j in range(CONV_WIDTH):
            ph = (off + j) % SUBLANES
            base = off + j - ph
            if ph == 0:
                xs = ext_ref[pl.ds(r0 + base, CONV_ROWS), :]
            else:
                xs = sh_ref[ph - 1, pl.ds(r0 + base, CONV_ROWS), :]
            acc = acc + jnp.tile(cw_ref[j], (CONV_ROWS // SUBLANES, 1)) * xs
        y_ref[pl.ds(r0, CONV_ROWS), :] = acc
        return carry

    lax.fori_loop(0, tm // CONV_ROWS, rows_chunk, 0)
    acc = y_ref[...]
    mu = jnp.mean(acc, axis=-1, keepdims=True)
    cen = acc - mu
    var = jnp.mean(cen * cen, axis=-1, keepdims=True)
    y = cen * lax.rsqrt(var + EPS) * lg_ref[...] + lb_ref[...]
    y = y * _sigmoid(y)
    yc = _dot(y.astype(BF16), wo_ref[...])
    ya_ref[...] = (ga_ref[...].astype(F32) * yc).astype(BF16)


def _convmix(a, mg, conv_w, conv_b, ln_g, ln_b, w_conv_out, batch, seq, tm=512):
    t, d_conv = a.shape
    d = w_conv_out.shape[1]
    a3 = a.reshape(batch, seq, d_conv)
    nt = seq // tm
    hb = tm // CONV_HALO
    cw = jnp.broadcast_to(conv_w[:, None, :], (CONV_WIDTH, SUBLANES, d_conv))
    vec = lambda v: v.reshape(1, -1)
    c2 = lambda a_: pl.BlockSpec(a_.shape, lambda b, i: (0,) * a_.ndim)
    return pl.pallas_call(
        functools.partial(_convmix_kernel, tm=tm),
        out_shape=jax.ShapeDtypeStruct((t, d), BF16), grid=(batch, nt),
        in_specs=[pl.BlockSpec((1, tm, d_conv), lambda b, i: (b, i, 0)),
                  pl.BlockSpec((1, CONV_HALO, d_conv),
                               lambda b, i: (b, jnp.maximum(i * hb - 1, 0), 0)),
                  pl.BlockSpec((tm, d), lambda b, i: (b * nt + i, 0)),
                  c2(cw), c2(vec(conv_b)), c2(vec(ln_g)), c2(vec(ln_b)), c2(w_conv_out)],
        out_specs=pl.BlockSpec((tm, d), lambda b, i: (b * nt + i, 0)),
        scratch_shapes=[pltpu.VMEM((tm + CONV_HALO, d_conv), F32),
                        pltpu.VMEM((SUBLANES - 1, tm + CONV_HALO - SUBLANES, d_conv), F32),
                        pltpu.VMEM((tm, d_conv), F32)],
        compiler_params=_cparams(2), name="convmix",
    )(a3, a3, mg, cw, vec(conv_b), vec(ln_g), vec(ln_b), w_conv_out)


def _nsa_kernel(qt_ref, kcmp_ref, vcmpt_ref, biasc_ref, ks_ref, vst_ref, kw_ref, vwt_ref,
                ng_ref, aggt_ref, bd_ref, o_ref,
                imp_ref, qa_ref, oc_ref, ms_ref, accs_ref, mw_ref, accw_ref, *, ns):
    i = pl.program_id(1)
    heads = range(N_HEADS)

    def k_tile(ref, g, j):
        return ref[0, g, pl.ds(pl.multiple_of(j * TQ, TQ), TQ), :]

    def vt_tile(ref, g, j):
        return ref[0, g, 0:PV_ROWS, pl.ds(pl.multiple_of(j * TQ, TQ), TQ)]

    def pipelined(tasks, scores, update):
        pending = [scores(t) for t in tasks[:PIPE_DEPTH]]
        for t, task in enumerate(tasks):
            update(task, pending.pop(0))
            if t + PIPE_DEPTH < len(tasks):
                pending.append(scores(tasks[t + PIPE_DEPTH]))

    def run(tasks):
        def scores(task):
            hh, kt, _, qmat, _, _, row0 = task
            s = _dot(kt, qmat)
            if row0 is None:
                return s
            if not isinstance(row0, int):
                row0 = pl.multiple_of(row0, TQ)
            return s + bd_ref[hh, pl.ds(row0, TQ), :]

        def update(task, s):
            hh, _, vt, _, m_ref, acc_ref, _ = task
            m_old = m_ref[hh]
            m_new = jnp.maximum(m_old, jnp.max(s, axis=0, keepdims=True))
            p = jnp.exp2(s - m_new).astype(BF16)
            acc_ref[hh] = jnp.exp2(m_old - m_new) * acc_ref[hh] + _dot(vt, p)
            m_ref[hh] = m_new

        pipelined(tasks, scores, update)

    def slc_tasks(j, row0=None):
        tiles = [(k_tile(ks_ref, g, j), vt_tile(vst_ref, g, j)) for g in range(N_KV_HEADS)]
        return [(hh,) + tiles[hh // HPG] + (qa_ref[hh], ms_ref, accs_ref, row0) for hh in heads]

    def win_tasks(j, row0):
        tiles = [(k_tile(kw_ref, g, j), vt_tile(vwt_ref, g, j)) for g in range(N_KV_HEADS)]
        return [(hh,) + tiles[hh // HPG] + (qt_ref[0, hh], mw_ref, accw_ref, row0) for hh in heads]

    for hh in heads:
        ms_ref[hh] = jnp.full((1, TQ), M_INIT, F32)
        mw_ref[hh] = jnp.full((1, TQ), M_INIT, F32)
        accs_ref[hh] = jnp.zeros((PV_ROWS, TQ), F32)
        accw_ref[hh] = jnp.zeros((PV_ROWS, TQ), F32)

    p_parts = []

    def cmp_scores(hh):
        g = hh // HPG
        return _dot(kcmp_ref[0, :, g * LANES:(g + 1) * LANES], qt_ref[0, hh]) + biasc_ref[hh]

    def cmp_update(hh, s_c):
        m_c = jnp.maximum(jnp.max(s_c, axis=0, keepdims=True), M_INIT)
        p_c = jnp.exp2(s_c - m_c)
        l_c = jnp.sum(p_c, axis=0, keepdims=True)
        p_c = p_c * jnp.where(l_c > 0.0, 1.0 / l_c, 0.0)
        oc_ref[hh] = _dot(vcmpt_ref[0, hh // HPG, 0:HEAD_DIM, :], p_c.astype(BF16))
        p_parts.append(p_c)

    pipelined(list(heads), cmp_scores, cmp_update)

    blk = lax.broadcasted_iota(jnp.int32, (ns, TQ), 0)
    cur = (i * TQ + lax.broadcasted_iota(jnp.int32, (ns, TQ), 1)) // L_SLC
    forced = (blk == 0) | (blk == cur) | (blk == cur - 1)
    for g in range(N_KV_HEADS):
        p4 = p_parts[g * HPG:(g + 1) * HPG]
        p_sum = (p4[0] + p4[1]) + (p4[2] + p4[3])
        p_hi = p_sum.astype(BF16)
        p_lo = (p_sum - p_hi.astype(F32)).astype(BF16)
        imp = _dot(aggt_ref[...], p_hi) + _dot(aggt_ref[...], p_lo)
        imp_ref[g] = jnp.where(forced, FORCE, jnp.where(blk <= cur, imp, -FORCE))

    rows2 = jnp.where(i >= 2, 0, 3 * TQ)
    rows1 = jnp.where(i >= 1, TQ, 3 * TQ)
    run(win_tasks(jnp.maximum(i - 2, 0), rows2) + win_tasks(jnp.maximum(i - 1, 0), rows1)
        + win_tasks(i, 2 * TQ))

    n_grp = ns // SUBLANES
    blk8 = lax.broadcasted_iota(jnp.int32, (SUBLANES, TQ), 0)
    for g in range(N_KV_HEADS):
        grp = [imp_ref[g, SUBLANES * r:SUBLANES * (r + 1), :] for r in range(n_grp)]
        rank = [jnp.zeros((SUBLANES, TQ), F32) for _ in range(n_grp)]
        for k in range(ns):
            rowk = imp_ref[g, k:k + 1, :]
            kg = k // SUBLANES
            for r in range(n_grp):
                if r < kg:
                    beats = jnp.where(rowk > grp[r], 1.0, 0.0)
                elif r > kg:
                    beats = jnp.where(rowk >= grp[r], 1.0, 0.0)
                else:
                    beats = jnp.where(blk8 > k - SUBLANES * kg,
                                      jnp.where(rowk >= grp[r], 1.0, 0.0),
                                      jnp.where(rowk > grp[r], 1.0, 0.0))
                rank[r] = rank[r] + beats
        rank = jnp.concatenate(rank, axis=0)
        flags = jnp.where(rank < float(N_SEL), 0.0, -1.0).astype(BF16)
        if ns < HEAD_DIM:
            flags = jnp.concatenate([flags, jnp.zeros((HEAD_DIM - ns, TQ), BF16)], axis=0)
        for hh in range(g * HPG, (g + 1) * HPG):
            qa_ref[hh, 0:HEAD_DIM, :] = qt_ref[0, hh, 0:HEAD_DIM, :]
            qa_ref[hh, HEAD_DIM:, :] = flags

    n_far = jnp.maximum(i - 1, 0)

    def far_group(jj, carry):
        tasks = []
        for u in range(FAR_GROUP):
            tasks += slc_tasks(FAR_GROUP * jj + u)
        run(tasks)
        return carry

    lax.fori_loop(0, n_far // FAR_GROUP, far_group, 0)
    for rem in range(FAR_GROUP):
        @pl.when(n_far % FAR_GROUP == rem)
        def _(rem=rem):
            tasks = []
            for u in range(rem):
                tasks += slc_tasks(n_far - rem + u)
            run(tasks + slc_tasks(jnp.maximum(i - 1, 0), rows1) + slc_tasks(i, 2 * TQ))

    gates_t = jnp.transpose(ng_ref[...])
    outs = []
    for hh in heads:
        a_s, a_w = accs_ref[hh], accw_ref[hh]
        o_s = a_s[0:HEAD_DIM] * (1.0 / a_s[HEAD_DIM:HEAD_DIM + 1])
        o_w = a_w[0:HEAD_DIM] * (1.0 / a_w[HEAD_DIM:HEAD_DIM + 1])
        row = (hh // HPG) * LANES + 3 * (hh % HPG)
        outs.append(gates_t[row:row + 1] * oc_ref[hh] + gates_t[row + 1:row + 2] * o_s
                    + gates_t[row + 2:row + 3] * o_w)
    o_ref[...] = jnp.transpose(jnp.concatenate(outs, axis=0)).astype(BF16)


def _bias_of(thr_ref, rb_ref, dist, h):
    b = jnp.full(dist.shape, rb_ref[h], F32)
    for k in range(1, N_BUCKETS):
        b = jnp.where(dist >= thr_ref[k], rb_ref[k * N_HEADS + h], b)
    return b


def _cmp_bias_kernel(thr_ref, rb_ref, biasc_ref, *, nc, ncp):
    i = pl.program_id(0)
    per_tile = TQ // D_STRIDE
    band = 2 * per_tile
    assert TQ + D_STRIDE - (L_CMP - 1) >= MAX_DISTANCE

    def dist_of(c0, rows):
        c = c0 + lax.broadcasted_iota(jnp.int32, (rows, TQ), 0)
        r = lax.broadcasted_iota(jnp.int32, (rows, TQ), 1)
        dist = i * TQ + r - (c * D_STRIDE + L_CMP - 1)
        return dist, (dist >= 0) & (c < nc)

    _, ok_all = dist_of(0, ncp)
    band0 = pl.multiple_of(jnp.maximum(i * per_tile - per_tile, 0), per_tile)
    dist_b, ok_b = dist_of(band0, band)
    for h in range(N_HEADS):
        biasc_ref[h] = jnp.where(ok_all, rb_ref[(N_BUCKETS - 1) * N_HEADS + h], NEG)
        biasc_ref[h, pl.ds(band0, band), :] = jnp.where(
            ok_b, _bias_of(thr_ref, rb_ref, dist_b, h), NEG)


def _diag_bias_kernel(thr_ref, rb_ref, bd_ref):
    d0 = (lax.broadcasted_iota(jnp.int32, (TQ, TQ), 1)
          - lax.broadcasted_iota(jnp.int32, (TQ, TQ), 0))
    for h in range(N_HEADS):
        far = rb_ref[(N_BUCKETS - 1) * N_HEADS + h]
        bd_ref[h, 0:TQ, :] = jnp.where(d0 < 0, _bias_of(thr_ref, rb_ref, d0 + 2 * TQ, h) - far, NEG)
        bd_ref[h, TQ:2 * TQ, :] = _bias_of(thr_ref, rb_ref, d0 + TQ, h) - far
        bd_ref[h, 2 * TQ:3 * TQ, :] = jnp.where(d0 >= 0, _bias_of(thr_ref, rb_ref, d0, h) - far, NEG)
        bd_ref[h, 3 * TQ:, :] = jnp.full((TQ, TQ), NEG, F32)


def _attention_tables(rel_bias, seq):
    ncp = seq // D_STRIDE
    nc = (seq - L_CMP) // D_STRIDE + 1
    ns = seq // L_SLC
    nt = seq // TQ
    n_probe = 2 * MAX_DISTANCE
    buckets = _t5_bucket(jnp.arange(n_probe))
    thr = jnp.sum(buckets[None, :] < jnp.arange(N_BUCKETS)[:, None], axis=1).astype(jnp.int32)
    assert WINDOW == 2 * TQ
    rb = (rel_bias.astype(F32) * LOG2E).reshape(-1)
    bias_c = pl.pallas_call(
        functools.partial(_cmp_bias_kernel, nc=nc, ncp=ncp),
        out_shape=jax.ShapeDtypeStruct((N_HEADS, ncp, seq), F32),
        grid_spec=pltpu.PrefetchScalarGridSpec(
            num_scalar_prefetch=2, grid=(nt,), in_specs=[],
            out_specs=pl.BlockSpec((N_HEADS, ncp, TQ), lambda i, *_: (0, 0, i))),
        compiler_params=_cparams(1), name="cmp_bias")(thr, rb)
    bd = pl.pallas_call(
        _diag_bias_kernel, out_shape=jax.ShapeDtypeStruct((N_HEADS, 4 * TQ, TQ), F32),
        grid_spec=pltpu.PrefetchScalarGridSpec(
            num_scalar_prefetch=2, grid=(1,), in_specs=[],
            out_specs=pl.BlockSpec((N_HEADS, 4 * TQ, TQ), lambda i, *_: (0, 0, 0))),
        compiler_params=_cparams(1), name="diag_bias")(thr, rb)
    c_start = jnp.arange(ncp) * D_STRIDE
    c_end = c_start + L_CMP - 1
    s_start = jnp.arange(ns) * L_SLC
    aggt = ((c_end[None, :] >= s_start[:, None]) & (c_start[None, :] <= s_start[:, None] + L_SLC - 1)
            & (jnp.arange(ncp)[None, :] < nc)).astype(BF16)
    return bias_c, bd, aggt


def _nsa(qt, kcmp, vcmpt, ks, vst, kw, vwt, ng, rel_bias, batch, seq):
    first_far = TQ + 1
    assert MAX_EXACT + int(math.log(first_far / MAX_EXACT) / math.log(MAX_DISTANCE / MAX_EXACT)
                           * (N_BUCKETS - MAX_EXACT)) >= N_BUCKETS - 1, "far tiles need one bucket"
    ncp = seq // D_STRIDE
    ns = seq // L_SLC
    nt = seq // TQ
    bias_c, bd, aggt = _attention_tables(rel_bias, seq)
    kspec = pl.BlockSpec((1, N_KV_HEADS, seq, LANES), lambda b, i: (b, 0, 0, 0))
    vtspec = pl.BlockSpec((1, N_KV_HEADS, LANES, seq), lambda b, i: (b, 0, 0, 0))
    state = lambda rows: pltpu.VMEM((N_HEADS, rows, TQ), F32)
    const = lambda shape: pl.BlockSpec(shape, lambda b, i: (0,) * len(shape),
                                       pipeline_mode=pl.Buffered(1))
    grid_spec = pltpu.PrefetchScalarGridSpec(
        num_scalar_prefetch=0, grid=(batch, nt),
        in_specs=[pl.BlockSpec((1, N_HEADS, LANES, TQ), lambda b, i: (b, 0, 0, i)),
                  pl.BlockSpec((1, ncp, N_KV_HEADS * LANES), lambda b, i: (b, 0, 0)),
                  pl.BlockSpec((1, N_KV_HEADS, LANES, ncp), lambda b, i: (b, 0, 0, 0)),
                  pl.BlockSpec((N_HEADS, ncp, TQ), lambda b, i: (0, 0, i)),
                  kspec, vtspec, kspec, vtspec,
                  pl.BlockSpec((TQ, N_KV_HEADS * LANES), lambda b, i: (b * nt + i, 0)),
                  const((ns, ncp)), const((N_HEADS, 4 * TQ, TQ))],
        out_specs=pl.BlockSpec((TQ, N_HEADS * HEAD_DIM), lambda b, i: (b * nt + i, 0)),
        scratch_shapes=[pltpu.VMEM((N_KV_HEADS, ns, TQ), F32),
                        pltpu.VMEM((N_HEADS, LANES, TQ), BF16),
                        state(HEAD_DIM),
                        state(1), state(PV_ROWS),
                        state(1), state(PV_ROWS)])
    return pl.pallas_call(
        functools.partial(_nsa_kernel, ns=ns),
        out_shape=jax.ShapeDtypeStruct((batch * seq, N_HEADS * HEAD_DIM), BF16),
        grid_spec=grid_spec, compiler_params=_cparams(2), name="nsa",
    )(qt, kcmp, vcmpt, bias_c, ks, vst, kw, vwt, ng, aggt, bd)


def _ffn_kernel(x_ref, ya_ref, o_ref, gb_ref, p_ref, wao_ref, wout_ref, nf_ref, wup_ref, cw_ref,
                cb_ref, wd_ref, np_ref, wpg_ref, wpe_ref, nfin_ref, out_ref, carry_ref,
                *, tm, tiles_per_seq, d_ff, chunks):
    r = pl.program_id(0)

    @pl.when((r % tiles_per_seq) == 0)
    def _():
        carry_ref[...] = jnp.zeros_like(carry_ref)

    n = tm // FFN_SPLIT
    groups = [slice(k * n, (k + 1) * n) for k in range(FFN_SPLIT)]

    x1, hf = [], []
    for rows in groups:
        y = (ya_ref[rows, :].astype(F32)
             + gb_ref[rows, :].astype(F32) * _dot(o_ref[rows, :], wao_ref[...]))
        x1.append(x_ref[rows, :] + _dot(y.astype(BF16), wout_ref[...]))
        hf.append(_rms(x1[-1], nf_ref[...]).astype(BF16))

    def conv3(up, prev, c0, width):
        rowi = lax.broadcasted_iota(jnp.int32, up.shape, 0)
        s1 = jnp.where(rowi == 0, prev[7:8, :], pltpu.roll(up, 1, 0))
        s2 = jnp.where(rowi == 0, prev[6:7, :],
                       jnp.where(rowi == 1, prev[7:8, :], pltpu.roll(up, 2, 0)))
        return (cw_ref[0:1, c0:c0 + width] * s2 + cw_ref[1:2, c0:c0 + width] * s1
                + cw_ref[2:3, c0:c0 + width] * up + cb_ref[:, c0:c0 + width])

    acc = [None] * FFN_SPLIT
    for c0, width in chunks:
        cols = (slice(c0, c0 + width), slice(d_ff + c0, d_ff + c0 + width))
        prev = [carry_ref[:, cs] for cs in cols]
        for k in range(FFN_SPLIT):
            ups = [_dot(hf[k], wup_ref[:, cs]) for cs in cols]
            gate, val = [conv3(ups[t], prev[t], cols[t].start, width) for t in range(2)]
            prev = [u[n - 8:, :] for u in ups]
            part = _dot((_gelu(gate) * val).astype(BF16), wd_ref[c0:c0 + width, :])
            acc[k] = part if acc[k] is None else acc[k] + part
        for t in range(2):
            carry_ref[:, cols[t]] = prev[t]

    for k, rows in enumerate(groups):
        x2 = x1[k] + acc[k]
        pg = _sigmoid(_dot(_rms(x2, np_ref[...]).astype(BF16), wpg_ref[...]))
        pe = _dot(p_ref[rows, :].astype(BF16), wpe_ref[...])
        out_ref[rows, :] = _rms(x2 + pg * pe, nfin_ref[...])


def _ffn(x2, ya, o, mg, p2, w_attn_out, w_out, norm_ffn, w_up, ffn_dw_w, ffn_dw_b, w_down,
         norm_ple, w_ple_gate, w_ple, norm_final, seq, tm=512):
    t, d = x2.shape
    d_ff = w_down.shape[0]
    assert d_ff % MXU_DEPTH == 0
    chunks, c0 = [], 0
    while c0 < d_ff:
        width = min(FFN_CHUNK, d_ff - c0)
        chunks.append((c0, width))
        c0 += width
    tiles_per_seq = seq // tm
    cw = jnp.concatenate([ffn_dw_w, jnp.zeros((8 - FFN_CONV_WIDTH, 2 * d_ff), ffn_dw_w.dtype)], 0)
    vec = lambda v: v.reshape(1, -1)
    const = lambda a: pl.BlockSpec(a.shape, lambda r: (0, 0), pipeline_mode=pl.Buffered(1))
    rows = lambda n: pl.BlockSpec((tm, n), lambda r: (r, 0))
    kern = functools.partial(_ffn_kernel, tm=tm, tiles_per_seq=tiles_per_seq, d_ff=d_ff,
                             chunks=tuple(chunks))
    operands = (x2, ya, o, mg, p2, w_attn_out, w_out, vec(norm_ffn), w_up, cw, vec(ffn_dw_b),
                w_down, vec(norm_ple), w_ple_gate, w_ple, vec(norm_final))
    return pl.pallas_call(
        kern, out_shape=jax.ShapeDtypeStruct((t, d), F32), grid=(t // tm,),
        in_specs=[rows(d), rows(d), rows(o.shape[1]), rows(d), rows(p2.shape[1])]
        + [const(a) for a in operands[5:]],
        out_specs=rows(d),
        scratch_shapes=[pltpu.VMEM((8, 2 * d_ff), F32)],
        compiler_params=_cparams(1), name="ffn",
    )(*operands)


def kernel(x, p, rel_bias, norm_mix, w_in, conv_dw_w, conv_dw_b, conv_ln_g, conv_ln_b, w_conv_out,
           cmp_pe_k, cmp_pe_v, w_ck1, w_ck2, w_cv1, w_cv2, w_attn_out, w_out, norm_ffn, w_up,
           ffn_dw_w, ffn_dw_b, w_down, norm_ple, w_ple_gate, w_ple, norm_final):
    batch, seq, d = x.shape
    depth = w_in.shape[0]
    x2 = x.reshape(batch * seq, d)
    for i in range(depth):
        ya, qt, kc, vc, ks, vst, kw, vwt, ng, gb = _inproj(
            x2, norm_mix[i], _prep_inproj_weights(w_in[i], d), conv_dw_w[i], conv_dw_b[i],
            conv_ln_g[i], conv_ln_b[i], w_conv_out[i].astype(BF16), batch, seq)
        kcmp, vcmpt = _compress(kc, vc,
                               _prep_compress_weights(cmp_pe_k[i], w_ck1[i], w_ck2[i]),
                               _prep_compress_weights(cmp_pe_v[i], w_cv1[i], w_cv2[i]),
                               batch, seq)
        o = _nsa(qt, kcmp, vcmpt, ks, vst, kw, vwt, ng, rel_bias, batch, seq)
        assert i == depth - 1, "the final RMSNorm is fused into the (single) layer's MLP kernel"
        x2 = _ffn(x2, ya, o, gb, p[i].reshape(batch * seq, -1), w_attn_out[i].astype(BF16),
                  w_out[i].astype(BF16), norm_ffn[i], w_up[i].astype(BF16), ffn_dw_w[i],
                  ffn_dw_b[i], w_down[i].astype(BF16), norm_ple[i], w_ple_gate[i].astype(BF16),
                  w_ple[i].astype(BF16), norm_final, seq)
    return x2.reshape(batch, seq, d)
```

```python
import functools
import math

import jax
import jax.numpy as jnp
import numpy as np
from jax import lax
from jax.experimental import pallas as pl
from jax.experimental.pallas import tpu as pltpu

N_HEADS = 8
HEAD_DIM = 64
N_KV_HEADS = 2
HPG = N_HEADS // N_KV_HEADS
L_CMP = 32
D_STRIDE = 16
CMP_HIDDEN = 256
L_SLC = 64
N_SEL = 16
WINDOW = 512
N_BUCKETS = 32
MAX_EXACT = N_BUCKETS // 2
MAX_DISTANCE = 128
CONV_WIDTH = 31
FFN_CONV_WIDTH = 3
EPS = 1e-6
FORCE = 1e4

LANES = 128
MXU_DEPTH = 256
FFN_CHUNK = 6 * MXU_DEPTH
FFN_SPLIT = 2
NEG = -1e30
M_INIT = -1e29
LOG2E = 1.4426950408889634
PIPE_DEPTH = 6
FAR_GROUP = 4
TQ = 256
PV_ROWS = HEAD_DIM + 16
SUBLANES = 8
CONV_HALO = 32
CONV_ROWS = 64
VMEM_LIMIT = 56 * 1024 * 1024

F32 = jnp.float32
BF16 = jnp.bfloat16


def _cparams(n_axes):
    return pltpu.CompilerParams(dimension_semantics=("arbitrary",) * n_axes,
                                vmem_limit_bytes=VMEM_LIMIT)


def _dot(a, b):
    return jnp.dot(a, b, preferred_element_type=F32)


def _rms(xf, g):
    return xf * lax.rsqrt(jnp.mean(xf * xf, axis=-1, keepdims=True) + EPS) * g


def _sigmoid(x):
    return 1.0 / (1.0 + jnp.exp(-x))


def _gelu(x):
    return 0.5 * x * (1.0 + jnp.tanh(0.7978845608028654 * (x + 0.044715 * x * x * x)))


def _t5_bucket(dist):
    n = jnp.maximum(dist, 0)
    nf = jnp.maximum(n, MAX_EXACT).astype(F32)
    large = MAX_EXACT + (jnp.log(nf / MAX_EXACT) / math.log(MAX_DISTANCE / MAX_EXACT)
                         * (N_BUCKETS - MAX_EXACT)).astype(jnp.int32)
    large = jnp.minimum(large, N_BUCKETS - 1)
    return jnp.where(n < MAX_EXACT, n, large)


def _inproj_kernel(x_ref, g_ref, wu_ref, wq_ref, wkv_ref, wng_ref, wmg_ref,
                   cw_ref, cb_ref, lg_ref, lb_ref, wco_ref,
                   ya_ref, qt_ref, kc_ref, vc_ref, ks_ref, vst_ref, kw_ref, vwt_ref, ng_ref, gb_ref,
                   cs_ref, ext_ref, sh_ref, ga_ref, a_stage_ref, ga_stage_ref,
                   *, tm, n_tiles, tiles_per_seq, d_conv):
    step = pl.program_id(0)
    r = jnp.minimum(step, n_tiles - 1)

    @pl.when(step == 0)
    def _():
        ext_ref[...] = jnp.zeros(ext_ref.shape, F32)
        ga_ref[...] = jnp.zeros(ga_ref.shape, BF16)

    span = sh_ref.shape[1]
    for ph in range(1, SUBLANES):
        sh_ref[ph - 1] = ext_ref[ph:ph + span, :]
    off = CONV_HALO - (CONV_WIDTH - 1)
    conv = []
    for r0 in range(0, tm, CONV_ROWS):
        acc = jnp.zeros((CONV_ROWS, d_conv), F32) + cb_ref[...]
        for j in range(CONV_WIDTH):
            ph = (off + j) % SUBLANES
            base = r0 + off + j - ph
            if ph == 0:
                xs = ext_ref[base:base + CONV_ROWS, :]
            else:
                xs = sh_ref[ph - 1, base:base + CONV_ROWS, :]
            acc = acc + jnp.tile(cw_ref[j], (CONV_ROWS // SUBLANES, 1)) * xs
        conv.append(acc)
    conv = jnp.concatenate(conv, axis=0)
    mu = jnp.mean(conv, axis=-1, keepdims=True)
    cen = conv - mu
    var = jnp.mean(cen * cen, axis=-1, keepdims=True)
    y = cen * lax.rsqrt(var + EPS) * lg_ref[...] + lb_ref[...]
    y = (y * _sigmoid(y)).astype(BF16)

    h = _rms(x_ref[...], g_ref[...]).astype(BF16)
    u = _dot(h, wu_ref[...])
    a_stage_ref[...] = u[:, :d_conv] * _sigmoid(u[:, d_conv:])
    zmg = _dot(h, wmg_ref[...])
    d_model = zmg.shape[1] // 2
    ga_stage_ref[...] = _sigmoid(zmg[:, :d_model]).astype(BF16)
    gb_ref[...] = _sigmoid(zmg[:, d_model:]).astype(BF16)

    zqt = jnp.transpose(_dot(h, wq_ref[...]))
    zero_rows = jnp.zeros((HEAD_DIM, tm), BF16)
    for hh in range(N_HEADS):
        qt_ref[0, hh, 0:HEAD_DIM, :] = zqt[hh * HEAD_DIM:(hh + 1) * HEAD_DIM].astype(BF16)
        qt_ref[0, hh, HEAD_DIM:, :] = zero_rows

    zkv = _dot(h, wkv_ref[...])
    for which, out_ref in enumerate((kc_ref, vc_ref)):
        cs_ref[which] = zkv[:, which * LANES:(which + 1) * LANES]
        for tok in range(D_STRIDE):
            rows = cs_ref[which, pl.ds(tok, tm // D_STRIDE, stride=D_STRIDE), :]
            out_ref[:, tok * LANES:(tok + 1) * LANES] = rows.astype(BF16)
    lane = lax.broadcasted_iota(jnp.int32, (tm, LANES), 1)
    row = lax.broadcasted_iota(jnp.int32, (tm, LANES), 0)
    spos = (r % tiles_per_seq) * tm + row
    lo = lane < HEAD_DIM
    blk_tag = jnp.where((lane - HEAD_DIM) == spos // L_SLC, -NEG, 0.0)
    ones_rows = jnp.where(lax.broadcasted_iota(jnp.int32, (HEAD_DIM, tm), 0) < 8, 1.0, 0.0)
    ones_rows = ones_rows.astype(BF16)
    for g in range(N_KV_HEADS):
        for k_ref, vt_ref, base, tag in ((ks_ref, vst_ref, (2 + g) * LANES, blk_tag),
                                         (kw_ref, vwt_ref, (4 + g) * LANES, 0.0)):
            pair = zkv[:, base:base + LANES]
            k_ref[0, g] = jnp.where(lo, pair, tag).astype(BF16)
            vt_ref[0, g, 0:HEAD_DIM, :] = jnp.transpose(pair)[HEAD_DIM:].astype(BF16)
            vt_ref[0, g, HEAD_DIM:, :] = ones_rows

    ng_ref[...] = _sigmoid(_dot(h, wng_ref[...]))

    ya_ref[...] = (ga_ref[...].astype(F32) * _dot(y, wco_ref[...])).astype(BF16)

    seq_start = (r % tiles_per_seq) == 0
    ext_ref[0:CONV_HALO, :] = jnp.where(seq_start, 0.0, ext_ref[tm:tm + CONV_HALO, :])
    ext_ref[CONV_HALO:, :] = a_stage_ref[...]
    ga_ref[...] = ga_stage_ref[...]


def _prep_inproj_weights(w_in, d_model):
    d_conv = d_model // 2
    n_conv = 2 * d_conv
    n_q = N_HEADS * HEAD_DIM
    n_kv = 6 * N_KV_HEADS * HEAD_DIM
    n_ng = 3 * N_HEADS
    o = 0
    wu = w_in[:, o:o + n_conv]; o += n_conv
    wq = w_in[:, o:o + n_q]; o += n_q
    wkv = w_in[:, o:o + n_kv]; o += n_kv
    wng = w_in[:, o:o + n_ng]; o += n_ng
    wmg = w_in[:, o:]
    wq_p = wq * (LOG2E / math.sqrt(HEAD_DIM))
    kvcols = [wkv[:, 0:LANES], wkv[:, LANES:2 * LANES]]
    for k_kind in (2, 4):
        for g in range(N_KV_HEADS):
            for kind in (k_kind, k_kind + 1):
                c0 = kind * N_KV_HEADS * HEAD_DIM + g * HEAD_DIM
                kvcols.append(wkv[:, c0:c0 + HEAD_DIM])
    wkv_p = jnp.concatenate(kvcols, axis=1)
    per_g = HPG * 3
    ngcols = []
    for g in range(N_KV_HEADS):
        ngcols += [wng[:, g * per_g:(g + 1) * per_g],
                   jnp.zeros((d_model, LANES - per_g), w_in.dtype)]
    wng_p = jnp.concatenate(ngcols, axis=1)
    return tuple(w.astype(BF16) for w in (wu, wq_p, wkv_p, wng_p, wmg))


def _inproj(x2, norm_g, weights, conv_w, conv_b, ln_g, ln_b, w_conv_out, batch, seq, tm=512):
    t, d = x2.shape
    wu, wq, wkv, wng, wmg = weights
    d_conv = d // 2
    tiles_per_seq = seq // tm
    n_tiles = t // tm
    assert tm % CONV_ROWS == 0 and CONV_HALO >= CONV_WIDTH - 1 and CONV_HALO % SUBLANES == 0
    cw = jnp.broadcast_to(conv_w[:, None, :], (CONV_WIDTH, SUBLANES, d_conv))
    vec = lambda v: v.reshape(1, -1)
    full = lambda w: pl.BlockSpec(w.shape, lambda s: (0,) * w.ndim, pipeline_mode=pl.Buffered(1))
    tile = lambda s: jnp.minimum(s, n_tiles - 1)
    row = lambda n: pl.BlockSpec((tm, n), lambda s: (tile(s), 0))
    row_prev = lambda n: pl.BlockSpec((tm, n), lambda s: (jnp.maximum(s - 1, 0), 0))
    headed = lambda nh: pl.BlockSpec(
        (1, nh, tm, LANES), lambda s: (tile(s) // tiles_per_seq, 0, tile(s) % tiles_per_seq, 0))
    headed_t = lambda nh: pl.BlockSpec(
        (1, nh, LANES, tm), lambda s: (tile(s) // tiles_per_seq, 0, 0, tile(s) % tiles_per_seq))
    k_shape = jax.ShapeDtypeStruct((batch, N_KV_HEADS, seq, LANES), BF16)
    vt_shape = jax.ShapeDtypeStruct((batch, N_KV_HEADS, LANES, seq), BF16)
    out_shape = (
        jax.ShapeDtypeStruct((t, d), BF16),
        jax.ShapeDtypeStruct((batch, N_HEADS, LANES, seq), BF16),
        jax.ShapeDtypeStruct((t // D_STRIDE, D_STRIDE * LANES), BF16),
        jax.ShapeDtypeStruct((t // D_STRIDE, D_STRIDE * LANES), BF16),
        k_shape, vt_shape, k_shape, vt_shape,
        jax.ShapeDtypeStruct((t, N_KV_HEADS * LANES), F32),
        jax.ShapeDtypeStruct((t, d), BF16),
    )
    blocks = pl.BlockSpec((tm // D_STRIDE, D_STRIDE * LANES), lambda s: (tile(s), 0))
    out_specs = (row_prev(d), headed_t(N_HEADS), blocks, blocks,
                 headed(N_KV_HEADS), headed_t(N_KV_HEADS), headed(N_KV_HEADS),
                 headed_t(N_KV_HEADS), row(N_KV_HEADS * LANES), row(d))
    kern = functools.partial(_inproj_kernel, tm=tm, n_tiles=n_tiles, tiles_per_seq=tiles_per_seq,
                             d_conv=d_conv)
    consts = (vec(norm_g), wu, wq, wkv, wng, wmg, cw, vec(conv_b), vec(ln_g), vec(ln_b), w_conv_out)
    return pl.pallas_call(
        kern, out_shape=out_shape, grid=(n_tiles + 1,),
        in_specs=[row(d)] + [full(w) for w in consts],
        out_specs=out_specs,
        scratch_shapes=[pltpu.VMEM((2, tm, LANES), F32),
                        pltpu.VMEM((tm + CONV_HALO, d_conv), F32),
                        pltpu.VMEM((SUBLANES - 1, tm + CONV_HALO - SUBLANES, d_conv), F32),
                        pltpu.VMEM((tm, d), BF16),
                        pltpu.VMEM((tm, d_conv), F32), pltpu.VMEM((tm, d), BF16)],
        compiler_params=_cparams(1), name="inproj",
    )(x2, *consts)


def _compress_kernel(rk_ref, rv_ref, pek_ref, pev_ref, w1k_ref, w1v_ref, w2k_ref, w2v_ref,
                     kcmp_ref, vcmp_ref, *, ncp):
    def one(r_ref, pe_ref, w1_ref, w2_ref):
        r = r_ref[0]
        top = _dot(r, w1_ref[0])
        bot = _dot(r, w1_ref[1])
        pe_h = _dot(pe_ref[0], w1_ref[0]) + _dot(pe_ref[1], w1_ref[1])
        nxt = pltpu.roll(bot, ncp - 1, 0)
        rowi = lax.broadcasted_iota(jnp.int32, top.shape, 0)
        hid = top + jnp.where(rowi == ncp - 1, 0.0, nxt) + pe_h[0:1, :]
        return _dot(_gelu(hid).astype(BF16), w2_ref[...])

    kcmp_ref[0] = one(rk_ref, pek_ref, w1k_ref, w2k_ref).astype(BF16)
    v = one(rv_ref, pev_ref, w1v_ref, w2v_ref)
    rowt = lax.broadcasted_iota(jnp.int32, (LANES, ncp), 0)
    ones_rows = (rowt >= HEAD_DIM) & (rowt < HEAD_DIM + 8)
    for g in range(N_KV_HEADS):
        vt = jnp.transpose(v[:, g * LANES:(g + 1) * LANES])
        vcmp_ref[0, g] = jnp.where(ones_rows, 1.0, vt).astype(BF16)


def _prep_compress_weights(pe, w1, w2):
    half = L_CMP // 2
    w1r = w1.reshape(L_CMP, HEAD_DIM, CMP_HIDDEN).astype(BF16)
    w2p = jnp.pad(w2, ((0, 0), (0, LANES - HEAD_DIM))).astype(BF16)

    def per_group(w, axis):
        z = jnp.zeros_like(w)
        return jnp.stack([jnp.concatenate([w if k == g else z for k in range(N_KV_HEADS)], axis=-1)
                          for g in range(N_KV_HEADS)], axis=axis)

    w1big = per_group(w1r, 1).reshape(2, half * N_KV_HEADS * HEAD_DIM, N_KV_HEADS * CMP_HIDDEN)
    w2big = per_group(w2p, 0).reshape(N_KV_HEADS * CMP_HIDDEN, N_KV_HEADS * LANES)
    per = pe.reshape(2, half, 1, HEAD_DIM)
    pebig = jnp.broadcast_to(per, (2, half, N_KV_HEADS, HEAD_DIM)).reshape(2, 1, -1)
    pebig = jnp.broadcast_to(pebig, (2, 8, pebig.shape[-1])).astype(BF16)
    return pebig, w1big, w2big


def _compress(kc, vc, wk, wv, batch, seq):
    ncp = seq // D_STRIDE
    width = D_STRIDE * LANES
    rk = kc.reshape(batch, ncp, width)
    rv = vc.reshape(batch, ncp, width)
    pek, w1k, w2k = wk
    pev, w1v, w2v = wv
    c3 = lambda a: pl.BlockSpec(a.shape, lambda b: (0, 0, 0))
    c2 = lambda a: pl.BlockSpec(a.shape, lambda b: (0, 0))
    rspec = pl.BlockSpec((1, ncp, width), lambda b: (b, 0, 0))
    kspec = pl.BlockSpec((1, ncp, N_KV_HEADS * LANES), lambda b: (b, 0, 0))
    kshape = jax.ShapeDtypeStruct((batch, ncp, N_KV_HEADS * LANES), BF16)
    vspec = pl.BlockSpec((1, N_KV_HEADS, LANES, ncp), lambda b: (b, 0, 0, 0))
    vshape = jax.ShapeDtypeStruct((batch, N_KV_HEADS, LANES, ncp), BF16)
    return pl.pallas_call(
        functools.partial(_compress_kernel, ncp=ncp), out_shape=(kshape, vshape), grid=(batch,),
        in_specs=[rspec, rspec, c3(pek), c3(pev), c3(w1k), c3(w1v), c2(w2k), c2(w2v)],
        out_specs=(kspec, vspec), compiler_params=_cparams(1), name="compress",
    )(rk, rv, pek, pev, w1k, w1v, w2k, w2v)


def _nsa_kernel(qt_ref, kcmp_ref, vcmpt_ref, biasc_ref, ks_ref, vst_ref, kw_ref, vwt_ref,
                ng_ref, aggt_ref, bd_ref, o_ref,
                imp_ref, qa_ref, oc_ref, ms_ref, accs_ref, mw_ref, accw_ref, *, ns):
    i = pl.program_id(1)
    heads = range(N_HEADS)

    def k_tile(ref, g, j):
        return ref[0, g, pl.ds(pl.multiple_of(j * TQ, TQ), TQ), :]

    def vt_tile(ref, g, j):
        return ref[0, g, 0:PV_ROWS, pl.ds(pl.multiple_of(j * TQ, TQ), TQ)]

    def pipelined(tasks, scores, update):
        pending = [scores(t) for t in tasks[:PIPE_DEPTH]]
        for t, task in enumerate(tasks):
            update(task, pending.pop(0))
            if t + PIPE_DEPTH < len(tasks):
                pending.append(scores(tasks[t + PIPE_DEPTH]))

    def run(tasks):
        def scores(task):
            hh, kt, _, qmat, _, _, row0 = task
            s = _dot(kt, qmat)
            if row0 is None:
                return s
            if not isinstance(row0, int):
                row0 = pl.multiple_of(row0, TQ)
            return s + bd_ref[hh, pl.ds(row0, TQ), :]

        def update(task, s):
            hh, _, vt, _, m_ref, acc_ref, _ = task
            m_old = m_ref[hh]
            m_new = jnp.maximum(m_old, jnp.max(s, axis=0, keepdims=True))
            p = jnp.exp2(s - m_new).astype(BF16)
            acc_ref[hh] = jnp.exp2(m_old - m_new) * acc_ref[hh] + _dot(vt, p)
            m_ref[hh] = m_new

        pipelined(tasks, scores, update)

    def slc_tasks(j, row0=None):
        tiles = [(k_tile(ks_ref, g, j), vt_tile(vst_ref, g, j)) for g in range(N_KV_HEADS)]
        return [(hh,) + tiles[hh // HPG] + (qa_ref[hh], ms_ref, accs_ref, row0) for hh in heads]

    def win_tasks(j, row0):
        tiles = [(k_tile(kw_ref, g, j), vt_tile(vwt_ref, g, j)) for g in range(N_KV_HEADS)]
        return [(hh,) + tiles[hh // HPG] + (qt_ref[0, hh], mw_ref, accw_ref, row0) for hh in heads]

    for hh in heads:
        ms_ref[hh] = jnp.full((1, TQ), M_INIT, F32)
        mw_ref[hh] = jnp.full((1, TQ), M_INIT, F32)
        accs_ref[hh] = jnp.zeros((PV_ROWS, TQ), F32)
        accw_ref[hh] = jnp.zeros((PV_ROWS, TQ), F32)

    p_parts = []

    def cmp_scores(hh):
        g = hh // HPG
        return _dot(kcmp_ref[0, :, g * LANES:(g + 1) * LANES], qt_ref[0, hh]) + biasc_ref[hh]

    def cmp_update(hh, s_c):
        m_c = jnp.maximum(jnp.max(s_c, axis=0, keepdims=True), M_INIT)
        p_c = jnp.exp2(s_c - m_c)
        l_c = jnp.sum(p_c, axis=0, keepdims=True)
        p_c = p_c * jnp.where(l_c > 0.0, 1.0 / l_c, 0.0)
        oc_ref[hh] = _dot(vcmpt_ref[0, hh // HPG, 0:HEAD_DIM, :], p_c.astype(BF16))
        p_parts.append(p_c)

    pipelined(list(heads), cmp_scores, cmp_update)

    blk = lax.broadcasted_iota(jnp.int32, (ns, TQ), 0)
    cur = (i * TQ + lax.broadcasted_iota(jnp.int32, (ns, TQ), 1)) // L_SLC
    forced = (blk == 0) | (blk == cur) | (blk == cur - 1)
    for g in range(N_KV_HEADS):
        p4 = p_parts[g * HPG:(g + 1) * HPG]
        p_sum = (p4[0] + p4[1]) + (p4[2] + p4[3])
        p_hi = p_sum.astype(BF16)
        p_lo = (p_sum - p_hi.astype(F32)).astype(BF16)
        imp = _dot(aggt_ref[...], p_hi) + _dot(aggt_ref[...], p_lo)
        imp_ref[g] = jnp.where(forced, FORCE, jnp.where(blk <= cur, imp, -FORCE))

    rows2 = jnp.where(i >= 2, 0, 3 * TQ)
    rows1 = jnp.where(i >= 1, TQ, 3 * TQ)
    run(win_tasks(jnp.maximum(i - 2, 0), rows2) + win_tasks(jnp.maximum(i - 1, 0), rows1)
        + win_tasks(i, 2 * TQ))

    n_grp = ns // SUBLANES
    blk8 = lax.broadcasted_iota(jnp.int32, (SUBLANES, TQ), 0)
    for g in range(N_KV_HEADS):
        grp = [imp_ref[g, SUBLANES * r:SUBLANES * (r + 1), :] for r in range(n_grp)]
        rank = [jnp.zeros((SUBLANES, TQ), F32) for _ in range(n_grp)]
        for k in range(ns):
            rowk = imp_ref[g, k:k + 1, :]
            kg = k // SUBLANES
            for r in range(n_grp):
                if r < kg:
                    beats = jnp.where(rowk > grp[r], 1.0, 0.0)
                elif r > kg:
                    beats = jnp.where(rowk >= grp[r], 1.0, 0.0)
                else:
                    beats = jnp.where(blk8 > k - SUBLANES * kg,
                                      jnp.where(rowk >= grp[r], 1.0, 0.0),
                                      jnp.where(rowk > grp[r], 1.0, 0.0))
                rank[r] = rank[r] + beats
        rank = jnp.concatenate(rank, axis=0)
        flags = jnp.where(rank < float(N_SEL), 0.0, -1.0).astype(BF16)
        if ns < HEAD_DIM:
            flags = jnp.concatenate([flags, jnp.zeros((HEAD_DIM - ns, TQ), BF16)], axis=0)
        for hh in range(g * HPG, (g + 1) * HPG):
            qa_ref[hh, 0:HEAD_DIM, :] = qt_ref[0, hh, 0:HEAD_DIM, :]
            qa_ref[hh, HEAD_DIM:, :] = flags

    n_far = jnp.maximum(i - 1, 0)

    def far_group(jj, carry):
        tasks = []
        for u in range(FAR_GROUP):
            tasks += slc_tasks(FAR_GROUP * jj + u)
        run(tasks)
        return carry

    lax.fori_loop(0, n_far // FAR_GROUP, far_group, 0)
    for rem in range(FAR_GROUP):
        @pl.when(n_far % FAR_GROUP == rem)
        def _(rem=rem):
            tasks = []
            for u in range(rem):
                tasks += slc_tasks(n_far - rem + u)
            run(tasks + slc_tasks(jnp.maximum(i - 1, 0), rows1) + slc_tasks(i, 2 * TQ))

    gates_t = jnp.transpose(ng_ref[...])
    outs = []
    for hh in heads:
        a_s, a_w = accs_ref[hh], accw_ref[hh]
        o_s = a_s[0:HEAD_DIM] * (1.0 / a_s[HEAD_DIM:HEAD_DIM + 1])
        o_w = a_w[0:HEAD_DIM] * (1.0 / a_w[HEAD_DIM:HEAD_DIM + 1])
        row = (hh // HPG) * LANES + 3 * (hh % HPG)
        outs.append(gates_t[row:row + 1] * oc_ref[hh] + gates_t[row + 1:row + 2] * o_s
                    + gates_t[row + 2:row + 3] * o_w)
    o_ref[...] = jnp.transpose(jnp.concatenate(outs, axis=0)).astype(BF16)


def _bias_of(thr_ref, rb_ref, dist, h):
    b = jnp.full(dist.shape, rb_ref[h], F32)
    for k in range(1, N_BUCKETS):
        b = jnp.where(dist >= thr_ref[k], rb_ref[k * N_HEADS + h], b)
    return b


def _cmp_bias_kernel(thr_ref, rb_ref, biasc_ref, *, nc, ncp):
    i = pl.program_id(0)
    per_tile = TQ // D_STRIDE
    band = 2 * per_tile
    assert TQ + D_STRIDE - (L_CMP - 1) >= MAX_DISTANCE

    def dist_of(c0, rows):
        c = c0 + lax.broadcasted_iota(jnp.int32, (rows, TQ), 0)
        r = lax.broadcasted_iota(jnp.int32, (rows, TQ), 1)
        dist = i * TQ + r - (c * D_STRIDE + L_CMP - 1)
        return dist, (dist >= 0) & (c < nc)

    _, ok_all = dist_of(0, ncp)
    band0 = pl.multiple_of(jnp.maximum(i * per_tile - per_tile, 0), per_tile)
    dist_b, ok_b = dist_of(band0, band)
    for h in range(N_HEADS):
        biasc_ref[h] = jnp.where(ok_all, rb_ref[(N_BUCKETS - 1) * N_HEADS + h], NEG)
        biasc_ref[h, pl.ds(band0, band), :] = jnp.where(
            ok_b, _bias_of(thr_ref, rb_ref, dist_b, h), NEG)


def _diag_bias_kernel(thr_ref, rb_ref, bd_ref):
    d0 = (lax.broadcasted_iota(jnp.int32, (TQ, TQ), 1)
          - lax.broadcasted_iota(jnp.int32, (TQ, TQ), 0))
    for h in range(N_HEADS):
        far = rb_ref[(N_BUCKETS - 1) * N_HEADS + h]
        bd_ref[h, 0:TQ, :] = jnp.where(d0 < 0, _bias_of(thr_ref, rb_ref, d0 + 2 * TQ, h) - far, NEG)
        bd_ref[h, TQ:2 * TQ, :] = _bias_of(thr_ref, rb_ref, d0 + TQ, h) - far
        bd_ref[h, 2 * TQ:3 * TQ, :] = jnp.where(d0 >= 0, _bias_of(thr_ref, rb_ref, d0, h) - far, NEG)
        bd_ref[h, 3 * TQ:, :] = jnp.full((TQ, TQ), NEG, F32)


def _attention_tables(rel_bias, seq):
    ncp = seq // D_STRIDE
    nc = (seq - L_CMP) // D_STRIDE + 1
    ns = seq // L_SLC
    nt = seq // TQ
    n_probe = 2 * MAX_DISTANCE
    buckets = _t5_bucket(jnp.arange(n_probe))
    thr = jnp.sum(buckets[None, :] < jnp.arange(N_BUCKETS)[:, None], axis=1).astype(jnp.int32)
    assert WINDOW == 2 * TQ
    rb = (rel_bias.astype(F32) * LOG2E).reshape(-1)
    bias_c = pl.pallas_call(
        functools.partial(_cmp_bias_kernel, nc=nc, ncp=ncp),
        out_shape=jax.ShapeDtypeStruct((N_HEADS, ncp, seq), F32),
        grid_spec=pltpu.PrefetchScalarGridSpec(
            num_scalar_prefetch=2, grid=(nt,), in_specs=[],
            out_specs=pl.BlockSpec((N_HEADS, ncp, TQ), lambda i, *_: (0, 0, i))),
        compiler_params=_cparams(1), name="cmp_bias")(thr, rb)
    bd = pl.pallas_call(
        _diag_bias_kernel, out_shape=jax.ShapeDtypeStruct((N_HEADS, 4 * TQ, TQ), F32),
        grid_spec=pltpu.PrefetchScalarGridSpec(
            num_scalar_prefetch=2, grid=(1,), in_specs=[],
            out_specs=pl.BlockSpec((N_HEADS, 4 * TQ, TQ), lambda i, *_: (0, 0, 0))),
        compiler_params=_cparams(1), name="diag_bias")(thr, rb)
    c_start = jnp.arange(ncp) * D_STRIDE
    c_end = c_start + L_CMP - 1
    s_start = jnp.arange(ns) * L_SLC
    aggt = ((c_end[None, :] >= s_start[:, None]) & (c_start[None, :] <= s_start[:, None] + L_SLC - 1)
            & (jnp.arange(ncp)[None, :] < nc)).astype(BF16)
    return bias_c, bd, aggt


def _nsa(qt, kcmp, vcmpt, ks, vst, kw, vwt, ng, rel_bias, batch, seq):
    first_far = TQ + 1
    assert MAX_EXACT + int(math.log(first_far / MAX_EXACT) / math.log(MAX_DISTANCE / MAX_EXACT)
                           * (N_BUCKETS - MAX_EXACT)) >= N_BUCKETS - 1, "far tiles need one bucket"
    ncp = seq // D_STRIDE
    ns = seq // L_SLC
    nt = seq // TQ
    bias_c, bd, aggt = _attention_tables(rel_bias, seq)
    kspec = pl.BlockSpec((1, N_KV_HEADS, seq, LANES), lambda b, i: (b, 0, 0, 0))
    vtspec = pl.BlockSpec((1, N_KV_HEADS, LANES, seq), lambda b, i: (b, 0, 0, 0))
    state = lambda rows: pltpu.VMEM((N_HEADS, rows, TQ), F32)
    const = lambda shape: pl.BlockSpec(shape, lambda b, i: (0,) * len(shape),
                                       pipeline_mode=pl.Buffered(1))
    grid_spec = pltpu.PrefetchScalarGridSpec(
        num_scalar_prefetch=0, grid=(batch, nt),
        in_specs=[pl.BlockSpec((1, N_HEADS, LANES, TQ), lambda b, i: (b, 0, 0, i)),
                  pl.BlockSpec((1, ncp, N_KV_HEADS * LANES), lambda b, i: (b, 0, 0)),
                  pl.BlockSpec((1, N_KV_HEADS, LANES, ncp), lambda b, i: (b, 0, 0, 0)),
                  pl.BlockSpec((N_HEADS, ncp, TQ), lambda b, i: (0, 0, i)),
                  kspec, vtspec, kspec, vtspec,
                  pl.BlockSpec((TQ, N_KV_HEADS * LANES), lambda b, i: (b * nt + i, 0)),
                  const((ns, ncp)), const((N_HEADS, 4 * TQ, TQ))],
        out_specs=pl.BlockSpec((TQ, N_HEADS * HEAD_DIM), lambda b, i: (b * nt + i, 0)),
        scratch_shapes=[pltpu.VMEM((N_KV_HEADS, ns, TQ), F32),
                        pltpu.VMEM((N_HEADS, LANES, TQ), BF16),
                        state(HEAD_DIM),
                        state(1), state(PV_ROWS),
                        state(1), state(PV_ROWS)])
    return pl.pallas_call(
        functools.partial(_nsa_kernel, ns=ns),
        out_shape=jax.ShapeDtypeStruct((batch * seq, N_HEADS * HEAD_DIM), BF16),
        grid_spec=grid_spec, compiler_params=_cparams(2), name="nsa",
    )(qt, kcmp, vcmpt, bias_c, ks, vst, kw, vwt, ng, aggt, bd)


def _ffn_kernel(x_ref, ya_ref, o_ref, gb_ref, p_ref, wao_ref, wout_ref, nf_ref, wup_ref, cw_ref,
                cb_ref, wd_ref, np_ref, wpg_ref, wpe_ref, nfin_ref, out_ref, carry_ref,
                *, tm, tiles_per_seq, d_ff, chunks):
    r = pl.program_id(0)

    @pl.when((r % tiles_per_seq) == 0)
    def _():
        carry_ref[...] = jnp.zeros_like(carry_ref)

    n = tm // FFN_SPLIT
    groups = [slice(k * n, (k + 1) * n) for k in range(FFN_SPLIT)]

    x1, hf = [], []
    for rows in groups:
        y = (ya_ref[rows, :].astype(F32)
             + gb_ref[rows, :].astype(F32) * _dot(o_ref[rows, :], wao_ref[...]))
        x1.append(x_ref[rows, :] + _dot(y.astype(BF16), wout_ref[...]))
        hf.append(_rms(x1[-1], nf_ref[...]).astype(BF16))

    def conv3(up, prev, c0, width):
        rowi = lax.broadcasted_iota(jnp.int32, up.shape, 0)
        s1 = jnp.where(rowi == 0, prev[7:8, :], pltpu.roll(up, 1, 0))
        s2 = jnp.where(rowi == 0, prev[6:7, :],
                       jnp.where(rowi == 1, prev[7:8, :], pltpu.roll(up, 2, 0)))
        return (cw_ref[0:1, c0:c0 + width] * s2 + cw_ref[1:2, c0:c0 + width] * s1
                + cw_ref[2:3, c0:c0 + width] * up + cb_ref[:, c0:c0 + width])

    acc = [None] * FFN_SPLIT
    for c0, width in chunks:
        cols = (slice(c0, c0 + width), slice(d_ff + c0, d_ff + c0 + width))
        prev = [carry_ref[:, cs] for cs in cols]
        for k in range(FFN_SPLIT):
            ups = [_dot(hf[k], wup_ref[:, cs]) for cs in cols]
            gate, val = [conv3(ups[t], prev[t], cols[t].start, width) for t in range(2)]
            prev = [u[n - 8:, :] for u in ups]
            part = _dot((_gelu(gate) * val).astype(BF16), wd_ref[c0:c0 + width, :])
            acc[k] = part if acc[k] is None else acc[k] + part
        for t in range(2):
            carry_ref[:, cols[t]] = prev[t]

    for k, rows in enumerate(groups):
        x2 = x1[k] + acc[k]
        pg = _sigmoid(_dot(_rms(x2, np_ref[...]).astype(BF16), wpg_ref[...]))
        pe = _dot(p_ref[rows, :].astype(BF16), wpe_ref[...])
        out_ref[rows, :] = _rms(x2 + pg * pe, nfin_ref[...])


def _ffn(x2, ya, o, gb, p2, w_attn_out, w_out, norm_ffn, w_up, ffn_dw_w, ffn_dw_b, w_down,
         norm_ple, w_ple_gate, w_ple, norm_final, seq, tm=512):
    t, d = x2.shape
    d_ff = w_down.shape[0]
    assert d_ff % MXU_DEPTH == 0
    chunks, c0 = [], 0
    while c0 < d_ff:
        width = min(FFN_CHUNK, d_ff - c0)
        chunks.append((c0, width))
        c0 += width
    tiles_per_seq = seq // tm
    cw = jnp.concatenate([ffn_dw_w, jnp.zeros((8 - FFN_CONV_WIDTH, 2 * d_ff), ffn_dw_w.dtype)], 0)
    vec = lambda v: v.reshape(1, -1)
    const = lambda a: pl.BlockSpec(a.shape, lambda r: (0, 0), pipeline_mode=pl.Buffered(1))
    rows = lambda n: pl.BlockSpec((tm, n), lambda r: (r, 0))
    kern = functools.partial(_ffn_kernel, tm=tm, tiles_per_seq=tiles_per_seq, d_ff=d_ff,
                             chunks=tuple(chunks))
    operands = (x2, ya, o, gb, p2, w_attn_out, w_out, vec(norm_ffn), w_up, cw, vec(ffn_dw_b),
                w_down, vec(norm_ple), w_ple_gate, w_ple, vec(norm_final))
    return pl.pallas_call(
        kern, out_shape=jax.ShapeDtypeStruct((t, d), F32), grid=(t // tm,),
        in_specs=[rows(d), rows(d), rows(o.shape[1]), rows(d), rows(p2.shape[1])]
        + [const(a) for a in operands[5:]],
        out_specs=rows(d),
        scratch_shapes=[pltpu.VMEM((8, 2 * d_ff), F32)],
        compiler_params=_cparams(1), name="ffn",
    )(*operands)


def kernel(x, p, rel_bias, norm_mix, w_in, conv_dw_w, conv_dw_b, conv_ln_g, conv_ln_b, w_conv_out,
           cmp_pe_k, cmp_pe_v, w_ck1, w_ck2, w_cv1, w_cv2, w_attn_out, w_out, norm_ffn, w_up,
           ffn_dw_w, ffn_dw_b, w_down, norm_ple, w_ple_gate, w_ple, norm_final):
    batch, seq, d = x.shape
    depth = w_in.shape[0]
    x2 = x.reshape(batch * seq, d)
    for i in range(depth):
        ya, qt, kc, vc, ks, vst, kw, vwt, ng, gb = _inproj(
            x2, norm_mix[i], _prep_inproj_weights(w_in[i], d), conv_dw_w[i], conv_dw_b[i],
            conv_ln_g[i], conv_ln_b[i], w_conv_out[i].astype(BF16), batch, seq)
        kcmp, vcmpt = _compress(kc, vc,
                               _prep_compress_weights(cmp_pe_k[i], w_ck1[i], w_ck2[i]),
                               _prep_compress_weights(cmp_pe_v[i], w_cv1[i], w_cv2[i]),
                               batch, seq)
        o = _nsa(qt, kcmp, vcmpt, ks, vst, kw, vwt, ng, rel_bias, batch, seq)
        assert i == depth - 1, "the final RMSNorm is fused into the (single) layer's MLP kernel"
        x2 = _ffn(x2, ya, o, gb, p[i].reshape(batch * seq, -1), w_attn_out[i].astype(BF16),
                  w_out[i].astype(BF16), norm_ffn[i], w_up[i].astype(BF16), ffn_dw_w[i],
                  ffn_dw_b[i], w_down[i].astype(BF16), norm_ple[i], w_ple_gate[i].astype(BF16),
                  w_ple[i].astype(BF16), norm_final, seq)
    return x2.reshape(batch, seq, d)
```

```python
import functools
import math

import jax
import jax.numpy as jnp
import numpy as np
from jax import lax
from jax.experimental import pallas as pl
from jax.experimental.pallas import tpu as pltpu

N_HEADS = 8
HEAD_DIM = 64
N_KV_HEADS = 2
HPG = N_HEADS // N_KV_HEADS
L_CMP = 32
D_STRIDE = 16
CMP_HIDDEN = 256
L_SLC = 64
N_SEL = 16
WINDOW = 512
N_BUCKETS = 32
MAX_EXACT = N_BUCKETS // 2
MAX_DISTANCE = 128
CONV_WIDTH = 31
FFN_CONV_WIDTH = 3
EPS = 1e-6
FORCE = 1e4

LANES = 128
MXU_DEPTH = 256
FFN_CHUNK = 6 * MXU_DEPTH
FFN_SPLIT = 2
NEG = -1e30
M_INIT = -1e29
LOG2E = 1.4426950408889634
PIPE_DEPTH = 6
FAR_GROUP = 4
TQ = 256
PV_ROWS = HEAD_DIM + 16
SUBLANES = 8
CONV_HALO = 32
CONV_ROWS = 64
VMEM_LIMIT = 56 * 1024 * 1024

F32 = jnp.float32
BF16 = jnp.bfloat16


def _cparams(n_axes):
    return pltpu.CompilerParams(dimension_semantics=("arbitrary",) * n_axes,
                                vmem_limit_bytes=VMEM_LIMIT)


def _dot(a, b):
    return jnp.dot(a, b, preferred_element_type=F32)


def _rms(xf, g):
    return xf * lax.rsqrt(jnp.mean(xf * xf, axis=-1, keepdims=True) + EPS) * g


def _sigmoid(x):
    return 1.0 / (1.0 + jnp.exp(-x))


def _gelu(x):
    return 0.5 * x * (1.0 + jnp.tanh(0.7978845608028654 * (x + 0.044715 * x * x * x)))


def _t5_bucket(dist):
    n = jnp.maximum(dist, 0)
    nf = jnp.maximum(n, MAX_EXACT).astype(F32)
    large = MAX_EXACT + (jnp.log(nf / MAX_EXACT) / math.log(MAX_DISTANCE / MAX_EXACT)
                         * (N_BUCKETS - MAX_EXACT)).astype(jnp.int32)
    large = jnp.minimum(large, N_BUCKETS - 1)
    return jnp.where(n < MAX_EXACT, n, large)


def _inproj_kernel(x_ref, g_ref, wu_ref, wq_ref, wkv_ref, wng_ref, wmg_ref,
                   cw_ref, cb_ref, lg_ref, lb_ref, wco_ref,
                   ya_ref, qt_ref, kc_ref, vc_ref, ks_ref, vst_ref, kw_ref, vwt_ref, ng_ref, gb_ref,
                   cs_ref, ext_ref, sh_ref, ga_ref, a_stage_ref, ga_stage_ref,
                   *, tm, n_tiles, tiles_per_seq, d_conv):
    step = pl.program_id(0)
    r = jnp.minimum(step, n_tiles - 1)

    @pl.when(step == 0)
    def _():
        ext_ref[...] = jnp.zeros(ext_ref.shape, F32)
        ga_ref[...] = jnp.zeros(ga_ref.shape, BF16)

    span = sh_ref.shape[1]
    for ph in range(1, SUBLANES):
        sh_ref[ph - 1] = ext_ref[ph:ph + span, :]
    off = CONV_HALO - (CONV_WIDTH - 1)
    conv = []
    for r0 in range(0, tm, CONV_ROWS):
        acc = jnp.zeros((CONV_ROWS, d_conv), F32) + cb_ref[...]
        for j in range(CONV_WIDTH):
            ph = (off + j) % SUBLANES
            base = r0 + off + j - ph
            if ph == 0:
                xs = ext_ref[base:base + CONV_ROWS, :]
            else:
                xs = sh_ref[ph - 1, base:base + CONV_ROWS, :]
            acc = acc + jnp.tile(cw_ref[j], (CONV_ROWS // SUBLANES, 1)) * xs
        conv.append(acc)
    conv = jnp.concatenate(conv, axis=0)
    mu = jnp.mean(conv, axis=-1, keepdims=True)
    cen = conv - mu
    var = jnp.mean(cen * cen, axis=-1, keepdims=True)
    y = cen * lax.rsqrt(var + EPS) * lg_ref[...] + lb_ref[...]
    y = (y * _sigmoid(y)).astype(BF16)

    h = _rms(x_ref[...], g_ref[...]).astype(BF16)
    u = _dot(h, wu_ref[...])
    a_stage_ref[...] = u[:, :d_conv] * _sigmoid(u[:, d_conv:])
    zmg = _dot(h, wmg_ref[...])
    d_model = zmg.shape[1] // 2
    ga_stage_ref[...] = _sigmoid(zmg[:, :d_model]).astype(BF16)
    gb_ref[...] = _sigmoid(zmg[:, d_model:]).astype(BF16)

    zqt = jnp.transpose(_dot(h, wq_ref[...]))
    zero_rows = jnp.zeros((HEAD_DIM, tm), BF16)
    for hh in range(N_HEADS):
        qt_ref[0, hh, 0:HEAD_DIM, :] = zqt[hh * HEAD_DIM:(hh + 1) * HEAD_DIM].astype(BF16)
        qt_ref[0, hh, HEAD_DIM:, :] = zero_rows

    zkv = _dot(h, wkv_ref[...])
    for which, out_ref in enumerate((kc_ref, vc_ref)):
        cs_ref[which] = zkv[:, which * LANES:(which + 1) * LANES]
        for tok in range(D_STRIDE):
            rows = cs_ref[which, pl.ds(tok, tm // D_STRIDE, stride=D_STRIDE), :]
            out_ref[:, tok * LANES:(tok + 1) * LANES] = rows.astype(BF16)
    lane = lax.broadcasted_iota(jnp.int32, (tm, LANES), 1)
    row = lax.broadcasted_iota(jnp.int32, (tm, LANES), 0)
    spos = (r % tiles_per_seq) * tm + row
    lo = lane < HEAD_DIM
    blk_tag = jnp.where((lane - HEAD_DIM) == spos // L_SLC, -NEG, 0.0)
    ones_rows = jnp.where(lax.broadcasted_iota(jnp.int32, (HEAD_DIM, tm), 0) < 8, 1.0, 0.0)
    ones_rows = ones_rows.astype(BF16)
    for g in range(N_KV_HEADS):
        for k_ref, vt_ref, base, tag in ((ks_ref, vst_ref, (2 + g) * LANES, blk_tag),
                                         (kw_ref, vwt_ref, (4 + g) * LANES, 0.0)):
            pair = zkv[:, base:base + LANES]
            k_ref[0, g] = jnp.where(lo, pair, tag).astype(BF16)
            vt_ref[0, g, 0:HEAD_DIM, :] = jnp.transpose(pair)[HEAD_DIM:].astype(BF16)
            vt_ref[0, g, HEAD_DIM:, :] = ones_rows

    ng_ref[...] = _sigmoid(_dot(h, wng_ref[...]))

    ya_ref[...] = (ga_ref[...].astype(F32) * _dot(y, wco_ref[...])).astype(BF16)

    seq_start = (r % tiles_per_seq) == 0
    ext_ref[0:CONV_HALO, :] = jnp.where(seq_start, 0.0, ext_ref[tm:tm + CONV_HALO, :])
    ext_ref[CONV_HALO:, :] = a_stage_ref[...]
    ga_ref[...] = ga_stage_ref[...]


def _prep_inproj_weights(w_in, d_model):
    d_conv = d_model // 2
    n_conv = 2 * d_conv
    n_q = N_HEADS * HEAD_DIM
    n_kv = 6 * N_KV_HEADS * HEAD_DIM
    n_ng = 3 * N_HEADS
    o = 0
    wu = w_in[:, o:o + n_conv]; o += n_conv
    wq = w_in[:, o:o + n_q]; o += n_q
    wkv = w_in[:, o:o + n_kv]; o += n_kv
    wng = w_in[:, o:o + n_ng]; o += n_ng
    wmg = w_in[:, o:]
    wq_p = wq * (LOG2E / math.sqrt(HEAD_DIM))
    kvcols = [wkv[:, 0:LANES], wkv[:, LANES:2 * LANES]]
    for k_kind in (2, 4):
        for g in range(N_KV_HEADS):
            for kind in (k_kind, k_kind + 1):
                c0 = kind * N_KV_HEADS * HEAD_DIM + g * HEAD_DIM
                kvcols.append(wkv[:, c0:c0 + HEAD_DIM])
    wkv_p = jnp.concatenate(kvcols, axis=1)
    per_g = HPG * 3
    ngcols = []
    for g in range(N_KV_HEADS):
        ngcols += [wng[:, g * per_g:(g + 1) * per_g],
                   jnp.zeros((d_model, LANES - per_g), w_in.dtype)]
    wng_p = jnp.concatenate(ngcols, axis=1)
    return tuple(w.astype(BF16) for w in (wu, wq_p, wkv_p, wng_p, wmg))


def _inproj(x2, norm_g, weights, conv_w, conv_b, ln_g, ln_b, w_conv_out, batch, seq, tm=512):
    t, d = x2.shape
    wu, wq, wkv, wng, wmg = weights
    d_conv = d // 2
    tiles_per_seq = seq // tm
    n_tiles = t // tm
    assert tm % CONV_ROWS == 0 and CONV_HALO >= CONV_WIDTH - 1 and CONV_HALO % SUBLANES == 0
    cw = jnp.broadcast_to(conv_w[:, None, :], (CONV_WIDTH, SUBLANES, d_conv))
    vec = lambda v: v.reshape(1, -1)
    full = lambda w: pl.BlockSpec(w.shape, lambda s: (0,) * w.ndim, pipeline_mode=pl.Buffered(1))
    tile = lambda s: jnp.minimum(s, n_tiles - 1)
    row = lambda n: pl.BlockSpec((tm, n), lambda s: (tile(s), 0))
    row_prev = lambda n: pl.BlockSpec((tm, n), lambda s: (jnp.maximum(s - 1, 0), 0))
    headed = lambda nh: pl.BlockSpec(
        (1, nh, tm, LANES), lambda s: (tile(s) // tiles_per_seq, 0, tile(s) % tiles_per_seq, 0))
    headed_t = lambda nh: pl.BlockSpec(
        (1, nh, LANES, tm), lambda s: (tile(s) // tiles_per_seq, 0, 0, tile(s) % tiles_per_seq))
    k_shape = jax.ShapeDtypeStruct((batch, N_KV_HEADS, seq, LANES), BF16)
    vt_shape = jax.ShapeDtypeStruct((batch, N_KV_HEADS, LANES, seq), BF16)
    out_shape = (
        jax.ShapeDtypeStruct((t, d), BF16),
        jax.ShapeDtypeStruct((batch, N_HEADS, LANES, seq), BF16),
        jax.ShapeDtypeStruct((t // D_STRIDE, D_STRIDE * LANES), BF16),
        jax.ShapeDtypeStruct((t // D_STRIDE, D_STRIDE * LANES), BF16),
        k_shape, vt_shape, k_shape, vt_shape,
        jax.ShapeDtypeStruct((t, N_KV_HEADS * LANES), F32),
        jax.ShapeDtypeStruct((t, d), BF16),
    )
    blocks = pl.BlockSpec((tm // D_STRIDE, D_STRIDE * LANES), lambda s: (tile(s), 0))
    out_specs = (row_prev(d), headed_t(N_HEADS), blocks, blocks,
                 headed(N_KV_HEADS), headed_t(N_KV_HEADS), headed(N_KV_HEADS),
                 headed_t(N_KV_HEADS), row(N_KV_HEADS * LANES), row(d))
    kern = functools.partial(_inproj_kernel, tm=tm, n_tiles=n_tiles, tiles_per_seq=tiles_per_seq,
                             d_conv=d_conv)
    consts = (vec(norm_g), wu, wq, wkv, wng, wmg, cw, vec(conv_b), vec(ln_g), vec(ln_b), w_conv_out)
    return pl.pallas_call(
        kern, out_shape=out_shape, grid=(n_tiles + 1,),
        in_specs=[row(d)] + [full(w) for w in consts],
        out_specs=out_specs,
        scratch_shapes=[pltpu.VMEM((2, tm, LANES), F32),
                        pltpu.VMEM((tm + CONV_HALO, d_conv), F32),
                        pltpu.VMEM((SUBLANES - 1, tm + CONV_HALO - SUBLANES, d_conv), F32),
                        pltpu.VMEM((tm, d), BF16),
                        pltpu.VMEM((tm, d_conv), F32), pltpu.VMEM((tm, d), BF16)],
        compiler_params=_cparams(1), name="inproj",
    )(x2, *consts)


def _compress_kernel(rk_ref, rv_ref, pek_ref, pev_ref, w1k_ref, w1v_ref, w2k_ref, w2v_ref,
                     kcmp_ref, vcmp_ref, *, ncp):
    def one(r_ref, pe_ref, w1_ref, w2_ref):
        r = r_ref[0]
        top = _dot(r, w1_ref[0])
        bot = _dot(r, w1_ref[1])
        pe_h = _dot(pe_ref[0], w1_ref[0]) + _dot(pe_ref[1], w1_ref[1])
        nxt = pltpu.roll(bot, ncp - 1, 0)
        rowi = lax.broadcasted_iota(jnp.int32, top.shape, 0)
        hid = top + jnp.where(rowi == ncp - 1, 0.0, nxt) + pe_h[0:1, :]
        return _dot(_gelu(hid).astype(BF16), w2_ref[...])

    kcmp_ref[0] = one(rk_ref, pek_ref, w1k_ref, w2k_ref).astype(BF16)
    v = one(rv_ref, pev_ref, w1v_ref, w2v_ref)
    rowt = lax.broadcasted_iota(jnp.int32, (LANES, ncp), 0)
    ones_rows = (rowt >= HEAD_DIM) & (rowt < HEAD_DIM + 8)
    for g in range(N_KV_HEADS):
        vt = jnp.transpose(v[:, g * LANES:(g + 1) * LANES])
        vcmp_ref[0, g] = jnp.where(ones_rows, 1.0, vt).astype(BF16)


def _prep_compress_weights(pe, w1, w2):
    half = L_CMP // 2
    w1r = w1.reshape(L_CMP, HEAD_DIM, CMP_HIDDEN).astype(BF16)
    w2p = jnp.pad(w2, ((0, 0), (0, LANES - HEAD_DIM))).astype(BF16)

    def per_group(w, axis):
        z = jnp.zeros_like(w)
        return jnp.stack([jnp.concatenate([w if k == g else z for k in range(N_KV_HEADS)], axis=-1)
                          for g in range(N_KV_HEADS)], axis=axis)

    w1big = per_group(w1r, 1).reshape(2, half * N_KV_HEADS * HEAD_DIM, N_KV_HEADS * CMP_HIDDEN)
    w2big = per_group(w2p, 0).reshape(N_KV_HEADS * CMP_HIDDEN, N_KV_HEADS * LANES)
    per = pe.reshape(2, half, 1, HEAD_DIM)
    pebig = jnp.broadcast_to(per, (2, half, N_KV_HEADS, HEAD_DIM)).reshape(2, 1, -1)
    pebig = jnp.broadcast_to(pebig, (2, 8, pebig.shape[-1])).astype(BF16)
    return pebig, w1big, w2big


def _compress(kc, vc, wk, wv, batch, seq):
    ncp = seq // D_STRIDE
    width = D_STRIDE * LANES
    rk = kc.reshape(batch, ncp, width)
    rv = vc.reshape(batch, ncp, width)
    pek, w1k, w2k = wk
    pev, w1v, w2v = wv
    c3 = lambda a: pl.BlockSpec(a.shape, lambda b: (0, 0, 0))
    c2 = lambda a: pl.BlockSpec(a.shape, lambda b: (0, 0))
    rspec = pl.BlockSpec((1, ncp, width), lambda b: (b, 0, 0))
    kspec = pl.BlockSpec((1, ncp, N_KV_HEADS * LANES), lambda b: (b, 0, 0))
    kshape = jax.ShapeDtypeStruct((batch, ncp, N_KV_HEADS * LANES), BF16)
    vspec = pl.BlockSpec((1, N_KV_HEADS, LANES, ncp), lambda b: (b, 0, 0, 0))
    vshape = jax.ShapeDtypeStruct((batch, N_KV_HEADS, LANES, ncp), BF16)
    return pl.pallas_call(
        functools.partial(_compress_kernel, ncp=ncp), out_shape=(kshape, vshape), grid=(batch,),
        in_specs=[rspec, rspec, c3(pek), c3(pev), c3(w1k), c3(w1v), c2(w2k), c2(w2v)],
        out_specs=(kspec, vspec), compiler_params=_cparams(1), name="compress",
    )(rk, rv, pek, pev, w1k, w1v, w2k, w2v)


def _nsa_kernel(qt_ref, kcmp_ref, vcmpt_ref, biasc_ref, ks_ref, vst_ref, kw_ref, vwt_ref,
                ng_ref, aggt_ref, bd_ref, o_ref,
                imp_ref, qa_ref, oc_ref, ms_ref, accs_ref, mw_ref, accw_ref, *, ns):
    i = pl.program_id(1)
    heads = range(N_HEADS)

    def k_tile(ref, g, j):
        return ref[0, g, pl.ds(pl.multiple_of(j * TQ, TQ), TQ), :]

    def vt_tile(ref, g, j):
        return ref[0, g, 0:PV_ROWS, pl.ds(pl.multiple_of(j * TQ, TQ), TQ)]

    def pipelined(tasks, scores, update):
        pending = [scores(t) for t in tasks[:PIPE_DEPTH]]
        for t, task in enumerate(tasks):
            update(task, pending.pop(0))
            if t + PIPE_DEPTH < len(tasks):
                pending.append(scores(tasks[t + PIPE_DEPTH]))

    def run(tasks):
        def scores(task):
            hh, kt, _, qmat, _, _, row0 = task
            s = _dot(kt, qmat)
            if row0 is None:
                return s
            if not isinstance(row0, int):
                row0 = pl.multiple_of(row0, TQ)
            return s + bd_ref[hh, pl.ds(row0, TQ), :]

        def update(task, s):
            hh, _, vt, _, m_ref, acc_ref, _ = task
            m_old = m_ref[hh]
            m_new = jnp.maximum(m_old, jnp.max(s, axis=0, keepdims=True))
            p = jnp.exp2(s - m_new).astype(BF16)
            acc_ref[hh] = jnp.exp2(m_old - m_new) * acc_ref[hh] + _dot(vt, p)
            m_ref[hh] = m_new

        pipelined(tasks, scores, update)

    def slc_tasks(j, row0=None):
        tiles = [(k_tile(ks_ref, g, j), vt_tile(vst_ref, g, j)) for g in range(N_KV_HEADS)]
        return [(hh,) + tiles[hh // HPG] + (qa_ref[hh], ms_ref, accs_ref, row0) for hh in heads]

    def win_tasks(j, row0):
        tiles = [(k_tile(kw_ref, g, j), vt_tile(vwt_ref, g, j)) for g in range(N_KV_HEADS)]
        return [(hh,) + tiles[hh // HPG] + (qt_ref[0, hh], mw_ref, accw_ref, row0) for hh in heads]

    for hh in heads:
        ms_ref[hh] = jnp.full((1, TQ), M_INIT, F32)
        mw_ref[hh] = jnp.full((1, TQ), M_INIT, F32)
        accs_ref[hh] = jnp.zeros((PV_ROWS, TQ), F32)
        accw_ref[hh] = jnp.zeros((PV_ROWS, TQ), F32)

    p_parts = []

    def cmp_scores(hh):
        g = hh // HPG
        return _dot(kcmp_ref[0, :, g * LANES:(g + 1) * LANES], qt_ref[0, hh]) + biasc_ref[hh]

    def cmp_update(hh, s_c):
        m_c = jnp.maximum(jnp.max(s_c, axis=0, keepdims=True), M_INIT)
        p_c = jnp.exp2(s_c - m_c)
        l_c = jnp.sum(p_c, axis=0, keepdims=True)
        p_c = p_c * jnp.where(l_c > 0.0, 1.0 / l_c, 0.0)
        oc_ref[hh] = _dot(vcmpt_ref[0, hh // HPG, 0:HEAD_DIM, :], p_c.astype(BF16))
        p_parts.append(p_c)

    pipelined(list(heads), cmp_scores, cmp_update)

    blk = lax.broadcasted_iota(jnp.int32, (ns, TQ), 0)
    cur = (i * TQ + lax.broadcasted_iota(jnp.int32, (ns, TQ), 1)) // L_SLC
    forced = (blk == 0) | (blk == cur) | (blk == cur - 1)
    for g in range(N_KV_HEADS):
        p4 = p_parts[g * HPG:(g + 1) * HPG]
        p_sum = (p4[0] + p4[1]) + (p4[2] + p4[3])
        p_hi = p_sum.astype(BF16)
        p_lo = (p_sum - p_hi.astype(F32)).astype(BF16)
        imp = _dot(aggt_ref[...], p_hi) + _dot(aggt_ref[...], p_lo)
        imp_ref[g] = jnp.where(forced, FORCE, jnp.where(blk <= cur, imp, -FORCE))

    rows2 = jnp.where(i >= 2, 0, 3 * TQ)
    rows1 = jnp.where(i >= 1, TQ, 3 * TQ)
    run(win_tasks(jnp.maximum(i - 2, 0), rows2) + win_tasks(jnp.maximum(i - 1, 0), rows1)
        + win_tasks(i, 2 * TQ))

    n_grp = ns // SUBLANES
    blk8 = lax.broadcasted_iota(jnp.int32, (SUBLANES, TQ), 0)

    def block_flags(g):
        grp = [imp_ref[g, SUBLANES * r:SUBLANES * (r + 1), :] for r in range(n_grp)]
        rank = [jnp.zeros((SUBLANES, TQ), F32) for _ in range(n_grp)]
        for k in range(ns):
            rowk = imp_ref[g, k:k + 1, :]
            kg = k // SUBLANES
            for r in range(n_grp):
                if r < kg:
                    beats = jnp.where(rowk > grp[r], 1.0, 0.0)
                elif r > kg:
                    beats = jnp.where(rowk >= grp[r], 1.0, 0.0)
                else:
                    beats = jnp.where(blk8 > k - SUBLANES * kg,
                                      jnp.where(rowk >= grp[r], 1.0, 0.0),
                                      jnp.where(rowk > grp[r], 1.0, 0.0))
                rank[r] = rank[r] + beats
        rank = jnp.concatenate(rank, axis=0)
        flags = jnp.where(rank < float(N_SEL), 0.0, -1.0).astype(BF16)
        if ns < HEAD_DIM:
            flags = jnp.concatenate([flags, jnp.zeros((HEAD_DIM - ns, TQ), BF16)], axis=0)
        return flags

    rank_needed = (i + 1) * (TQ // L_SLC) > N_SEL
    for hh in heads:
        qa_ref[hh, 0:HEAD_DIM, :] = qt_ref[0, hh, 0:HEAD_DIM, :]

    @pl.when(rank_needed)
    def _():
        for g in range(N_KV_HEADS):
            flags = block_flags(g)
            for hh in range(g * HPG, (g + 1) * HPG):
                qa_ref[hh, HEAD_DIM:, :] = flags

    @pl.when(jnp.logical_not(rank_needed))
    def _():
        for hh in heads:
            qa_ref[hh, HEAD_DIM:, :] = jnp.zeros((HEAD_DIM, TQ), BF16)

    n_far = jnp.maximum(i - 1, 0)

    def far_group(jj, carry):
        tasks = []
        for u in range(FAR_GROUP):
            tasks += slc_tasks(FAR_GROUP * jj + u)
        run(tasks)
        return carry

    lax.fori_loop(0, n_far // FAR_GROUP, far_group, 0)
    for rem in range(FAR_GROUP):
        @pl.when(n_far % FAR_GROUP == rem)
        def _(rem=rem):
            tasks = []
            for u in range(rem):
                tasks += slc_tasks(n_far - rem + u)
            run(tasks + slc_tasks(jnp.maximum(i - 1, 0), rows1) + slc_tasks(i, 2 * TQ))

    gates_t = jnp.transpose(ng_ref[...])
    outs = []
    for hh in heads:
        a_s, a_w = accs_ref[hh], accw_ref[hh]
        o_s = a_s[0:HEAD_DIM] * (1.0 / a_s[HEAD_DIM:HEAD_DIM + 1])
        o_w = a_w[0:HEAD_DIM] * (1.0 / a_w[HEAD_DIM:HEAD_DIM + 1])
        row = (hh // HPG) * LANES + 3 * (hh % HPG)
        outs.append(gates_t[row:row + 1] * oc_ref[hh] + gates_t[row + 1:row + 2] * o_s
                    + gates_t[row + 2:row + 3] * o_w)
    o_ref[...] = jnp.transpose(jnp.concatenate(outs, axis=0)).astype(BF16)


def _bias_of(thr_ref, rb_ref, dist, h):
    b = jnp.full(dist.shape, rb_ref[h], F32)
    for k in range(1, N_BUCKETS):
        b = jnp.where(dist >= thr_ref[k], rb_ref[k * N_HEADS + h], b)
    return b


def _cmp_bias_kernel(thr_ref, rb_ref, biasc_ref, *, nc, ncp):
    i = pl.program_id(0)
    per_tile = TQ // D_STRIDE
    band = 2 * per_tile
    assert TQ + D_STRIDE - (L_CMP - 1) >= MAX_DISTANCE

    def dist_of(c0, rows):
        c = c0 + lax.broadcasted_iota(jnp.int32, (rows, TQ), 0)
        r = lax.broadcasted_iota(jnp.int32, (rows, TQ), 1)
        dist = i * TQ + r - (c * D_STRIDE + L_CMP - 1)
        return dist, (dist >= 0) & (c < nc)

    _, ok_all = dist_of(0, ncp)
    band0 = pl.multiple_of(jnp.maximum(i * per_tile - per_tile, 0), per_tile)
    dist_b, ok_b = dist_of(band0, band)
    for h in range(N_HEADS):
        biasc_ref[h] = jnp.where(ok_all, rb_ref[(N_BUCKETS - 1) * N_HEADS + h], NEG)
        biasc_ref[h, pl.ds(band0, band), :] = jnp.where(
            ok_b, _bias_of(thr_ref, rb_ref, dist_b, h), NEG)


def _diag_bias_kernel(thr_ref, rb_ref, bd_ref):
    d0 = (lax.broadcasted_iota(jnp.int32, (TQ, TQ), 1)
          - lax.broadcasted_iota(jnp.int32, (TQ, TQ), 0))
    for h in range(N_HEADS):
        far = rb_ref[(N_BUCKETS - 1) * N_HEADS + h]
        bd_ref[h, 0:TQ, :] = jnp.where(d0 < 0, _bias_of(thr_ref, rb_ref, d0 + 2 * TQ, h) - far, NEG)
        bd_ref[h, TQ:2 * TQ, :] = _bias_of(thr_ref, rb_ref, d0 + TQ, h) - far
        bd_ref[h, 2 * TQ:3 * TQ, :] = jnp.where(d0 >= 0, _bias_of(thr_ref, rb_ref, d0, h) - far, NEG)
        bd_ref[h, 3 * TQ:, :] = jnp.full((TQ, TQ), NEG, F32)


def _attention_tables(rel_bias, seq):
    ncp = seq // D_STRIDE
    nc = (seq - L_CMP) // D_STRIDE + 1
    ns = seq // L_SLC
    nt = seq // TQ
    n_probe = 2 * MAX_DISTANCE
    buckets = _t5_bucket(jnp.arange(n_probe))
    thr = jnp.sum(buckets[None, :] < jnp.arange(N_BUCKETS)[:, None], axis=1).astype(jnp.int32)
    assert WINDOW == 2 * TQ
    rb = (rel_bias.astype(F32) * LOG2E).reshape(-1)
    bias_c = pl.pallas_call(
        functools.partial(_cmp_bias_kernel, nc=nc, ncp=ncp),
        out_shape=jax.ShapeDtypeStruct((N_HEADS, ncp, seq), F32),
        grid_spec=pltpu.PrefetchScalarGridSpec(
            num_scalar_prefetch=2, grid=(nt,), in_specs=[],
            out_specs=pl.BlockSpec((N_HEADS, ncp, TQ), lambda i, *_: (0, 0, i))),
        compiler_params=_cparams(1), name="cmp_bias")(thr, rb)
    bd = pl.pallas_call(
        _diag_bias_kernel, out_shape=jax.ShapeDtypeStruct((N_HEADS, 4 * TQ, TQ), F32),
        grid_spec=pltpu.PrefetchScalarGridSpec(
            num_scalar_prefetch=2, grid=(1,), in_specs=[],
            out_specs=pl.BlockSpec((N_HEADS, 4 * TQ, TQ), lambda i, *_: (0, 0, 0))),
        compiler_params=_cparams(1), name="diag_bias")(thr, rb)
    c_start = jnp.arange(ncp) * D_STRIDE
    c_end = c_start + L_CMP - 1
    s_start = jnp.arange(ns) * L_SLC
    aggt = ((c_end[None, :] >= s_start[:, None]) & (c_start[None, :] <= s_start[:, None] + L_SLC - 1)
            & (jnp.arange(ncp)[None, :] < nc)).astype(BF16)
    return bias_c, bd, aggt


def _nsa(qt, kcmp, vcmpt, ks, vst, kw, vwt, ng, rel_bias, batch, seq):
    first_far = TQ + 1
    assert MAX_EXACT + int(math.log(first_far / MAX_EXACT) / math.log(MAX_DISTANCE / MAX_EXACT)
                           * (N_BUCKETS - MAX_EXACT)) >= N_BUCKETS - 1, "far tiles need one bucket"
    ncp = seq // D_STRIDE
    ns = seq // L_SLC
    nt = seq // TQ
    bias_c, bd, aggt = _attention_tables(rel_bias, seq)
    kspec = pl.BlockSpec((1, N_KV_HEADS, seq, LANES), lambda b, i: (b, 0, 0, 0))
    vtspec = pl.BlockSpec((1, N_KV_HEADS, LANES, seq), lambda b, i: (b, 0, 0, 0))
    state = lambda rows: pltpu.VMEM((N_HEADS, rows, TQ), F32)
    const = lambda shape: pl.BlockSpec(shape, lambda b, i: (0,) * len(shape),
                                       pipeline_mode=pl.Buffered(1))
    grid_spec = pltpu.PrefetchScalarGridSpec(
        num_scalar_prefetch=0, grid=(batch, nt),
        in_specs=[pl.BlockSpec((1, N_HEADS, LANES, TQ), lambda b, i: (b, 0, 0, i)),
                  pl.BlockSpec((1, ncp, N_KV_HEADS * LANES), lambda b, i: (b, 0, 0)),
                  pl.BlockSpec((1, N_KV_HEADS, LANES, ncp), lambda b, i: (b, 0, 0, 0)),
                  pl.BlockSpec((N_HEADS, ncp, TQ), lambda b, i: (0, 0, i)),
                  kspec, vtspec, kspec, vtspec,
                  pl.BlockSpec((TQ, N_KV_HEADS * LANES), lambda b, i: (b * nt + i, 0)),
                  const((ns, ncp)), const((N_HEADS, 4 * TQ, TQ))],
        out_specs=pl.BlockSpec((TQ, N_HEADS * HEAD_DIM), lambda b, i: (b * nt + i, 0)),
        scratch_shapes=[pltpu.VMEM((N_KV_HEADS, ns, TQ), F32),
                        pltpu.VMEM((N_HEADS, LANES, TQ), BF16),
                        state(HEAD_DIM),
                        state(1), state(PV_ROWS),
                        state(1), state(PV_ROWS)])
    return pl.pallas_call(
        functools.partial(_nsa_kernel, ns=ns),
        out_shape=jax.ShapeDtypeStruct((batch * seq, N_HEADS * HEAD_DIM), BF16),
        grid_spec=grid_spec, compiler_params=_cparams(2), name="nsa",
    )(qt, kcmp, vcmpt, bias_c, ks, vst, kw, vwt, ng, aggt, bd)


def _ffn_kernel(x_ref, ya_ref, o_ref, gb_ref, p_ref, wao_ref, wout_ref, nf_ref, wup_ref, cw_ref,
                cb_ref, wd_ref, np_ref, wpg_ref, wpe_ref, nfin_ref, out_ref, carry_ref,
                *, tm, tiles_per_seq, d_ff, chunks):
    r = pl.program_id(0)

    @pl.when((r % tiles_per_seq) == 0)
    def _():
        carry_ref[...] = jnp.zeros_like(carry_ref)

    n = tm // FFN_SPLIT
    groups = [slice(k * n, (k + 1) * n) for k in range(FFN_SPLIT)]

    x1, hf = [], []
    for rows in groups:
        y = (ya_ref[rows, :].astype(F32)
             + gb_ref[rows, :].astype(F32) * _dot(o_ref[rows, :], wao_ref[...]))
        x1.append(x_ref[rows, :] + _dot(y.astype(BF16), wout_ref[...]))
        hf.append(_rms(x1[-1], nf_ref[...]).astype(BF16))

    def conv3(up, prev, c0, width):
        rowi = lax.broadcasted_iota(jnp.int32, up.shape, 0)
        s1 = jnp.where(rowi == 0, prev[7:8, :], pltpu.roll(up, 1, 0))
        s2 = jnp.where(rowi == 0, prev[6:7, :],
                       jnp.where(rowi == 1, prev[7:8, :], pltpu.roll(up, 2, 0)))
        return (cw_ref[0:1, c0:c0 + width] * s2 + cw_ref[1:2, c0:c0 + width] * s1
                + cw_ref[2:3, c0:c0 + width] * up + cb_ref[:, c0:c0 + width])

    acc = [None] * FFN_SPLIT
    for c0, width in chunks:
        cols = (slice(c0, c0 + width), slice(d_ff + c0, d_ff + c0 + width))
        prev = [carry_ref[:, cs] for cs in cols]
        for k in range(FFN_SPLIT):
            ups = [_dot(hf[k], wup_ref[:, cs]) for cs in cols]
            gate, val = [conv3(ups[t], prev[t], cols[t].start, width) for t in range(2)]
            prev = [u[n - 8:, :] for u in ups]
            part = _dot((_gelu(gate) * val).astype(BF16), wd_ref[c0:c0 + width, :])
            acc[k] = part if acc[k] is None else acc[k] + part
        for t in range(2):
            carry_ref[:, cols[t]] = prev[t]

    for k, rows in enumerate(groups):
        x2 = x1[k] + acc[k]
        pg = _sigmoid(_dot(_rms(x2, np_ref[...]).astype(BF16), wpg_ref[...]))
        pe = _dot(p_ref[rows, :].astype(BF16), wpe_ref[...])
        out_ref[rows, :] = _rms(x2 + pg * pe, nfin_ref[...])


def _ffn(x2, ya, o, gb, p2, w_attn_out, w_out, norm_ffn, w_up, ffn_dw_w, ffn_dw_b, w_down,
         norm_ple, w_ple_gate, w_ple, norm_final, seq, p_tile0=0, tm=512):
    t, d = x2.shape
    d_ff = w_down.shape[0]
    assert d_ff % MXU_DEPTH == 0
    chunks, c0 = [], 0
    while c0 < d_ff:
        width = min(FFN_CHUNK, d_ff - c0)
        chunks.append((c0, width))
        c0 += width
    tiles_per_seq = seq // tm
    cw = jnp.concatenate([ffn_dw_w, jnp.zeros((8 - FFN_CONV_WIDTH, 2 * d_ff), ffn_dw_w.dtype)], 0)
    vec = lambda v: v.reshape(1, -1)
    const = lambda a: pl.BlockSpec(a.shape, lambda r: (0, 0), pipeline_mode=pl.Buffered(1))
    rows = lambda n: pl.BlockSpec((tm, n), lambda r: (r, 0))
    kern = functools.partial(_ffn_kernel, tm=tm, tiles_per_seq=tiles_per_seq, d_ff=d_ff,
                             chunks=tuple(chunks))
    operands = (x2, ya, o, gb, p2, w_attn_out, w_out, vec(norm_ffn), w_up, cw, vec(ffn_dw_b),
                w_down, vec(norm_ple), w_ple_gate, w_ple, vec(norm_final))
    return pl.pallas_call(
        kern, out_shape=jax.ShapeDtypeStruct((t, d), F32), grid=(t // tm,),
        in_specs=[rows(d), rows(d), rows(o.shape[1]), rows(d),
                  pl.BlockSpec((tm, p2.shape[1]), lambda r: (p_tile0 + r, 0))]
        + [const(a) for a in operands[5:]],
        out_specs=rows(d),
        scratch_shapes=[pltpu.VMEM((8, 2 * d_ff), F32)],
        compiler_params=_cparams(1), name="ffn",
    )(*operands)


def kernel(x, p, rel_bias, norm_mix, w_in, conv_dw_w, conv_dw_b, conv_ln_g, conv_ln_b, w_conv_out,
           cmp_pe_k, cmp_pe_v, w_ck1, w_ck2, w_cv1, w_cv2, w_attn_out, w_out, norm_ffn, w_up,
           ffn_dw_w, ffn_dw_b, w_down, norm_ple, w_ple_gate, w_ple, norm_final):
    batch, seq, d = x.shape
    depth = w_in.shape[0]
    x2 = x.reshape(batch * seq, d)
    for i in range(depth):
        ya, qt, kc, vc, ks, vst, kw, vwt, ng, gb = _inproj(
            x2, norm_mix[i], _prep_inproj_weights(w_in[i], d), conv_dw_w[i], conv_dw_b[i],
            conv_ln_g[i], conv_ln_b[i], w_conv_out[i].astype(BF16), batch, seq)
        kcmp, vcmpt = _compress(kc, vc,
                               _prep_compress_weights(cmp_pe_k[i], w_ck1[i], w_ck2[i]),
                               _prep_compress_weights(cmp_pe_v[i], w_cv1[i], w_cv2[i]),
                               batch, seq)
        o = _nsa(qt, kcmp, vcmpt, ks, vst, kw, vwt, ng, rel_bias, batch, seq)
        assert i == depth - 1, "the final RMSNorm is fused into the (single) layer's MLP kernel"
        ffn_tm = 512
        x2 = _ffn(x2, ya, o, gb, p.reshape(depth * batch * seq, -1), w_attn_out[i].astype(BF16),
                  w_out[i].astype(BF16), norm_ffn[i], w_up[i].astype(BF16), ffn_dw_w[i],
                  ffn_dw_b[i], w_down[i].astype(BF16), norm_ple[i], w_ple_gate[i].astype(BF16),
                  w_ple[i].astype(BF16), norm_final, seq,
                  p_tile0=i * (batch * seq // ffn_tm), tm=ffn_tm)
    return x2.reshape(batch, seq, d)
```

```python
import functools
import math

import jax
import jax.numpy as jnp
from jax import lax
from jax.experimental import pallas as pl
from jax.experimental.pallas import tpu as pltpu

N_HEADS = 8
HEAD_DIM = 64
N_KV_HEADS = 2
HPG = N_HEADS // N_KV_HEADS
L_CMP = 32
D_STRIDE = 16
CMP_HIDDEN = 256
L_SLC = 64
N_SEL = 16
WINDOW = 512
N_BUCKETS = 32
MAX_EXACT = N_BUCKETS // 2
MAX_DISTANCE = 128
CONV_WIDTH = 31
FFN_CONV_WIDTH = 3
EPS = 1e-6
FORCE = 1e4

LANES = 128
MXU_DEPTH = 256
FFN_CHUNK = 6 * MXU_DEPTH
FFN_SPLIT = 2
NEG = -1e30
M_INIT = -1e29
LOG2E = 1.4426950408889634
PIPE_DEPTH = 6
RANK_TILE_SPAN = 4
FAR_GROUP = 4
TQ = 256
PV_ROWS = HEAD_DIM + 16
SUBLANES = 8
CONV_HALO = 32
CONV_ROWS = 64
VMEM_LIMIT = 56 * 1024 * 1024

F32 = jnp.float32
BF16 = jnp.bfloat16


def _cparams(n_axes):
    return pltpu.CompilerParams(dimension_semantics=("arbitrary",) * n_axes,
                                vmem_limit_bytes=VMEM_LIMIT)


def _dot(a, b):
    return jnp.dot(a, b, preferred_element_type=F32)


def _rms(xf, g):
    return xf * lax.rsqrt(jnp.mean(xf * xf, axis=-1, keepdims=True) + EPS) * g


def _sigmoid(x):
    return 1.0 / (1.0 + jnp.exp(-x))


def _gelu(x):
    return 0.5 * x * (1.0 + jnp.tanh(0.7978845608028654 * (x + 0.044715 * x * x * x)))


def _t5_bucket(dist):
    n = jnp.maximum(dist, 0)
    nf = jnp.maximum(n, MAX_EXACT).astype(F32)
    large = MAX_EXACT + (jnp.log(nf / MAX_EXACT) / math.log(MAX_DISTANCE / MAX_EXACT)
                         * (N_BUCKETS - MAX_EXACT)).astype(jnp.int32)
    large = jnp.minimum(large, N_BUCKETS - 1)
    return jnp.where(n < MAX_EXACT, n, large)


def _inproj_kernel(x_ref, g_ref, wu_ref, wq_ref, wkv_ref, wng_ref, wmg_ref,
                   cw_ref, cb_ref, lg_ref, lb_ref, wco_ref,
                   ya_ref, qt_ref, kc_ref, vc_ref, ks_ref, vst_ref, kw_ref, vwt_ref, ng_ref, gb_ref,
                   cs_ref, ext_ref, sh_ref, ga_ref, a_stage_ref, ga_stage_ref,
                   *, tm, n_tiles, tiles_per_seq, d_conv):
    step = pl.program_id(0)
    r = jnp.minimum(step, n_tiles - 1)

    @pl.when(step == 0)
    def _():
        ext_ref[...] = jnp.zeros(ext_ref.shape, F32)
        ga_ref[...] = jnp.zeros(ga_ref.shape, BF16)

    span = sh_ref.shape[1]
    for ph in range(1, SUBLANES):
        sh_ref[ph - 1] = ext_ref[ph:ph + span, :]
    off = CONV_HALO - (CONV_WIDTH - 1)
    conv = []
    for r0 in range(0, tm, CONV_ROWS):
        acc = jnp.zeros((CONV_ROWS, d_conv), F32) + cb_ref[...]
        for j in range(CONV_WIDTH):
            ph = (off + j) % SUBLANES
            base = r0 + off + j - ph
            if ph == 0:
                xs = ext_ref[base:base + CONV_ROWS, :]
            else:
                xs = sh_ref[ph - 1, base:base + CONV_ROWS, :]
            acc = acc + jnp.tile(cw_ref[j], (CONV_ROWS // SUBLANES, 1)) * xs
        conv.append(acc)
    conv = jnp.concatenate(conv, axis=0)
    mu = jnp.mean(conv, axis=-1, keepdims=True)
    cen = conv - mu
    var = jnp.mean(cen * cen, axis=-1, keepdims=True)
    y = cen * lax.rsqrt(var + EPS) * lg_ref[...] + lb_ref[...]
    y = (y * _sigmoid(y)).astype(BF16)

    h = _rms(x_ref[...], g_ref[...]).astype(BF16)
    u = _dot(h, wu_ref[...])
    a_stage_ref[...] = u[:, :d_conv] * _sigmoid(u[:, d_conv:])
    zmg = _dot(h, wmg_ref[...])
    d_model = zmg.shape[1] // 2
    ga_stage_ref[...] = _sigmoid(zmg[:, :d_model]).astype(BF16)
    gb_ref[...] = _sigmoid(zmg[:, d_model:]).astype(BF16)

    zqt = jnp.transpose(_dot(h, wq_ref[...]))
    zero_rows = jnp.zeros((HEAD_DIM, tm), BF16)
    for hh in range(N_HEADS):
        qt_ref[0, hh, 0:HEAD_DIM, :] = zqt[hh * HEAD_DIM:(hh + 1) * HEAD_DIM].astype(BF16)
        qt_ref[0, hh, HEAD_DIM:, :] = zero_rows

    zkv = _dot(h, wkv_ref[...])
    for which, out_ref in enumerate((kc_ref, vc_ref)):
        cs_ref[which] = zkv[:, which * LANES:(which + 1) * LANES]
        for tok in range(D_STRIDE):
            rows = cs_ref[which, pl.ds(tok, tm // D_STRIDE, stride=D_STRIDE), :]
            out_ref[:, tok * LANES:(tok + 1) * LANES] = rows.astype(BF16)
    lane = lax.broadcasted_iota(jnp.int32, (tm, LANES), 1)
    row = lax.broadcasted_iota(jnp.int32, (tm, LANES), 0)
    spos = (r % tiles_per_seq) * tm + row
    lo = lane < HEAD_DIM
    blk_tag = jnp.where((lane - HEAD_DIM) == spos // L_SLC, -NEG, 0.0)
    ones_rows = jnp.where(lax.broadcasted_iota(jnp.int32, (HEAD_DIM, tm), 0) < 8, 1.0, 0.0)
    ones_rows = ones_rows.astype(BF16)
    for g in range(N_KV_HEADS):
        for k_ref, vt_ref, base, tag in ((ks_ref, vst_ref, (2 + g) * LANES, blk_tag),
                                         (kw_ref, vwt_ref, (4 + g) * LANES, 0.0)):
            pair = zkv[:, base:base + LANES]
            k_ref[0, g] = jnp.where(lo, pair, tag).astype(BF16)
            vt_ref[0, g, 0:HEAD_DIM, :] = jnp.transpose(pair)[HEAD_DIM:].astype(BF16)
            vt_ref[0, g, HEAD_DIM:, :] = ones_rows

    ng_ref[...] = _sigmoid(_dot(h, wng_ref[...]))

    ya_ref[...] = (ga_ref[...].astype(F32) * _dot(y, wco_ref[...])).astype(BF16)

    seq_start = (r % tiles_per_seq) == 0
    ext_ref[0:CONV_HALO, :] = jnp.where(seq_start, 0.0, ext_ref[tm:tm + CONV_HALO, :])
    ext_ref[CONV_HALO:, :] = a_stage_ref[...]
    ga_ref[...] = ga_stage_ref[...]


def _prep_inproj_weights(w_in, d_model):
    d_conv = d_model // 2
    n_conv = 2 * d_conv
    n_q = N_HEADS * HEAD_DIM
    n_kv = 6 * N_KV_HEADS * HEAD_DIM
    n_ng = 3 * N_HEADS
    o = 0
    wu = w_in[:, o:o + n_conv]; o += n_conv
    wq = w_in[:, o:o + n_q]; o += n_q
    wkv = w_in[:, o:o + n_kv]; o += n_kv
    wng = w_in[:, o:o + n_ng]; o += n_ng
    wmg = w_in[:, o:]
    wq_p = wq * (LOG2E / math.sqrt(HEAD_DIM))
    kvcols = [wkv[:, 0:LANES], wkv[:, LANES:2 * LANES]]
    for k_kind in (2, 4):
        for g in range(N_KV_HEADS):
            for kind in (k_kind, k_kind + 1):
                c0 = kind * N_KV_HEADS * HEAD_DIM + g * HEAD_DIM
                kvcols.append(wkv[:, c0:c0 + HEAD_DIM])
    wkv_p = jnp.concatenate(kvcols, axis=1)
    per_g = HPG * 3
    ngcols = []
    for g in range(N_KV_HEADS):
        ngcols += [wng[:, g * per_g:(g + 1) * per_g],
                   jnp.zeros((d_model, LANES - per_g), w_in.dtype)]
    wng_p = jnp.concatenate(ngcols, axis=1)
    return tuple(w.astype(BF16) for w in (wu, wq_p, wkv_p, wng_p, wmg))


def _inproj(x2, norm_g, weights, conv_w, conv_b, ln_g, ln_b, w_conv_out, batch, seq, tm=512):
    t, d = x2.shape
    wu, wq, wkv, wng, wmg = weights
    d_conv = d // 2
    tiles_per_seq = seq // tm
    n_tiles = t // tm
    assert tm % CONV_ROWS == 0 and CONV_HALO >= CONV_WIDTH - 1 and CONV_HALO % SUBLANES == 0
    cw = jnp.broadcast_to(conv_w[:, None, :], (CONV_WIDTH, SUBLANES, d_conv))
    vec = lambda v: v.reshape(1, -1)
    full = lambda w: pl.BlockSpec(w.shape, lambda s: (0,) * w.ndim, pipeline_mode=pl.Buffered(1))
    tile = lambda s: jnp.minimum(s, n_tiles - 1)
    row = lambda n: pl.BlockSpec((tm, n), lambda s: (tile(s), 0))
    row_prev = lambda n: pl.BlockSpec((tm, n), lambda s: (jnp.maximum(s - 1, 0), 0))
    headed = lambda nh: pl.BlockSpec(
        (1, nh, tm, LANES), lambda s: (tile(s) // tiles_per_seq, 0, tile(s) % tiles_per_seq, 0))
    headed_t = lambda nh: pl.BlockSpec(
        (1, nh, LANES, tm), lambda s: (tile(s) // tiles_per_seq, 0, 0, tile(s) % tiles_per_seq))
    k_shape = jax.ShapeDtypeStruct((batch, N_KV_HEADS, seq, LANES), BF16)
    vt_shape = jax.ShapeDtypeStruct((batch, N_KV_HEADS, LANES, seq), BF16)
    out_shape = (
        jax.ShapeDtypeStruct((t, d), BF16),
        jax.ShapeDtypeStruct((batch, N_HEADS, LANES, seq), BF16),
        jax.ShapeDtypeStruct((t // D_STRIDE, D_STRIDE * LANES), BF16),
        jax.ShapeDtypeStruct((t // D_STRIDE, D_STRIDE * LANES), BF16),
        k_shape, vt_shape, k_shape, vt_shape,
        jax.ShapeDtypeStruct((t, N_KV_HEADS * LANES), F32),
        jax.ShapeDtypeStruct((t, d), BF16),
    )
    blocks = pl.BlockSpec((tm // D_STRIDE, D_STRIDE * LANES), lambda s: (tile(s), 0))
    out_specs = (row_prev(d), headed_t(N_HEADS), blocks, blocks,
                 headed(N_KV_HEADS), headed_t(N_KV_HEADS), headed(N_KV_HEADS),
                 headed_t(N_KV_HEADS), row(N_KV_HEADS * LANES), row(d))
    kern = functools.partial(_inproj_kernel, tm=tm, n_tiles=n_tiles, tiles_per_seq=tiles_per_seq,
                             d_conv=d_conv)
    consts = (vec(norm_g), wu, wq, wkv, wng, wmg, cw, vec(conv_b), vec(ln_g), vec(ln_b), w_conv_out)
    return pl.pallas_call(
        kern, out_shape=out_shape, grid=(n_tiles + 1,),
        in_specs=[row(d)] + [full(w) for w in consts],
        out_specs=out_specs,
        scratch_shapes=[pltpu.VMEM((2, tm, LANES), F32),
                        pltpu.VMEM((tm + CONV_HALO, d_conv), F32),
                        pltpu.VMEM((SUBLANES - 1, tm + CONV_HALO - SUBLANES, d_conv), F32),
                        pltpu.VMEM((tm, d), BF16),
                        pltpu.VMEM((tm, d_conv), F32), pltpu.VMEM((tm, d), BF16)],
        compiler_params=_cparams(1), name="inproj",
    )(x2, *consts)


def _compress_kernel(rk_ref, rv_ref, pek_ref, pev_ref, w1k_ref, w1v_ref, w2k_ref, w2v_ref,
                     kcmp_ref, vcmp_ref, *, ncp):
    def one(r_ref, pe_ref, w1_ref, w2_ref):
        r = r_ref[0]
        top = _dot(r, w1_ref[0])
        bot = _dot(r, w1_ref[1])
        pe_h = _dot(pe_ref[0], w1_ref[0]) + _dot(pe_ref[1], w1_ref[1])
        nxt = pltpu.roll(bot, ncp - 1, 0)
        rowi = lax.broadcasted_iota(jnp.int32, top.shape, 0)
        hid = top + jnp.where(rowi == ncp - 1, 0.0, nxt) + pe_h[0:1, :]
        return _dot(_gelu(hid).astype(BF16), w2_ref[...])

    kcmp_ref[0] = one(rk_ref, pek_ref, w1k_ref, w2k_ref).astype(BF16)
    v = one(rv_ref, pev_ref, w1v_ref, w2v_ref)
    rowt = lax.broadcasted_iota(jnp.int32, (LANES, ncp), 0)
    ones_rows = (rowt >= HEAD_DIM) & (rowt < HEAD_DIM + 8)
    for g in range(N_KV_HEADS):
        vt = jnp.transpose(v[:, g * LANES:(g + 1) * LANES])
        vcmp_ref[0, g] = jnp.where(ones_rows, 1.0, vt).astype(BF16)


def _prep_compress_weights(pe, w1, w2):
    half = L_CMP // 2
    w1r = w1.reshape(L_CMP, HEAD_DIM, CMP_HIDDEN).astype(BF16)
    w2p = jnp.pad(w2, ((0, 0), (0, LANES - HEAD_DIM))).astype(BF16)

    def per_group(w, axis):
        z = jnp.zeros_like(w)
        return jnp.stack([jnp.concatenate([w if k == g else z for k in range(N_KV_HEADS)], axis=-1)
                          for g in range(N_KV_HEADS)], axis=axis)

    w1big = per_group(w1r, 1).reshape(2, half * N_KV_HEADS * HEAD_DIM, N_KV_HEADS * CMP_HIDDEN)
    w2big = per_group(w2p, 0).reshape(N_KV_HEADS * CMP_HIDDEN, N_KV_HEADS * LANES)
    per = pe.reshape(2, half, 1, HEAD_DIM)
    pebig = jnp.broadcast_to(per, (2, half, N_KV_HEADS, HEAD_DIM)).reshape(2, 1, -1)
    pebig = jnp.broadcast_to(pebig, (2, 8, pebig.shape[-1])).astype(BF16)
    return pebig, w1big, w2big


def _compress(kc, vc, wk, wv, batch, seq):
    ncp = seq // D_STRIDE
    width = D_STRIDE * LANES
    rk = kc.reshape(batch, ncp, width)
    rv = vc.reshape(batch, ncp, width)
    pek, w1k, w2k = wk
    pev, w1v, w2v = wv
    c3 = lambda a: pl.BlockSpec(a.shape, lambda b: (0, 0, 0))
    c2 = lambda a: pl.BlockSpec(a.shape, lambda b: (0, 0))
    rspec = pl.BlockSpec((1, ncp, width), lambda b: (b, 0, 0))
    kspec = pl.BlockSpec((1, ncp, N_KV_HEADS * LANES), lambda b: (b, 0, 0))
    kshape = jax.ShapeDtypeStruct((batch, ncp, N_KV_HEADS * LANES), BF16)
    vspec = pl.BlockSpec((1, N_KV_HEADS, LANES, ncp), lambda b: (b, 0, 0, 0))
    vshape = jax.ShapeDtypeStruct((batch, N_KV_HEADS, LANES, ncp), BF16)
    return pl.pallas_call(
        functools.partial(_compress_kernel, ncp=ncp), out_shape=(kshape, vshape), grid=(batch,),
        in_specs=[rspec, rspec, c3(pek), c3(pev), c3(w1k), c3(w1v), c2(w2k), c2(w2v)],
        out_specs=(kspec, vspec), compiler_params=_cparams(1), name="compress",
    )(rk, rv, pek, pev, w1k, w1v, w2k, w2v)


def _nsa_kernel(qt_ref, kcmp_ref, vcmpt_ref, biasc_ref, ks_ref, vst_ref, kw_ref, vwt_ref,
                ng_ref, aggt_ref, bd_ref, o_ref,
                imp_ref, qa_ref, oc_ref, ms_ref, accs_ref, mw_ref, accw_ref, *, ns):
    i = pl.program_id(1)
    heads = range(N_HEADS)

    def k_tile(ref, g, j):
        return ref[0, g, pl.ds(pl.multiple_of(j * TQ, TQ), TQ), :]

    def vt_tile(ref, g, j):
        return ref[0, g, 0:PV_ROWS, pl.ds(pl.multiple_of(j * TQ, TQ), TQ)]

    def pipelined(tasks, scores, update):
        pending = [scores(t) for t in tasks[:PIPE_DEPTH]]
        for t, task in enumerate(tasks):
            update(task, pending.pop(0))
            if t + PIPE_DEPTH < len(tasks):
                pending.append(scores(tasks[t + PIPE_DEPTH]))

    def run(tasks):
        def scores(task):
            hh, kt, _, qmat, _, _, row0 = task
            s = _dot(kt, qmat)
            if row0 is None:
                return s
            if not isinstance(row0, int):
                row0 = pl.multiple_of(row0, TQ)
            return s + bd_ref[hh, pl.ds(row0, TQ), :]

        def update(task, s):
            hh, _, vt, _, m_ref, acc_ref, _ = task
            m_old = m_ref[hh]
            m_new = jnp.maximum(m_old, jnp.max(s, axis=0, keepdims=True))
            p = jnp.exp2(s - m_new).astype(BF16)
            acc_ref[hh] = jnp.exp2(m_old - m_new) * acc_ref[hh] + _dot(vt, p)
            m_ref[hh] = m_new

        pipelined(tasks, scores, update)

    def slc_tasks(j, row0=None):
        tiles = [(k_tile(ks_ref, g, j), vt_tile(vst_ref, g, j)) for g in range(N_KV_HEADS)]
        return [(hh,) + tiles[hh // HPG] + (qa_ref[hh], ms_ref, accs_ref, row0) for hh in heads]

    def win_tasks(j, row0):
        tiles = [(k_tile(kw_ref, g, j), vt_tile(vwt_ref, g, j)) for g in range(N_KV_HEADS)]
        return [(hh,) + tiles[hh // HPG] + (qt_ref[0, hh], mw_ref, accw_ref, row0) for hh in heads]

    for hh in heads:
        ms_ref[hh] = jnp.full((1, TQ), M_INIT, F32)
        mw_ref[hh] = jnp.full((1, TQ), M_INIT, F32)
        accs_ref[hh] = jnp.zeros((PV_ROWS, TQ), F32)
        accw_ref[hh] = jnp.zeros((PV_ROWS, TQ), F32)

    p_parts = []

    def cmp_scores(hh):
        g = hh // HPG
        return _dot(kcmp_ref[0, :, g * LANES:(g + 1) * LANES], qt_ref[0, hh]) + biasc_ref[hh]

    def cmp_update(hh, s_c):
        m_c = jnp.maximum(jnp.max(s_c, axis=0, keepdims=True), M_INIT)
        p_c = jnp.exp2(s_c - m_c)
        l_c = jnp.sum(p_c, axis=0, keepdims=True)
        p_c = p_c * jnp.where(l_c > 0.0, 1.0 / l_c, 0.0)
        oc_ref[hh] = _dot(vcmpt_ref[0, hh // HPG, 0:HEAD_DIM, :], p_c.astype(BF16))
        p_parts.append(p_c)

    pipelined(list(heads), cmp_scores, cmp_update)

    blk = lax.broadcasted_iota(jnp.int32, (ns, TQ), 0)
    cur = (i * TQ + lax.broadcasted_iota(jnp.int32, (ns, TQ), 1)) // L_SLC
    forced = (blk == 0) | (blk == cur) | (blk == cur - 1)
    for g in range(N_KV_HEADS):
        p4 = p_parts[g * HPG:(g + 1) * HPG]
        p_sum = (p4[0] + p4[1]) + (p4[2] + p4[3])
        p_hi = p_sum.astype(BF16)
        p_lo = (p_sum - p_hi.astype(F32)).astype(BF16)
        imp = _dot(aggt_ref[...], p_hi) + _dot(aggt_ref[...], p_lo)
        imp_ref[g] = jnp.where(forced, FORCE, jnp.where(blk <= cur, imp, -FORCE))

    rows2 = jnp.where(i >= 2, 0, 3 * TQ)
    rows1 = jnp.where(i >= 1, TQ, 3 * TQ)
    run(win_tasks(jnp.maximum(i - 2, 0), rows2) + win_tasks(jnp.maximum(i - 1, 0), rows1)
        + win_tasks(i, 2 * TQ))

    blk8 =lax.broadcasted_iota(jnp.int32, (SUBLANES, TQ), 0)

    def block_flags(g, n_used):
        grp = [imp_ref[g, SUBLANES * r:SUBLANES * (r + 1), :] for r in range(n_used)]
        rank = [jnp.zeros((SUBLANES, TQ), F32) for _ in range(n_used)]
        for k in range(n_used * SUBLANES):
            rowk = imp_ref[g, k:k + 1, :]
            kg = k // SUBLANES
            for r in range(n_used):
                if r < kg:
                    beats = jnp.where(rowk > grp[r], 1.0, 0.0)
                elif r > kg:
                    beats = jnp.where(rowk >= grp[r], 1.0, 0.0)
                else:
                    beats = jnp.where(blk8 > k - SUBLANES * kg,
                                      jnp.where(rowk >= grp[r], 1.0, 0.0),
                                      jnp.where(rowk > grp[r], 1.0, 0.0))
                rank[r] = rank[r] + beats
        rank = jnp.concatenate(rank, axis=0)
        flags = jnp.where(rank < float(N_SEL), 0.0, -1.0).astype(BF16)
        pad = HEAD_DIM - n_used * SUBLANES
        if pad:
            flags = jnp.concatenate([flags, jnp.zeros((pad, TQ), BF16)], axis=0)
        return flags

    for hh in heads:
        qa_ref[hh, 0:HEAD_DIM, :] = qt_ref[0, hh, 0:HEAD_DIM, :]
    per_tile = TQ // L_SLC
    no_rank_tiles = N_SEL // per_tile

    @pl.when(i < no_rank_tiles)
    def _():
        for hh in heads:
            qa_ref[hh, HEAD_DIM:, :] = jnp.zeros((HEAD_DIM, TQ), BF16)

    lo = no_rank_tiles
    while lo * per_tile < ns:
        hi = min(lo + RANK_TILE_SPAN, ns // per_tile)
        n_used = -(-hi * per_tile // SUBLANES)

        @pl.when((i >= lo) & (i < hi))
        def _(n_used=n_used):
            for g in range(N_KV_HEADS):
                flags = block_flags(g, n_used)
                for hh in range(g * HPG, (g + 1) * HPG):
                    qa_ref[hh, HEAD_DIM:, :] = flags
        lo = hi

    n_far = jnp.maximum(i - 1, 0)

    def far_group(jj, carry):
        tasks = []
        for u in range(FAR_GROUP):
            tasks += slc_tasks(FAR_GROUP * jj + u)
        run(tasks)
        return carry

    lax.fori_loop(0, n_far // FAR_GROUP, far_group, 0)
    for rem in range(FAR_GROUP):
        @pl.when(n_far % FAR_GROUP == rem)
        def _(rem=rem):
            tasks = []
            for u in range(rem):
                tasks += slc_tasks(n_far - rem + u)
            run(tasks + slc_tasks(jnp.maximum(i - 1, 0), rows1) + slc_tasks(i, 2 * TQ))

    gates_t = jnp.transpose(ng_ref[...])
    outs = []
    for hh in heads:
        a_s, a_w = accs_ref[hh], accw_ref[hh]
        o_s = a_s[0:HEAD_DIM] * (1.0 / a_s[HEAD_DIM:HEAD_DIM + 1])
        o_w = a_w[0:HEAD_DIM] * (1.0 / a_w[HEAD_DIM:HEAD_DIM + 1])
        row = (hh // HPG) * LANES + 3 * (hh % HPG)
        outs.append(gates_t[row:row + 1] * oc_ref[hh] + gates_t[row + 1:row + 2] * o_s
                    + gates_t[row + 2:row + 3] * o_w)
    o_ref[...] = jnp.transpose(jnp.concatenate(outs, axis=0)).astype(BF16)


def _bias_of(thr_ref, rb_ref, dist, h):
    b = jnp.full(dist.shape, rb_ref[h], F32)
    for k in range(1, N_BUCKETS):
        b = jnp.where(dist >= thr_ref[k], rb_ref[k * N_HEADS + h], b)
    return b


def _cmp_bias_kernel(thr_ref, rb_ref, biasc_ref, *, nc, ncp):
    i = pl.program_id(0)
    per_tile = TQ // D_STRIDE
    band = 2 * per_tile
    assert TQ + D_STRIDE - (L_CMP - 1) >= MAX_DISTANCE

    def dist_of(c0, rows):
        c = c0 + lax.broadcasted_iota(jnp.int32, (rows, TQ), 0)
        r = lax.broadcasted_iota(jnp.int32, (rows, TQ), 1)
        dist = i * TQ + r - (c * D_STRIDE + L_CMP - 1)
        return dist, (dist >= 0) & (c < nc)

    _, ok_all = dist_of(0, ncp)
    band0 = pl.multiple_of(jnp.maximum(i * per_tile - per_tile, 0), per_tile)
    dist_b, ok_b = dist_of(band0, band)
    for h in range(N_HEADS):
        biasc_ref[h] = jnp.where(ok_all, rb_ref[(N_BUCKETS - 1) * N_HEADS + h], NEG)
        biasc_ref[h, pl.ds(band0, band), :] = jnp.where(
            ok_b, _bias_of(thr_ref, rb_ref, dist_b, h), NEG)


def _diag_bias_kernel(thr_ref, rb_ref, bd_ref):
    d0 = (lax.broadcasted_iota(jnp.int32, (TQ, TQ), 1)
          - lax.broadcasted_iota(jnp.int32, (TQ, TQ), 0))
    for h in range(N_HEADS):
        far = rb_ref[(N_BUCKETS - 1) * N_HEADS + h]
        bd_ref[h, 0:TQ, :] = jnp.where(d0 < 0, _bias_of(thr_ref, rb_ref, d0 + 2 * TQ, h) - far, NEG)
        bd_ref[h, TQ:2 * TQ, :] = _bias_of(thr_ref, rb_ref, d0 + TQ, h) - far
        bd_ref[h, 2 * TQ:3 * TQ, :] = jnp.where(d0 >= 0, _bias_of(thr_ref, rb_ref, d0, h) - far, NEG)
        bd_ref[h, 3 * TQ:, :] = jnp.full((TQ, TQ), NEG, F32)


def _attention_tables(rel_bias, seq):
    ncp = seq // D_STRIDE
    nc = (seq - L_CMP) // D_STRIDE + 1
    ns = seq // L_SLC
    nt = seq // TQ
    n_probe = 2 * MAX_DISTANCE
    buckets = _t5_bucket(jnp.arange(n_probe))
    thr = jnp.sum(buckets[None, :] < jnp.arange(N_BUCKETS)[:, None], axis=1).astype(jnp.int32)
    assert WINDOW == 2 * TQ
    rb = (rel_bias.astype(F32) * LOG2E).reshape(-1)
    bias_c = pl.pallas_call(
        functools.partial(_cmp_bias_kernel, nc=nc, ncp=ncp),
        out_shape=jax.ShapeDtypeStruct((N_HEADS, ncp, seq), F32),
        grid_spec=pltpu.PrefetchScalarGridSpec(
            num_scalar_prefetch=2, grid=(nt,), in_specs=[],
            out_specs=pl.BlockSpec((N_HEADS, ncp, TQ), lambda i, *_: (0, 0, i))),
        compiler_params=_cparams(1), name="cmp_bias")(thr, rb)
    bd = pl.pallas_call(
        _diag_bias_kernel, out_shape=jax.ShapeDtypeStruct((N_HEADS, 4 * TQ, TQ), F32),
        grid_spec=pltpu.PrefetchScalarGridSpec(
            num_scalar_prefetch=2, grid=(1,), in_specs=[],
            out_specs=pl.BlockSpec((N_HEADS, 4 * TQ, TQ), lambda i, *_: (0, 0, 0))),
        compiler_params=_cparams(1), name="diag_bias")(thr, rb)
    c_start = jnp.arange(ncp) * D_STRIDE
    c_end = c_start + L_CMP - 1
    s_start = jnp.arange(ns) * L_SLC
    aggt = ((c_end[None, :] >= s_start[:, None]) & (c_start[None, :] <= s_start[:, None] + L_SLC - 1)
            & (jnp.arange(ncp)[None, :] < nc)).astype(BF16)
    return bias_c, bd, aggt


def _nsa(qt, kcmp, vcmpt, ks, vst, kw, vwt, ng, rel_bias, batch, seq):
    first_far = TQ + 1
    assert MAX_EXACT + int(math.log(first_far / MAX_EXACT) / math.log(MAX_DISTANCE / MAX_EXACT)
                           * (N_BUCKETS - MAX_EXACT)) >= N_BUCKETS - 1, "far tiles need one bucket"
    ncp = seq // D_STRIDE
    ns = seq // L_SLC
    nt = seq // TQ
    bias_c, bd, aggt = _attention_tables(rel_bias, seq)
    kspec = pl.BlockSpec((1, N_KV_HEADS, seq, LANES), lambda b, i: (b, 0, 0, 0))
    vtspec = pl.BlockSpec((1, N_KV_HEADS, LANES, seq), lambda b, i: (b, 0, 0, 0))
    state = lambda rows: pltpu.VMEM((N_HEADS, rows, TQ), F32)
    const = lambda shape: pl.BlockSpec(shape, lambda b, i: (0,) * len(shape),
                                       pipeline_mode=pl.Buffered(1))
    grid_spec = pltpu.PrefetchScalarGridSpec(
        num_scalar_prefetch=0, grid=(batch, nt),
        in_specs=[pl.BlockSpec((1, N_HEADS, LANES, TQ), lambda b, i: (b, 0, 0, i)),
                  pl.BlockSpec((1, ncp, N_KV_HEADS * LANES), lambda b, i: (b, 0, 0)),
                  pl.BlockSpec((1, N_KV_HEADS, LANES, ncp), lambda b, i: (b, 0, 0, 0)),
                  pl.BlockSpec((N_HEADS, ncp, TQ), lambda b, i: (0, 0, i)),
                  kspec, vtspec, kspec, vtspec,
                  pl.BlockSpec((TQ, N_KV_HEADS * LANES), lambda b, i: (b * nt + i, 0)),
                  const((ns, ncp)), const((N_HEADS, 4 * TQ, TQ))],
        out_specs=pl.BlockSpec((TQ, N_HEADS * HEAD_DIM), lambda b, i: (b * nt + i, 0)),
        scratch_shapes=[pltpu.VMEM((N_KV_HEADS, ns, TQ), F32),
                        pltpu.VMEM((N_HEADS, LANES, TQ), BF16),
                        state(HEAD_DIM),
                        state(1), state(PV_ROWS),
                        state(1), state(PV_ROWS)])
    return pl.pallas_call(
        functools.partial(_nsa_kernel, ns=ns),
        out_shape=jax.ShapeDtypeStruct((batch * seq, N_HEADS * HEAD_DIM), BF16),
        grid_spec=grid_spec, compiler_params=_cparams(2), name="nsa",
    )(qt, kcmp, vcmpt, bias_c, ks, vst, kw, vwt, ng, aggt, bd)


def _ffn_kernel(x_ref, ya_ref, o_ref, gb_ref, p_ref, wao_ref, wout_ref, nf_ref, wup_ref, cw_ref,
                cb_ref, wd_ref, np_ref, wpg_ref, wpe_ref, nfin_ref, out_ref, carry_ref,
                *, tm, tiles_per_seq, d_ff, chunks):
    r = pl.program_id(0)

    @pl.when((r % tiles_per_seq) == 0)
    def _():
        carry_ref[...] = jnp.zeros_like(carry_ref)

    n = tm // FFN_SPLIT
    groups = [slice(k * n, (k + 1) * n) for k in range(FFN_SPLIT)]

    x1, hf = [], []
    for rows in groups:
        y = (ya_ref[rows, :].astype(F32)
             + gb_ref[rows, :].astype(F32) * _dot(o_ref[rows, :], wao_ref[...]))
        x1.append(x_ref[rows, :] + _dot(y.astype(BF16), wout_ref[...]))
        hf.append(_rms(x1[-1], nf_ref[...]).astype(BF16))

    def conv3(up, prev, c0, width):
        rowi = lax.broadcasted_iota(jnp.int32, up.shape, 0)
        s1 = jnp.where(rowi == 0, prev[7:8, :], pltpu.roll(up, 1, 0))
        s2 = jnp.where(rowi == 0, prev[6:7, :],
                       jnp.where(rowi == 1, prev[7:8, :], pltpu.roll(up, 2, 0)))
        return (cw_ref[0:1, c0:c0 + width] * s2 + cw_ref[1:2, c0:c0 + width] * s1
                + cw_ref[2:3, c0:c0 + width] * up + cb_ref[:, c0:c0 + width])

    acc = [None] * FFN_SPLIT
    for c0, width in chunks:
        cols = (slice(c0, c0 + width), slice(d_ff + c0, d_ff + c0 + width))
        prev = [carry_ref[:, cs] for cs in cols]
        for k in range(FFN_SPLIT):
            ups = [_dot(hf[k], wup_ref[:, cs]) for cs in cols]
            gate, val = [conv3(ups[t], prev[t], cols[t].start, width) for t in range(2)]
            prev = [u[n - 8:, :] for u in ups]
            part = _dot((_gelu(gate) * val).astype(BF16), wd_ref[c0:c0 + width, :])
            acc[k] = part if acc[k] is None else acc[k] + part
        for t in range(2):
            carry_ref[:, cols[t]] = prev[t]

    for k, rows in enumerate(groups):
        x2 = x1[k] + acc[k]
        pg = _sigmoid(_dot(_rms(x2, np_ref[...]).astype(BF16), wpg_ref[...]))
        pe = _dot(p_ref[rows, :].astype(BF16), wpe_ref[...])
        out_ref[rows, :] = _rms(x2 + pg * pe, nfin_ref[...])


def _ffn(x2, ya, o, gb, p2, w_attn_out, w_out, norm_ffn, w_up, ffn_dw_w, ffn_dw_b, w_down,
         norm_ple, w_ple_gate, w_ple, norm_final, seq, p_tile0=0, tm=512):
    t, d = x2.shape
    d_ff = w_down.shape[0]
    assert d_ff % MXU_DEPTH == 0
    chunks, c0 = [], 0
    while c0 < d_ff:
        width = min(FFN_CHUNK, d_ff - c0)
        chunks.append((c0, width))
        c0 += width
    tiles_per_seq = seq // tm
    cw = jnp.concatenate([ffn_dw_w, jnp.zeros((8 - FFN_CONV_WIDTH, 2 * d_ff), ffn_dw_w.dtype)], 0)
    vec = lambda v: v.reshape(1, -1)
    const = lambda a: pl.BlockSpec(a.shape, lambda r: (0, 0), pipeline_mode=pl.Buffered(1))
    rows = lambda n: pl.BlockSpec((tm, n), lambda r: (r, 0))
    kern = functools.partial(_ffn_kernel, tm=tm, tiles_per_seq=tiles_per_seq, d_ff=d_ff,
                             chunks=tuple(chunks))
    operands = (x2, ya, o, gb, p2, w_attn_out, w_out, vec(norm_ffn), w_up, cw, vec(ffn_dw_b),
                w_down, vec(norm_ple), w_ple_gate, w_ple, vec(norm_final))
    return pl.pallas_call(
        kern, out_shape=jax.ShapeDtypeStruct((t, d), F32), grid=(t // tm,),
        in_specs=[rows(d), rows(d), rows(o.shape[1]), rows(d),
                  pl.BlockSpec((tm, p2.shape[1]), lambda r: (p_tile0 + r, 0))]
        + [const(a) for a in operands[5:]],
        out_specs=rows(d),
        scratch_shapes=[pltpu.VMEM((8, 2 * d_ff), F32)],
        compiler_params=_cparams(1), name="ffn",
    )(*operands)


def kernel(x, p, rel_bias, norm_mix, w_in, conv_dw_w, conv_dw_b, conv_ln_g, conv_ln_b, w_conv_out,
           cmp_pe_k, cmp_pe_v, w_ck1, w_ck2, w_cv1, w_cv2, w_attn_out, w_out, norm_ffn, w_up,
           ffn_dw_w, ffn_dw_b, w_down, norm_ple, w_ple_gate, w_ple, norm_final):
    batch, seq, d = x.shape
    depth = w_in.shape[0]
    x2 = x.reshape(batch * seq, d)
    for i in range(depth):
        ya, qt, kc, vc, ks, vst, kw, vwt, ng, gb = _inproj(
            x2, norm_mix[i], _prep_inproj_weights(w_in[i], d), conv_dw_w[i], conv_dw_b[i],
            conv_ln_g[i], conv_ln_b[i], w_conv_out[i].astype(BF16), batch, seq)
        kcmp, vcmpt = _compress(kc, vc,
                               _prep_compress_weights(cmp_pe_k[i], w_ck1[i], w_ck2[i]),
                               _prep_compress_weights(cmp_pe_v[i], w_cv1[i], w_cv2[i]),
                               batch, seq)
        o = _nsa(qt, kcmp, vcmpt, ks, vst, kw, vwt, ng, rel_bias, batch, seq)
        assert i == depth - 1, "the final RMSNorm is fused into the (single) layer's MLP kernel"
        ffn_tm = 512
        x2 = _ffn(x2, ya, o, gb, p.reshape(depth * batch * seq, -1), w_attn_out[i].astype(BF16),
                  w_out[i].astype(BF16), norm_ffn[i], w_up[i].astype(BF16), ffn_dw_w[i],
                  ffn_dw_b[i], w_down[i].astype(BF16), norm_ple[i], w_ple_gate[i].astype(BF16),
                  w_ple[i].astype(BF16), norm_final, seq,
                  p_tile0=i * (batch * seq // ffn_tm), tm=ffn_tm)
    return x2.reshape(batch, seq, d)
```

```python
import functools
import math

import jax
import jax.numpy as jnp
from jax import lax
from jax.experimental import pallas as pl
from jax.experimental.pallas import tpu as pltpu

N_HEADS = 8
HEAD_DIM = 64
N_KV_HEADS = 2
HPG = N_HEADS // N_KV_HEADS
L_CMP = 32
D_STRIDE = 16
CMP_HIDDEN = 256
L_SLC = 64
N_SEL = 16
WINDOW = 512
N_BUCKETS = 32
MAX_EXACT = N_BUCKETS // 2
MAX_DISTANCE = 128
CONV_WIDTH = 31
FFN_CONV_WIDTH = 3
EPS = 1e-6
FORCE = 1e4

LANES = 128
MXU_DEPTH = 256
FFN_CHUNK = 6 * MXU_DEPTH
FFN_SPLIT = 2
NEG = -1e30
M_INIT = -1e29
LOG2E = 1.4426950408889634
PIPE_DEPTH = 6
RANK_TILE_SPAN = 4
FAR_GROUP = 8
TQ = 256
PV_ROWS = HEAD_DIM + 16
SUBLANES = 8
CONV_HALO = 32
CONV_ROWS = 64
VMEM_LIMIT = 56 * 1024 * 1024

F32 = jnp.float32
BF16 = jnp.bfloat16


def _cparams(n_axes):
    return pltpu.CompilerParams(dimension_semantics=("arbitrary",) * n_axes,
                                vmem_limit_bytes=VMEM_LIMIT)


def _dot(a, b):
    return jnp.dot(a, b, preferred_element_type=F32)


def _rms(xf, g):
    return xf * lax.rsqrt(jnp.mean(xf * xf, axis=-1, keepdims=True) + EPS) * g


def _sigmoid(x):
    return 1.0 / (1.0 + jnp.exp(-x))


def _gelu(x):
    return 0.5 * x * (1.0 + jnp.tanh(0.7978845608028654 * (x + 0.044715 * x * x * x)))


def _t5_bucket(dist):
    n = jnp.maximum(dist, 0)
    nf = jnp.maximum(n, MAX_EXACT).astype(F32)
    large = MAX_EXACT + (jnp.log(nf / MAX_EXACT) / math.log(MAX_DISTANCE / MAX_EXACT)
                         * (N_BUCKETS - MAX_EXACT)).astype(jnp.int32)
    large = jnp.minimum(large, N_BUCKETS - 1)
    return jnp.where(n < MAX_EXACT, n, large)


def _inproj_kernel(x_ref, g_ref, wu_ref, wq_ref, wkv_ref, wng_ref, wmg_ref,
                   cw_ref, cb_ref, lg_ref, lb_ref, wco_ref,
                   ya_ref, qt_ref, kc_ref, vc_ref, ks_ref, vst_ref, kw_ref, vwt_ref, ng_ref, gb_ref,
                   cs_ref, ext_ref, sh_ref, ga_ref, a_stage_ref, ga_stage_ref,
                   *, tm, n_tiles, tiles_per_seq, d_conv):
    step = pl.program_id(0)
    r = jnp.minimum(step, n_tiles - 1)

    @pl.when(step == 0)
    def _():
        ext_ref[...] = jnp.zeros(ext_ref.shape, F32)
        ga_ref[...] = jnp.zeros(ga_ref.shape, BF16)

    span = sh_ref.shape[1]
    for ph in range(1, SUBLANES):
        sh_ref[ph - 1] = ext_ref[ph:ph + span, :]
    off = CONV_HALO - (CONV_WIDTH - 1)
    conv = []
    for r0 in range(0, tm, CONV_ROWS):
        acc = jnp.zeros((CONV_ROWS, d_conv), F32) + cb_ref[...]
        for j in range(CONV_WIDTH):
            ph = (off + j) % SUBLANES
            base = r0 + off + j - ph
            if ph == 0:
                xs = ext_ref[base:base + CONV_ROWS, :]
            else:
                xs = sh_ref[ph - 1, base:base + CONV_ROWS, :]
            acc = acc + jnp.tile(cw_ref[j], (CONV_ROWS // SUBLANES, 1)) * xs
        conv.append(acc)
    conv = jnp.concatenate(conv, axis=0)
    mu = jnp.mean(conv, axis=-1, keepdims=True)
    cen = conv - mu
    var = jnp.mean(cen * cen, axis=-1, keepdims=True)
    y = cen * lax.rsqrt(var + EPS) * lg_ref[...] + lb_ref[...]
    y = (y * _sigmoid(y)).astype(BF16)

    h = _rms(x_ref[...], g_ref[...]).astype(BF16)
    u = _dot(h, wu_ref[...])
    a_stage_ref[...] = u[:, :d_conv] * _sigmoid(u[:, d_conv:])
    zmg = _dot(h, wmg_ref[...])
    d_model = zmg.shape[1] // 2
    ga_stage_ref[...] = _sigmoid(zmg[:, :d_model]).astype(BF16)
    gb_ref[...] = _sigmoid(zmg[:, d_model:]).astype(BF16)

    zqt = jnp.transpose(_dot(h, wq_ref[...]))
    zero_rows = jnp.zeros((HEAD_DIM, tm), BF16)
    for hh in range(N_HEADS):
        qt_ref[0, hh, 0:HEAD_DIM, :] = zqt[hh * HEAD_DIM:(hh + 1) * HEAD_DIM].astype(BF16)
        qt_ref[0, hh, HEAD_DIM:, :] = zero_rows

    zkv = _dot(h, wkv_ref[...])
    for which, out_ref in enumerate((kc_ref, vc_ref)):
        cs_ref[which] = zkv[:, which * LANES:(which + 1) * LANES]
        for tok in range(D_STRIDE):
            rows = cs_ref[which, pl.ds(tok, tm // D_STRIDE, stride=D_STRIDE), :]
            out_ref[:, tok * LANES:(tok + 1) * LANES] = rows.astype(BF16)
    lane = lax.broadcasted_iota(jnp.int32, (tm, LANES), 1)
    row = lax.broadcasted_iota(jnp.int32, (tm, LANES), 0)
    spos = (r % tiles_per_seq) * tm + row
    lo = lane < HEAD_DIM
    blk_tag = jnp.where((lane - HEAD_DIM) == spos // L_SLC, -NEG, 0.0)
    ones_rows = jnp.where(lax.broadcasted_iota(jnp.int32, (HEAD_DIM, tm), 0) < 8, 1.0, 0.0)
    ones_rows = ones_rows.astype(BF16)
    for g in range(N_KV_HEADS):
        for k_ref, vt_ref, base, tag in ((ks_ref, vst_ref, (2 + g) * LANES, blk_tag),
                                         (kw_ref, vwt_ref, (4 + g) * LANES, 0.0)):
            pair = zkv[:, base:base + LANES]
            k_ref[0, g] = jnp.where(lo, pair, tag).astype(BF16)
            vt_ref[0, g, 0:HEAD_DIM, :] = jnp.transpose(pair)[HEAD_DIM:].astype(BF16)
            vt_ref[0, g, HEAD_DIM:, :] = ones_rows

    ng_ref[...] = _sigmoid(_dot(h, wng_ref[...]))

    ya_ref[...] = (ga_ref[...].astype(F32) * _dot(y, wco_ref[...])).astype(BF16)

    seq_start = (r % tiles_per_seq) == 0
    ext_ref[0:CONV_HALO, :] = jnp.where(seq_start, 0.0, ext_ref[tm:tm + CONV_HALO, :])
    ext_ref[CONV_HALO:, :] = a_stage_ref[...]
    ga_ref[...] = ga_stage_ref[...]


def _prep_inproj_weights(w_in, d_model):
    d_conv = d_model // 2
    n_conv = 2 * d_conv
    n_q = N_HEADS * HEAD_DIM
    n_kv = 6 * N_KV_HEADS * HEAD_DIM
    n_ng = 3 * N_HEADS
    o = 0
    wu = w_in[:, o:o + n_conv]; o += n_conv
    wq = w_in[:, o:o + n_q]; o += n_q
    wkv = w_in[:, o:o + n_kv]; o += n_kv
    wng = w_in[:, o:o + n_ng]; o += n_ng
    wmg = w_in[:, o:]
    wq_p = wq * (LOG2E / math.sqrt(HEAD_DIM))
    kvcols = [wkv[:, 0:LANES], wkv[:, LANES:2 * LANES]]
    for k_kind in (2, 4):
        for g in range(N_KV_HEADS):
            for kind in (k_kind, k_kind + 1):
                c0 = kind * N_KV_HEADS * HEAD_DIM + g * HEAD_DIM
                kvcols.append(wkv[:, c0:c0 + HEAD_DIM])
    wkv_p = jnp.concatenate(kvcols, axis=1)
    per_g = HPG * 3
    ngcols = []
    for g in range(N_KV_HEADS):
        ngcols += [wng[:, g * per_g:(g + 1) * per_g],
                   jnp.zeros((d_model, LANES - per_g), w_in.dtype)]
    wng_p = jnp.concatenate(ngcols, axis=1)
    return tuple(w.astype(BF16) for w in (wu, wq_p, wkv_p, wng_p, wmg))


def _inproj(x2, norm_g, weights, conv_w, conv_b, ln_g, ln_b, w_conv_out, batch, seq, tm=512):
    t, d = x2.shape
    wu, wq, wkv, wng, wmg = weights
    d_conv = d // 2
    tiles_per_seq = seq // tm
    n_tiles = t // tm
    assert tm % CONV_ROWS == 0 and CONV_HALO >= CONV_WIDTH - 1 and CONV_HALO % SUBLANES == 0
    cw = jnp.broadcast_to(conv_w[:, None, :], (CONV_WIDTH, SUBLANES, d_conv))
    vec = lambda v: v.reshape(1, -1)
    full = lambda w: pl.BlockSpec(w.shape, lambda s: (0,) * w.ndim, pipeline_mode=pl.Buffered(1))
    tile = lambda s: jnp.minimum(s, n_tiles - 1)
    row = lambda n: pl.BlockSpec((tm, n), lambda s: (tile(s), 0))
    row_prev = lambda n: pl.BlockSpec((tm, n), lambda s: (jnp.maximum(s - 1, 0), 0))
    headed = lambda nh: pl.BlockSpec(
        (1, nh, tm, LANES), lambda s: (tile(s) // tiles_per_seq, 0, tile(s) % tiles_per_seq, 0))
    headed_t = lambda nh: pl.BlockSpec(
        (1, nh, LANES, tm), lambda s: (tile(s) // tiles_per_seq, 0, 0, tile(s) % tiles_per_seq))
    k_shape = jax.ShapeDtypeStruct((batch, N_KV_HEADS, seq, LANES), BF16)
    vt_shape = jax.ShapeDtypeStruct((batch, N_KV_HEADS, LANES, seq), BF16)
    out_shape = (
        jax.ShapeDtypeStruct((t, d), BF16),
        jax.ShapeDtypeStruct((batch, N_HEADS, LANES, seq), BF16),
        jax.ShapeDtypeStruct((t // D_STRIDE, D_STRIDE * LANES), BF16),
        jax.ShapeDtypeStruct((t // D_STRIDE, D_STRIDE * LANES), BF16),
        k_shape, vt_shape, k_shape, vt_shape,
        jax.ShapeDtypeStruct((t, N_KV_HEADS * LANES), F32),
        jax.ShapeDtypeStruct((t, d), BF16),
    )
    blocks = pl.BlockSpec((tm // D_STRIDE, D_STRIDE * LANES), lambda s: (tile(s), 0))
    out_specs = (row_prev(d), headed_t(N_HEADS), blocks, blocks,
                 headed(N_KV_HEADS), headed_t(N_KV_HEADS), headed(N_KV_HEADS),
                 headed_t(N_KV_HEADS), row(N_KV_HEADS * LANES), row(d))
    kern = functools.partial(_inproj_kernel, tm=tm, n_tiles=n_tiles, tiles_per_seq=tiles_per_seq,
                             d_conv=d_conv)
    consts = (vec(norm_g), wu, wq, wkv, wng, wmg, cw, vec(conv_b), vec(ln_g), vec(ln_b), w_conv_out)
    return pl.pallas_call(
        kern, out_shape=out_shape, grid=(n_tiles + 1,),
        in_specs=[row(d)] + [full(w) for w in consts],
        out_specs=out_specs,
        scratch_shapes=[pltpu.VMEM((2, tm, LANES), F32),
                        pltpu.VMEM((tm + CONV_HALO, d_conv), F32),
                        pltpu.VMEM((SUBLANES - 1, tm + CONV_HALO - SUBLANES, d_conv), F32),
                        pltpu.VMEM((tm, d), BF16),
                        pltpu.VMEM((tm, d_conv), F32), pltpu.VMEM((tm, d), BF16)],
        compiler_params=_cparams(1), name="inproj",
    )(x2, *consts)


def _compress_kernel(rk_ref, rv_ref, pek_ref, pev_ref, w1k_ref, w1v_ref, w2k_ref, w2v_ref,
                     kcmp_ref, vcmp_ref, *, ncp):
    def one(r_ref, pe_ref, w1_ref, w2_ref):
        r = r_ref[0]
        top = _dot(r, w1_ref[0])
        bot = _dot(r, w1_ref[1])
        pe_h = _dot(pe_ref[0], w1_ref[0]) + _dot(pe_ref[1], w1_ref[1])
        nxt = pltpu.roll(bot, ncp - 1, 0)
        rowi = lax.broadcasted_iota(jnp.int32, top.shape, 0)
        hid = top + jnp.where(rowi == ncp - 1, 0.0, nxt) + pe_h[0:1, :]
        return _dot(_gelu(hid).astype(BF16), w2_ref[...])

    kcmp_ref[0] = one(rk_ref, pek_ref, w1k_ref, w2k_ref).astype(BF16)
    v = one(rv_ref, pev_ref, w1v_ref, w2v_ref)
    rowt = lax.broadcasted_iota(jnp.int32, (LANES, ncp), 0)
    ones_rows = (rowt >= HEAD_DIM) & (rowt < HEAD_DIM + 8)
    for g in range(N_KV_HEADS):
        vt = jnp.transpose(v[:, g * LANES:(g + 1) * LANES])
        vcmp_ref[0, g] = jnp.where(ones_rows, 1.0, vt).astype(BF16)


def _prep_compress_weights(pe, w1, w2):
    half = L_CMP // 2
    w1r = w1.reshape(L_CMP, HEAD_DIM, CMP_HIDDEN).astype(BF16)
    w2p = jnp.pad(w2, ((0, 0), (0, LANES - HEAD_DIM))).astype(BF16)

    def per_group(w, axis):
        z = jnp.zeros_like(w)
        return jnp.stack([jnp.concatenate([w if k == g else z for k in range(N_KV_HEADS)], axis=-1)
                          for g in range(N_KV_HEADS)], axis=axis)

    w1big = per_group(w1r, 1).reshape(2, half * N_KV_HEADS * HEAD_DIM, N_KV_HEADS * CMP_HIDDEN)
    w2big = per_group(w2p, 0).reshape(N_KV_HEADS * CMP_HIDDEN, N_KV_HEADS * LANES)
    per = pe.reshape(2, half, 1, HEAD_DIM)
    pebig = jnp.broadcast_to(per, (2, half, N_KV_HEADS, HEAD_DIM)).reshape(2, 1, -1)
    pebig = jnp.broadcast_to(pebig, (2, 8, pebig.shape[-1])).astype(BF16)
    return pebig, w1big, w2big


def _compress(kc, vc, wk, wv, batch, seq):
    ncp = seq // D_STRIDE
    width = D_STRIDE * LANES
    rk = kc.reshape(batch, ncp, width)
    rv = vc.reshape(batch, ncp, width)
    pek, w1k, w2k = wk
    pev, w1v, w2v = wv
    c3 = lambda a: pl.BlockSpec(a.shape, lambda b: (0, 0, 0))
    c2 = lambda a: pl.BlockSpec(a.shape, lambda b: (0, 0))
    rspec = pl.BlockSpec((1, ncp, width), lambda b: (b, 0, 0))
    kspec = pl.BlockSpec((1, ncp, N_KV_HEADS * LANES), lambda b: (b, 0, 0))
    kshape = jax.ShapeDtypeStruct((batch, ncp, N_KV_HEADS * LANES), BF16)
    vspec = pl.BlockSpec((1, N_KV_HEADS, LANES, ncp), lambda b: (b, 0, 0, 0))
    vshape = jax.ShapeDtypeStruct((batch, N_KV_HEADS, LANES, ncp), BF16)
    return pl.pallas_call(
        functools.partial(_compress_kernel, ncp=ncp), out_shape=(kshape, vshape), grid=(batch,),
        in_specs=[rspec, rspec, c3(pek), c3(pev), c3(w1k), c3(w1v), c2(w2k), c2(w2v)],
        out_specs=(kspec, vspec), compiler_params=_cparams(1), name="compress",
    )(rk, rv, pek, pev, w1k, w1v, w2k, w2v)


def _nsa_kernel(qt_ref, kcmp_ref, vcmpt_ref, biasc_ref, ks_ref, vst_ref, kw_ref, vwt_ref,
                ng_ref, aggt_ref, bd_ref, o_ref,
                imp_ref, qa_ref, oc_ref, ms_ref, accs_ref, mw_ref, accw_ref, *, ns):
    i = pl.program_id(1)
    heads = range(N_HEADS)

    def k_tile(ref, g, j):
        return ref[0, g, pl.ds(pl.multiple_of(j * TQ, TQ), TQ), :]

    def vt_tile(ref, g, j):
        return ref[0, g, 0:PV_ROWS, pl.ds(pl.multiple_of(j * TQ, TQ), TQ)]

    def pipelined(tasks, scores, update):
        pending = [scores(t) for t in tasks[:PIPE_DEPTH]]
        for t, task in enumerate(tasks):
            update(task, pending.pop(0))
            if t + PIPE_DEPTH < len(tasks):
                pending.append(scores(tasks[t + PIPE_DEPTH]))

    def run(tasks):
        def scores(task):
            hh, kt, _, qmat, _, _, row0 = task
            s = _dot(kt, qmat)
            if row0 is None:
                return s
            if not isinstance(row0, int):
                row0 = pl.multiple_of(row0, TQ)
            return s + bd_ref[hh, pl.ds(row0, TQ), :]

        def update(task, s):
            hh, _, vt, _, m_ref, acc_ref, _ = task
            m_old = m_ref[hh]
            m_new = jnp.maximum(m_old, jnp.max(s, axis=0, keepdims=True))
            p = jnp.exp2(s - m_new).astype(BF16)
            acc_ref[hh] = jnp.exp2(m_old - m_new) * acc_ref[hh] + _dot(vt, p)
            m_ref[hh] = m_new

        pipelined(tasks, scores, update)

    def slc_tasks(j, row0=None):
        tiles = [(k_tile(ks_ref, g, j), vt_tile(vst_ref, g, j)) for g in range(N_KV_HEADS)]
        return [(hh,) + tiles[hh // HPG] + (qa_ref[hh], ms_ref, accs_ref, row0) for hh in heads]

    def win_tasks(j, row0):
        tiles = [(k_tile(kw_ref, g, j), vt_tile(vwt_ref, g, j)) for g in range(N_KV_HEADS)]
        return [(hh,) + tiles[hh // HPG] + (qt_ref[0, hh], mw_ref, accw_ref, row0) for hh in heads]

    for hh in heads:
        ms_ref[hh] = jnp.full((1, TQ), M_INIT, F32)
        mw_ref[hh] = jnp.full((1, TQ), M_INIT, F32)
        accs_ref[hh] = jnp.zeros((PV_ROWS, TQ), F32)
        accw_ref[hh] = jnp.zeros((PV_ROWS, TQ), F32)

    p_parts = []

    def cmp_scores(hh):
        g = hh // HPG
        return _dot(kcmp_ref[0, :, g * LANES:(g + 1) * LANES], qt_ref[0, hh]) + biasc_ref[hh]

    def cmp_update(hh, s_c):
        m_c = jnp.maximum(jnp.max(s_c, axis=0, keepdims=True), M_INIT)
        p_c = jnp.exp2(s_c - m_c)
        l_c = jnp.sum(p_c, axis=0, keepdims=True)
        p_c = p_c * jnp.where(l_c > 0.0, 1.0 / l_c, 0.0)
        oc_ref[hh] = _dot(vcmpt_ref[0, hh // HPG, 0:HEAD_DIM, :], p_c.astype(BF16))
        p_parts.append(p_c)

    pipelined(list(heads), cmp_scores, cmp_update)

    blk = lax.broadcasted_iota(jnp.int32, (ns, TQ), 0)
    cur = (i * TQ + lax.broadcasted_iota(jnp.int32, (ns, TQ), 1)) // L_SLC
    forced = (blk == 0) | (blk == cur) | (blk == cur - 1)
    for g in range(N_KV_HEADS):
        p4 = p_parts[g * HPG:(g + 1) * HPG]
        p_sum = (p4[0] + p4[1]) + (p4[2] + p4[3])
        p_hi = p_sum.astype(BF16)
        p_lo = (p_sum - p_hi.astype(F32)).astype(BF16)
        imp = _dot(aggt_ref[...], p_hi) + _dot(aggt_ref[...], p_lo)
        imp_ref[g] = jnp.where(forced, FORCE, jnp.where(blk <= cur, imp, -FORCE))

    rows2 = jnp.where(i >= 2, 0, 3 * TQ)
    rows1 = jnp.where(i >= 1, TQ, 3 * TQ)
    run(win_tasks(jnp.maximum(i - 2, 0), rows2) + win_tasks(jnp.maximum(i - 1, 0), rows1)
        + win_tasks(i, 2 * TQ))

    blk8 =lax.broadcasted_iota(jnp.int32, (SUBLANES, TQ), 0)

    def block_flags(g, n_used):
        grp = [imp_ref[g, SUBLANES * r:SUBLANES * (r + 1), :] for r in range(n_used)]
        rank = [jnp.zeros((SUBLANES, TQ), F32) for _ in range(n_used)]
        for k in range(n_used * SUBLANES):
            rowk = imp_ref[g, k:k + 1, :]
            kg = k // SUBLANES
            for r in range(n_used):
                if r < kg:
                    beats = jnp.where(rowk > grp[r], 1.0, 0.0)
                elif r > kg:
                    beats = jnp.where(rowk >= grp[r], 1.0, 0.0)
                else:
                    beats = jnp.where(blk8 > k - SUBLANES * kg,
                                      jnp.where(rowk >= grp[r], 1.0, 0.0),
                                      jnp.where(rowk > grp[r], 1.0, 0.0))
                rank[r] = rank[r] + beats
        rank = jnp.concatenate(rank, axis=0)
        flags = jnp.where(rank < float(N_SEL), 0.0, -1.0).astype(BF16)
        pad = HEAD_DIM - n_used * SUBLANES
        if pad:
            flags = jnp.concatenate([flags, jnp.zeros((pad, TQ), BF16)], axis=0)
        return flags

    for hh in heads:
        qa_ref[hh, 0:HEAD_DIM, :] = qt_ref[0, hh, 0:HEAD_DIM, :]
    per_tile = TQ // L_SLC
    no_rank_tiles = N_SEL // per_tile

    @pl.when(i < no_rank_tiles)
    def _():
        for hh in heads:
            qa_ref[hh, HEAD_DIM:, :] = jnp.zeros((HEAD_DIM, TQ), BF16)

    lo = no_rank_tiles
    while lo * per_tile < ns:
        hi = min(lo + RANK_TILE_SPAN, ns // per_tile)
        n_used = -(-hi * per_tile // SUBLANES)

        @pl.when((i >= lo) & (i < hi))
        def _(n_used=n_used):
            for g in range(N_KV_HEADS):
                flags = block_flags(g, n_used)
                for hh in range(g * HPG, (g + 1) * HPG):
                    qa_ref[hh, HEAD_DIM:, :] = flags
        lo = hi

    n_far = jnp.maximum(i - 1, 0)

    def far_group(jj, carry):
        tasks = []
        for u in range(FAR_GROUP):
            tasks += slc_tasks(FAR_GROUP * jj + u)
        run(tasks)
        return carry

    lax.fori_loop(0, n_far // FAR_GROUP, far_group, 0)
    for rem in range(FAR_GROUP):
        @pl.when(n_far % FAR_GROUP == rem)
        def _(rem=rem):
            tasks = []
            for u in range(rem):
                tasks += slc_tasks(n_far - rem + u)
            run(tasks + slc_tasks(jnp.maximum(i - 1, 0), rows1) + slc_tasks(i, 2 * TQ))

    gates_t = jnp.transpose(ng_ref[...])
    outs = []
    for hh in heads:
        a_s, a_w = accs_ref[hh], accw_ref[hh]
        o_s = a_s[0:HEAD_DIM] * (1.0 / a_s[HEAD_DIM:HEAD_DIM + 1])
        o_w = a_w[0:HEAD_DIM] * (1.0 / a_w[HEAD_DIM:HEAD_DIM + 1])
        row = (hh // HPG) * LANES + 3 * (hh % HPG)
        outs.append(gates_t[row:row + 1] * oc_ref[hh] + gates_t[row + 1:row + 2] * o_s
                    + gates_t[row + 2:row + 3] * o_w)
    o_ref[...] = jnp.transpose(jnp.concatenate(outs, axis=0)).astype(BF16)


def _bias_of(thr_ref, rb_ref, dist, h):
    b = jnp.full(dist.shape, rb_ref[h], F32)
    for k in range(1, N_BUCKETS):
        b = jnp.where(dist >= thr_ref[k], rb_ref[k * N_HEADS + h], b)
    return b


def _cmp_bias_kernel(thr_ref, rb_ref, biasc_ref, *, nc, ncp):
    i = pl.program_id(0)
    per_tile = TQ // D_STRIDE
    band = 2 * per_tile
    assert TQ + D_STRIDE - (L_CMP - 1) >= MAX_DISTANCE

    def dist_of(c0, rows):
        c = c0 + lax.broadcasted_iota(jnp.int32, (rows, TQ), 0)
        r = lax.broadcasted_iota(jnp.int32, (rows, TQ), 1)
        dist = i * TQ + r - (c * D_STRIDE + L_CMP - 1)
        return dist, (dist >= 0) & (c < nc)

    _, ok_all = dist_of(0, ncp)
    band0 = pl.multiple_of(jnp.maximum(i * per_tile - per_tile, 0), per_tile)
    dist_b, ok_b = dist_of(band0, band)
    for h in range(N_HEADS):
        biasc_ref[h] = jnp.where(ok_all, rb_ref[(N_BUCKETS - 1) * N_HEADS + h], NEG)
        biasc_ref[h, pl.ds(band0, band), :] = jnp.where(
            ok_b, _bias_of(thr_ref, rb_ref, dist_b, h), NEG)


def _diag_bias_kernel(thr_ref, rb_ref, bd_ref):
    d0 = (lax.broadcasted_iota(jnp.int32, (TQ, TQ), 1)
          - lax.broadcasted_iota(jnp.int32, (TQ, TQ), 0))
    for h in range(N_HEADS):
        far = rb_ref[(N_BUCKETS - 1) * N_HEADS + h]
        bd_ref[h, 0:TQ, :] = jnp.where(d0 < 0, _bias_of(thr_ref, rb_ref, d0 + 2 * TQ, h) - far, NEG)
        bd_ref[h, TQ:2 * TQ, :] = _bias_of(thr_ref, rb_ref, d0 + TQ, h) - far
        bd_ref[h, 2 * TQ:3 * TQ, :] = jnp.where(d0 >= 0, _bias_of(thr_ref, rb_ref, d0, h) - far, NEG)
        bd_ref[h, 3 * TQ:, :] = jnp.full((TQ, TQ), NEG, F32)


def _attention_tables(rel_bias, seq):
    ncp = seq // D_STRIDE
    nc = (seq - L_CMP) // D_STRIDE + 1
    ns = seq // L_SLC
    nt = seq // TQ
    n_probe = 2 * MAX_DISTANCE
    buckets = _t5_bucket(jnp.arange(n_probe))
    thr = jnp.sum(buckets[None, :] < jnp.arange(N_BUCKETS)[:, None], axis=1).astype(jnp.int32)
    assert WINDOW == 2 * TQ
    rb = (rel_bias.astype(F32) * LOG2E).reshape(-1)
    bias_c = pl.pallas_call(
        functools.partial(_cmp_bias_kernel, nc=nc, ncp=ncp),
        out_shape=jax.ShapeDtypeStruct((N_HEADS, ncp, seq), F32),
        grid_spec=pltpu.PrefetchScalarGridSpec(
            num_scalar_prefetch=2, grid=(nt,), in_specs=[],
            out_specs=pl.BlockSpec((N_HEADS, ncp, TQ), lambda i, *_: (0, 0, i))),
        compiler_params=_cparams(1), name="cmp_bias")(thr, rb)
    bd = pl.pallas_call(
        _diag_bias_kernel, out_shape=jax.ShapeDtypeStruct((N_HEADS, 4 * TQ, TQ), F32),
        grid_spec=pltpu.PrefetchScalarGridSpec(
            num_scalar_prefetch=2, grid=(1,), in_specs=[],
            out_specs=pl.BlockSpec((N_HEADS, 4 * TQ, TQ), lambda i, *_: (0, 0, 0))),
        compiler_params=_cparams(1), name="diag_bias")(thr, rb)
    c_start = jnp.arange(ncp) * D_STRIDE
    c_end = c_start + L_CMP - 1
    s_start = jnp.arange(ns) * L_SLC
    aggt = ((c_end[None, :] >= s_start[:, None]) & (c_start[None, :] <= s_start[:, None] + L_SLC - 1)
            & (jnp.arange(ncp)[None, :] < nc)).astype(BF16)
    return bias_c, bd, aggt


def _nsa(qt, kcmp, vcmpt, ks, vst, kw, vwt, ng, rel_bias, batch, seq):
    first_far = TQ + 1
    assert MAX_EXACT + int(math.log(first_far / MAX_EXACT) / math.log(MAX_DISTANCE / MAX_EXACT)
                           * (N_BUCKETS - MAX_EXACT)) >= N_BUCKETS - 1, "far tiles need one bucket"
    ncp = seq // D_STRIDE
    ns = seq // L_SLC
    nt = seq // TQ
    bias_c, bd, aggt = _attention_tables(rel_bias, seq)
    kspec = pl.BlockSpec((1, N_KV_HEADS, seq, LANES), lambda b, i: (b, 0, 0, 0))
    vtspec = pl.BlockSpec((1, N_KV_HEADS, LANES, seq), lambda b, i: (b, 0, 0, 0))
    state = lambda rows: pltpu.VMEM((N_HEADS, rows, TQ), F32)
    const = lambda shape: pl.BlockSpec(shape, lambda b, i: (0,) * len(shape),
                                       pipeline_mode=pl.Buffered(1))
    grid_spec = pltpu.PrefetchScalarGridSpec(
        num_scalar_prefetch=0, grid=(batch, nt),
        in_specs=[pl.BlockSpec((1, N_HEADS, LANES, TQ), lambda b, i: (b, 0, 0, i)),
                  pl.BlockSpec((1, ncp, N_KV_HEADS * LANES), lambda b, i: (b, 0, 0)),
                  pl.BlockSpec((1, N_KV_HEADS, LANES, ncp), lambda b, i: (b, 0, 0, 0)),
                  pl.BlockSpec((N_HEADS, ncp, TQ), lambda b, i: (0, 0, i)),
                  kspec, vtspec, kspec, vtspec,
                  pl.BlockSpec((TQ, N_KV_HEADS * LANES), lambda b, i: (b * nt + i, 0)),
                  const((ns, ncp)), const((N_HEADS, 4 * TQ, TQ))],
        out_specs=pl.BlockSpec((TQ, N_HEADS * HEAD_DIM), lambda b, i: (b * nt + i, 0)),
        scratch_shapes=[pltpu.VMEM((N_KV_HEADS, ns, TQ), F32),
                        pltpu.VMEM((N_HEADS, LANES, TQ), BF16),
                        state(HEAD_DIM),
                        state(1), state(PV_ROWS),
                        state(1), state(PV_ROWS)])
    return pl.pallas_call(
        functools.partial(_nsa_kernel, ns=ns),
        out_shape=jax.ShapeDtypeStruct((batch * seq, N_HEADS * HEAD_DIM), BF16),
        grid_spec=grid_spec, compiler_params=_cparams(2), name="nsa",
    )(qt, kcmp, vcmpt, bias_c, ks, vst, kw, vwt, ng, aggt, bd)


def _ffn_kernel(x_ref, ya_ref, o_ref, gb_ref, p_ref, wao_ref, wout_ref, nf_ref, wup_ref, cw_ref,
                cb_ref, wd_ref, np_ref, wpg_ref, wpe_ref, nfin_ref, out_ref, carry_ref,
                *, tm, tiles_per_seq, d_ff, chunks):
    r = pl.program_id(0)

    @pl.when((r % tiles_per_seq) == 0)
    def _():
        carry_ref[...] = jnp.zeros_like(carry_ref)

    n = tm // FFN_SPLIT
    groups = [slice(k * n, (k + 1) * n) for k in range(FFN_SPLIT)]

    x1, hf = [], []
    for rows in groups:
        y = (ya_ref[rows, :].astype(F32)
             + gb_ref[rows, :].astype(F32) * _dot(o_ref[rows, :], wao_ref[...]))
        x1.append(x_ref[rows, :] + _dot(y.astype(BF16), wout_ref[...]))
        hf.append(_rms(x1[-1], nf_ref[...]).astype(BF16))

    def conv3(up, prev, c0, width):
        rowi = lax.broadcasted_iota(jnp.int32, up.shape, 0)
        s1 = jnp.where(rowi == 0, prev[7:8, :], pltpu.roll(up, 1, 0))
        s2 = jnp.where(rowi == 0, prev[6:7, :],
                       jnp.where(rowi == 1, prev[7:8, :], pltpu.roll(up, 2, 0)))
        return (cw_ref[0:1, c0:c0 + width] * s2 + cw_ref[1:2, c0:c0 + width] * s1
                + cw_ref[2:3, c0:c0 + width] * up + cb_ref[:, c0:c0 + width])

    acc = [None] * FFN_SPLIT
    for c0, width in chunks:
        cols = (slice(c0, c0 + width), slice(d_ff + c0, d_ff + c0 + width))
        prev = [carry_ref[:, cs] for cs in cols]
        for k in range(FFN_SPLIT):
            ups = [_dot(hf[k], wup_ref[:, cs]) for cs in cols]
            gate, val = [conv3(ups[t], prev[t], cols[t].start, width) for t in range(2)]
            prev = [u[n - 8:, :] for u in ups]
            part = _dot((_gelu(gate) * val).astype(BF16), wd_ref[c0:c0 + width, :])
            acc[k] = part if acc[k] is None else acc[k] + part
        for t in range(2):
            carry_ref[:, cols[t]] = prev[t]

    for k, rows in enumerate(groups):
        x2 = x1[k] + acc[k]
        pg = _sigmoid(_dot(_rms(x2, np_ref[...]).astype(BF16), wpg_ref[...]))
        pe = _dot(p_ref[rows, :].astype(BF16), wpe_ref[...])
        out_ref[rows, :] = _rms(x2 + pg * pe, nfin_ref[...])


def _ffn(x2, ya, o, gb, p2, w_attn_out, w_out, norm_ffn, w_up, ffn_dw_w, ffn_dw_b, w_down,
         norm_ple, w_ple_gate, w_ple, norm_final, seq, p_tile0=0, tm=512):
    t, d = x2.shape
    d_ff = w_down.shape[0]
    assert d_ff % MXU_DEPTH == 0
    chunks, c0 = [], 0
    while c0 < d_ff:
        width = min(FFN_CHUNK, d_ff - c0)
        chunks.append((c0, width))
        c0 += width
    tiles_per_seq = seq // tm
    cw = jnp.concatenate([ffn_dw_w, jnp.zeros((8 - FFN_CONV_WIDTH, 2 * d_ff), ffn_dw_w.dtype)], 0)
    vec = lambda v: v.reshape(1, -1)
    const = lambda a: pl.BlockSpec(a.shape, lambda r: (0, 0), pipeline_mode=pl.Buffered(1))
    rows = lambda n: pl.BlockSpec((tm, n), lambda r: (r, 0))
    kern = functools.partial(_ffn_kernel, tm=tm, tiles_per_seq=tiles_per_seq, d_ff=d_ff,
                             chunks=tuple(chunks))
    operands = (x2, ya, o, gb, p2, w_attn_out, w_out, vec(norm_ffn), w_up, cw, vec(ffn_dw_b),
                w_down, vec(norm_ple), w_ple_gate, w_ple, vec(norm_final))
    return pl.pallas_call(
        kern, out_shape=jax.ShapeDtypeStruct((t, d), F32), grid=(t // tm,),
        in_specs=[rows(d), rows(d), rows(o.shape[1]), rows(d),
                  pl.BlockSpec((tm, p2.shape[1]), lambda r: (p_tile0 + r, 0))]
        + [const(a) for a in operands[5:]],
        out_specs=rows(d),
        scratch_shapes=[pltpu.VMEM((8, 2 * d_ff), F32)],
        compiler_params=_cparams(1), name="ffn",
    )(*operands)


def kernel(x, p, rel_bias, norm_mix, w_in, conv_dw_w, conv_dw_b, conv_ln_g, conv_ln_b, w_conv_out,
           cmp_pe_k, cmp_pe_v, w_ck1, w_ck2, w_cv1, w_cv2, w_attn_out, w_out, norm_ffn, w_up,
           ffn_dw_w, ffn_dw_b, w_down, norm_ple, w_ple_gate, w_ple, norm_final):
    batch, seq, d = x.shape
    depth = w_in.shape[0]
    x2 = x.reshape(batch * seq, d)
    for i in range(depth):
        ya, qt, kc, vc, ks, vst, kw, vwt, ng, gb = _inproj(
            x2, norm_mix[i], _prep_inproj_weights(w_in[i], d), conv_dw_w[i], conv_dw_b[i],
            conv_ln_g[i], conv_ln_b[i], w_conv_out[i].astype(BF16), batch, seq)
        kcmp, vcmpt = _compress(kc, vc,
                               _prep_compress_weights(cmp_pe_k[i], w_ck1[i], w_ck2[i]),
                               _prep_compress_weights(cmp_pe_v[i], w_cv1[i], w_cv2[i]),
                               batch, seq)
        o = _nsa(qt, kcmp, vcmpt, ks, vst, kw, vwt, ng, rel_bias, batch, seq)
        assert i == depth - 1, "the final RMSNorm is fused into the (single) layer's MLP kernel"
        ffn_tm = 512
        x2 = _ffn(x2, ya, o, gb, p.reshape(depth * batch * seq, -1), w_attn_out[i].astype(BF16),
                  w_out[i].astype(BF16), norm_ffn[i], w_up[i].astype(BF16), ffn_dw_w[i],
                  ffn_dw_b[i], w_down[i].astype(BF16), norm_ple[i], w_ple_gate[i].astype(BF16),
                  w_ple[i].astype(BF16), norm_final, seq,
                  p_tile0=i * (batch * seq // ffn_tm), tm=ffn_tm)
    return x2.reshape(batch, seq, d)
```

```python
import functools
import math

import jax
import jax.numpy as jnp
from jax import lax
from jax.experimental import pallas as pl
from jax.experimental.pallas import tpu as pltpu

N_HEADS = 8
HEAD_DIM = 64
N_KV_HEADS = 2
HPG = N_HEADS // N_KV_HEADS
L_CMP = 32
D_STRIDE = 16
CMP_HIDDEN = 256
L_SLC = 64
N_SEL = 16
WINDOW = 512
N_BUCKETS = 32
MAX_EXACT = N_BUCKETS // 2
MAX_DISTANCE = 128
CONV_WIDTH = 31
FFN_CONV_WIDTH = 3
EPS = 1e-6
FORCE = 1e4

LANES = 128
MXU_DEPTH = 256
FFN_CHUNK = 6 * MXU_DEPTH
FFN_SPLIT = 2
NEG = -1e30
M_INIT = -1e29
LOG2E = 1.4426950408889634
PIPE_DEPTH = 6
RANK_TILE_SPAN = 4
FAR_GROUP = 4
TQ = 256
PV_ROWS = HEAD_DIM + 16
SUBLANES = 8
CONV_HALO = 32
CONV_ROWS = 64
VMEM_LIMIT = 56 * 1024 * 1024

F32 = jnp.float32
BF16 = jnp.bfloat16


def _cparams(n_axes):
    return pltpu.CompilerParams(dimension_semantics=("arbitrary",) * n_axes,
                                vmem_limit_bytes=VMEM_LIMIT)


def _dot(a, b):
    return jnp.dot(a, b, preferred_element_type=F32)


def _rms(xf, g):
    return xf * lax.rsqrt(jnp.mean(xf * xf, axis=-1, keepdims=True) + EPS) * g


def _sigmoid(x):
    return 1.0 / (1.0 + jnp.exp(-x))


def _gelu(x):
    return 0.5 * x * (1.0 + jnp.tanh(0.7978845608028654 * (x + 0.044715 * x * x * x)))


def _t5_bucket(dist):
    n = jnp.maximum(dist, 0)
    nf = jnp.maximum(n, MAX_EXACT).astype(F32)
    large = MAX_EXACT + (jnp.log(nf / MAX_EXACT) / math.log(MAX_DISTANCE / MAX_EXACT)
                         * (N_BUCKETS - MAX_EXACT)).astype(jnp.int32)
    large = jnp.minimum(large, N_BUCKETS - 1)
    return jnp.where(n < MAX_EXACT, n, large)


def _inproj_kernel(x_ref, g_ref, wu_ref, wq_ref, wkv_ref, wng_ref, wmg_ref,
                   cw_ref, cb_ref, lg_ref, lb_ref, wco_ref,
                   ya_ref, qt_ref, kc_ref, vc_ref, ks_ref, vst_ref, kw_ref, vwt_ref, ng_ref, gb_ref,
                   cs_ref, ext_ref, sh_ref, ga_ref, a_stage_ref, ga_stage_ref,
                   *, tm, n_tiles, tiles_per_seq, d_conv):
    step = pl.program_id(0)
    r = jnp.minimum(step, n_tiles - 1)

    @pl.when(step == 0)
    def _():
        ext_ref[...] = jnp.zeros(ext_ref.shape, F32)
        ga_ref[...] = jnp.zeros(ga_ref.shape, BF16)

    span = sh_ref.shape[1]
    for ph in range(1, SUBLANES):
        sh_ref[ph - 1] = ext_ref[ph:ph + span, :]
    off = CONV_HALO - (CONV_WIDTH - 1)
    conv = []
    for r0 in range(0, tm, CONV_ROWS):
        acc = jnp.zeros((CONV_ROWS, d_conv), F32) + cb_ref[...]
        for j in range(CONV_WIDTH):
            ph = (off + j) % SUBLANES
            base = r0 + off + j - ph
            if ph == 0:
                xs = ext_ref[base:base + CONV_ROWS, :]
            else:
                xs = sh_ref[ph - 1, base:base + CONV_ROWS, :]
            acc = acc + jnp.tile(cw_ref[j], (CONV_ROWS // SUBLANES, 1)) * xs
        conv.append(acc)
    conv = jnp.concatenate(conv, axis=0)
    mu = jnp.mean(conv, axis=-1, keepdims=True)
    cen = conv - mu
    var = jnp.mean(cen * cen, axis=-1, keepdims=True)
    y = cen * lax.rsqrt(var + EPS) * lg_ref[...] + lb_ref[...]
    y = (y * _sigmoid(y)).astype(BF16)

    h = _rms(x_ref[...], g_ref[...]).astype(BF16)
    u = _dot(h, wu_ref[...])
    a_stage_ref[...] = u[:, :d_conv] * _sigmoid(u[:, d_conv:])
    zmg = _dot(h, wmg_ref[...])
    d_model = zmg.shape[1] // 2
    ga_stage_ref[...] = _sigmoid(zmg[:, :d_model]).astype(BF16)
    gb_ref[...] = _sigmoid(zmg[:, d_model:]).astype(BF16)

    zqt = jnp.transpose(_dot(h, wq_ref[...]))
    zero_rows = jnp.zeros((HEAD_DIM, tm), BF16)
    for hh in range(N_HEADS):
        qt_ref[0, hh, 0:HEAD_DIM, :] = zqt[hh * HEAD_DIM:(hh + 1) * HEAD_DIM].astype(BF16)
        qt_ref[0, hh, HEAD_DIM:, :] = zero_rows

    zkv = _dot(h, wkv_ref[...])
    for which, out_ref in enumerate((kc_ref, vc_ref)):
        cs_ref[which] = zkv[:, which * LANES:(which + 1) * LANES]
        for tok in range(D_STRIDE):
            rows = cs_ref[which, pl.ds(tok, tm // D_STRIDE, stride=D_STRIDE), :]
            out_ref[:, tok * LANES:(tok + 1) * LANES] = rows.astype(BF16)
    lane = lax.broadcasted_iota(jnp.int32, (tm, LANES), 1)
    row = lax.broadcasted_iota(jnp.int32, (tm, LANES), 0)
    spos = (r % tiles_per_seq) * tm + row
    lo = lane < HEAD_DIM
    blk_tag = jnp.where((lane - HEAD_DIM) == spos // L_SLC, -NEG, 0.0)
    ones_rows = jnp.where(lax.broadcasted_iota(jnp.int32, (HEAD_DIM, tm), 0) < 8, 1.0, 0.0)
    ones_rows = ones_rows.astype(BF16)
    for g in range(N_KV_HEADS):
        for k_ref, vt_ref, base, tag in ((ks_ref, vst_ref, (2 + g) * LANES, blk_tag),
                                         (kw_ref, vwt_ref, (4 + g) * LANES, 0.0)):
            pair = zkv[:, base:base + LANES]
            k_ref[0, g] = jnp.where(lo, pair, tag).astype(BF16)
            vt_ref[0, g, 0:HEAD_DIM, :] = jnp.transpose(pair)[HEAD_DIM:].astype(BF16)
            vt_ref[0, g, HEAD_DIM:, :] = ones_rows

    ng_ref[...] = _sigmoid(_dot(h, wng_ref[...]))

    ya_ref[...] = (ga_ref[...].astype(F32) * _dot(y, wco_ref[...])).astype(BF16)

    seq_start = (r % tiles_per_seq) == 0
    ext_ref[0:CONV_HALO, :] = jnp.where(seq_start, 0.0, ext_ref[tm:tm + CONV_HALO, :])
    ext_ref[CONV_HALO:, :] = a_stage_ref[...]
    ga_ref[...] = ga_stage_ref[...]


def _prep_inproj_weights(w_in, d_model):
    d_conv = d_model // 2
    n_conv = 2 * d_conv
    n_q = N_HEADS * HEAD_DIM
    n_kv = 6 * N_KV_HEADS * HEAD_DIM
    n_ng = 3 * N_HEADS
    o = 0
    wu = w_in[:, o:o + n_conv]; o += n_conv
    wq = w_in[:, o:o + n_q]; o += n_q
    wkv = w_in[:, o:o + n_kv]; o += n_kv
    wng = w_in[:, o:o + n_ng]; o += n_ng
    wmg = w_in[:, o:]
    wq_p = wq * (LOG2E / math.sqrt(HEAD_DIM))
    kvcols = [wkv[:, 0:LANES], wkv[:, LANES:2 * LANES]]
    for k_kind in (2, 4):
        for g in range(N_KV_HEADS):
            for kind in (k_kind, k_kind + 1):
                c0 = kind * N_KV_HEADS * HEAD_DIM + g * HEAD_DIM
                kvcols.append(wkv[:, c0:c0 + HEAD_DIM])
    wkv_p = jnp.concatenate(kvcols, axis=1)
    per_g = HPG * 3
    ngcols = []
    for g in range(N_KV_HEADS):
        ngcols += [wng[:, g * per_g:(g + 1) * per_g],
                   jnp.zeros((d_model, LANES - per_g), w_in.dtype)]
    wng_p = jnp.concatenate(ngcols, axis=1)
    return tuple(w.astype(BF16) for w in (wu, wq_p, wkv_p, wng_p, wmg))


def _inproj(x2, norm_g, weights, conv_w, conv_b, ln_g, ln_b, w_conv_out, batch, seq, tm=512):
    t, d = x2.shape
    wu, wq, wkv, wng, wmg = weights
    d_conv = d // 2
    tiles_per_seq = seq // tm
    n_tiles = t // tm
    assert tm % CONV_ROWS == 0 and CONV_HALO >= CONV_WIDTH - 1 and CONV_HALO % SUBLANES == 0
    cw = jnp.broadcast_to(conv_w[:, None, :], (CONV_WIDTH, SUBLANES, d_conv))
    vec = lambda v: v.reshape(1, -1)
    full = lambda w: pl.BlockSpec(w.shape, lambda s: (0,) * w.ndim, pipeline_mode=pl.Buffered(1))
    tile = lambda s: jnp.minimum(s, n_tiles - 1)
    row = lambda n: pl.BlockSpec((tm, n), lambda s: (tile(s), 0))
    row_prev = lambda n: pl.BlockSpec((tm, n), lambda s: (jnp.maximum(s - 1, 0), 0))
    headed = lambda nh: pl.BlockSpec(
        (1, nh, tm, LANES), lambda s: (tile(s) // tiles_per_seq, 0, tile(s) % tiles_per_seq, 0))
    headed_t = lambda nh: pl.BlockSpec(
        (1, nh, LANES, tm), lambda s: (tile(s) // tiles_per_seq, 0, 0, tile(s) % tiles_per_seq))
    k_shape = jax.ShapeDtypeStruct((batch, N_KV_HEADS, seq, LANES), BF16)
    vt_shape = jax.ShapeDtypeStruct((batch, N_KV_HEADS, LANES, seq), BF16)
    out_shape = (
        jax.ShapeDtypeStruct((t, d), BF16),
        jax.ShapeDtypeStruct((batch, N_HEADS, LANES, seq), BF16),
        jax.ShapeDtypeStruct((t // D_STRIDE, D_STRIDE * LANES), BF16),
        jax.ShapeDtypeStruct((t // D_STRIDE, D_STRIDE * LANES), BF16),
        k_shape, vt_shape, k_shape, vt_shape,
        jax.ShapeDtypeStruct((t, N_KV_HEADS * LANES), F32),
        jax.ShapeDtypeStruct((t, d), BF16),
    )
    blocks = pl.BlockSpec((tm // D_STRIDE, D_STRIDE * LANES), lambda s: (tile(s), 0))
    out_specs = (row_prev(d), headed_t(N_HEADS), blocks, blocks,
                 headed(N_KV_HEADS), headed_t(N_KV_HEADS), headed(N_KV_HEADS),
                 headed_t(N_KV_HEADS), row(N_KV_HEADS * LANES), row(d))
    kern = functools.partial(_inproj_kernel, tm=tm, n_tiles=n_tiles, tiles_per_seq=tiles_per_seq,
                             d_conv=d_conv)
    consts = (vec(norm_g), wu, wq, wkv, wng, wmg, cw, vec(conv_b), vec(ln_g), vec(ln_b), w_conv_out)
    return pl.pallas_call(
        kern, out_shape=out_shape, grid=(n_tiles + 1,),
        in_specs=[row(d)] + [full(w) for w in consts],
        out_specs=out_specs,
        scratch_shapes=[pltpu.VMEM((2, tm, LANES), F32),
                        pltpu.VMEM((tm + CONV_HALO, d_conv), F32),
                        pltpu.VMEM((SUBLANES - 1, tm + CONV_HALO - SUBLANES, d_conv), F32),
                        pltpu.VMEM((tm, d), BF16),
                        pltpu.VMEM((tm, d_conv), F32), pltpu.VMEM((tm, d), BF16)],
        compiler_params=_cparams(1), name="inproj",
    )(x2, *consts)


def _compress_kernel(rk_ref, rv_ref, pek_ref, pev_ref, w1k_ref, w1v_ref, w2k_ref, w2v_ref,
                     kcmp_ref, vcmp_ref, *, ncp):
    def one(r_ref, pe_ref, w1_ref, w2_ref):
        r = r_ref[0]
        top = _dot(r, w1_ref[0])
        bot = _dot(r, w1_ref[1])
        pe_h = _dot(pe_ref[0], w1_ref[0]) + _dot(pe_ref[1], w1_ref[1])
        nxt = pltpu.roll(bot, ncp - 1, 0)
        rowi = lax.broadcasted_iota(jnp.int32, top.shape, 0)
        hid = top + jnp.where(rowi == ncp - 1, 0.0, nxt) + pe_h[0:1, :]
        return _dot(_gelu(hid).astype(BF16), w2_ref[...])

    kcmp_ref[0] = one(rk_ref, pek_ref, w1k_ref, w2k_ref).astype(BF16)
    v = one(rv_ref, pev_ref, w1v_ref, w2v_ref)
    rowt = lax.broadcasted_iota(jnp.int32, (LANES, ncp), 0)
    ones_rows = (rowt >= HEAD_DIM) & (rowt < HEAD_DIM + 8)
    for g in range(N_KV_HEADS):
        vt = jnp.transpose(v[:, g * LANES:(g + 1) * LANES])
        vcmp_ref[0, g] = jnp.where(ones_rows, 1.0, vt).astype(BF16)


def _prep_compress_weights(pe, w1, w2):
    half = L_CMP // 2
    w1r = w1.reshape(L_CMP, HEAD_DIM, CMP_HIDDEN).astype(BF16)
    w2p = jnp.pad(w2, ((0, 0), (0, LANES - HEAD_DIM))).astype(BF16)

    def per_group(w, axis):
        z = jnp.zeros_like(w)
        return jnp.stack([jnp.concatenate([w if k == g else z for k in range(N_KV_HEADS)], axis=-1)
                          for g in range(N_KV_HEADS)], axis=axis)

    w1big = per_group(w1r, 1).reshape(2, half * N_KV_HEADS * HEAD_DIM, N_KV_HEADS * CMP_HIDDEN)
    w2big = per_group(w2p, 0).reshape(N_KV_HEADS * CMP_HIDDEN, N_KV_HEADS * LANES)
    per = pe.reshape(2, half, 1, HEAD_DIM)
    pebig = jnp.broadcast_to(per, (2, half, N_KV_HEADS, HEAD_DIM)).reshape(2, 1, -1)
    pebig = jnp.broadcast_to(pebig, (2, 8, pebig.shape[-1])).astype(BF16)
    return pebig, w1big, w2big


def _compress(kc, vc, wk, wv, batch, seq):
    ncp = seq // D_STRIDE
    width = D_STRIDE * LANES
    rk = kc.reshape(batch, ncp, width)
    rv = vc.reshape(batch, ncp, width)
    pek, w1k, w2k = wk
    pev, w1v, w2v = wv
    c3 = lambda a: pl.BlockSpec(a.shape, lambda b: (0, 0, 0))
    c2 = lambda a: pl.BlockSpec(a.shape, lambda b: (0, 0))
    rspec = pl.BlockSpec((1, ncp, width), lambda b: (b, 0, 0))
    kspec = pl.BlockSpec((1, ncp, N_KV_HEADS * LANES), lambda b: (b, 0, 0))
    kshape = jax.ShapeDtypeStruct((batch, ncp, N_KV_HEADS * LANES), BF16)
    vspec = pl.BlockSpec((1, N_KV_HEADS, LANES, ncp), lambda b: (b, 0, 0, 0))
    vshape = jax.ShapeDtypeStruct((batch, N_KV_HEADS, LANES, ncp), BF16)
    return pl.pallas_call(
        functools.partial(_compress_kernel, ncp=ncp), out_shape=(kshape, vshape), grid=(batch,),
        in_specs=[rspec, rspec, c3(pek), c3(pev), c3(w1k), c3(w1v), c2(w2k), c2(w2v)],
        out_specs=(kspec, vspec), compiler_params=_cparams(1), name="compress",
    )(rk, rv, pek, pev, w1k, w1v, w2k, w2v)


def _nsa_kernel(qt_ref, kcmp_ref, vcmpt_ref, biasc_ref, ks_ref, vst_ref, kw_ref, vwt_ref,
                ng_ref, aggt_ref, bd_ref, o_ref,
                imp_ref, qa_ref, oc_ref, ms_ref, accs_ref, mw_ref, accw_ref, *, ns):
    i = pl.program_id(1)
    heads = range(N_HEADS)

    def k_tile(ref, g, j):
        return ref[0, g, pl.ds(pl.multiple_of(j * TQ, TQ), TQ), :]

    def vt_tile(ref, g, j):
        return ref[0, g, 0:PV_ROWS, pl.ds(pl.multiple_of(j * TQ, TQ), TQ)]

    def pipelined(tasks, scores, update):
        pending = [scores(t) for t in tasks[:PIPE_DEPTH]]
        for t, task in enumerate(tasks):
            update(task, pending.pop(0))
            if t + PIPE_DEPTH < len(tasks):
                pending.append(scores(tasks[t + PIPE_DEPTH]))

    def run(tasks):
        def scores(task):
            hh, kt, _, qmat, _, _, row0 = task
            s = _dot(kt, qmat)
            if row0 is None:
                return s
            if not isinstance(row0, int):
                row0 = pl.multiple_of(row0, TQ)
            return s + bd_ref[hh, pl.ds(row0, TQ), :]

        def update(task, s):
            hh, _, vt, _, m_ref, acc_ref, _ = task
            m_old = m_ref[hh]
            m_new = jnp.maximum(m_old, jnp.max(s, axis=0, keepdims=True))
            p = jnp.exp2(s - m_new).astype(BF16)
            acc_ref[hh] = jnp.exp2(m_old - m_new) * acc_ref[hh] + _dot(vt, p)
            m_ref[hh] = m_new

        pipelined(tasks, scores, update)

    def slc_tasks(j, row0=None):
        tiles = [(k_tile(ks_ref, g, j), vt_tile(vst_ref, g, j)) for g in range(N_KV_HEADS)]
        return [(hh,) + tiles[hh // HPG] + (qa_ref[hh], ms_ref, accs_ref, row0) for hh in heads]

    def win_tasks(j, row0):
        tiles = [(k_tile(kw_ref, g, j), vt_tile(vwt_ref, g, j)) for g in range(N_KV_HEADS)]
        return [(hh,) + tiles[hh // HPG] + (qt_ref[0, hh], mw_ref, accw_ref, row0) for hh in heads]

    for hh in heads:
        ms_ref[hh] = jnp.full((1, TQ), M_INIT, F32)
        mw_ref[hh] = jnp.full((1, TQ), M_INIT, F32)
        accs_ref[hh] = jnp.zeros((PV_ROWS, TQ), F32)
        accw_ref[hh] = jnp.zeros((PV_ROWS, TQ), F32)

    p_parts = []

    def cmp_scores(hh):
        g = hh // HPG
        return _dot(kcmp_ref[0, :, g * LANES:(g + 1) * LANES], qt_ref[0, hh]) + biasc_ref[hh]

    def cmp_update(hh, s_c):
        m_c = jnp.maximum(jnp.max(s_c, axis=0, keepdims=True), M_INIT)
        p_c = jnp.exp2(s_c - m_c)
        l_c = jnp.sum(p_c, axis=0, keepdims=True)
        p_c = p_c * jnp.where(l_c > 0.0, 1.0 / l_c, 0.0)
        oc_ref[hh] = _dot(vcmpt_ref[0, hh // HPG, 0:HEAD_DIM, :], p_c.astype(BF16))
        p_parts.append(p_c)

    pipelined(list(heads), cmp_scores, cmp_update)

    blk = lax.broadcasted_iota(jnp.int32, (ns, TQ), 0)
    cur = (i * TQ + lax.broadcasted_iota(jnp.int32, (ns, TQ), 1)) // L_SLC
    forced = (blk == 0) | (blk == cur) | (blk == cur - 1)
    for g in range(N_KV_HEADS):
        p4 = p_parts[g * HPG:(g + 1) * HPG]
        p_sum = (p4[0] + p4[1]) + (p4[2] + p4[3])
        p_hi = p_sum.astype(BF16)
        p_lo = (p_sum - p_hi.astype(F32)).astype(BF16)
        imp = _dot(aggt_ref[...], p_hi) + _dot(aggt_ref[...], p_lo)
        imp_ref[g] = jnp.where(forced, FORCE, jnp.where(blk <= cur, imp, -FORCE))

    rows2 = jnp.where(i >= 2, 0, 3 * TQ)
    rows1 = jnp.where(i >= 1, TQ, 3 * TQ)
    run(win_tasks(jnp.maximum(i - 2, 0), rows2) + win_tasks(jnp.maximum(i - 1, 0), rows1)
        + win_tasks(i, 2 * TQ))

    blk8 = lax.broadcasted_iota(jnp.int32, (SUBLANES, TQ), 0)

    def block_flags(g, n_used):
        grp = [imp_ref[g, SUBLANES * r:SUBLANES * (r + 1), :] for r in range(n_used)]
        rank = [jnp.zeros((SUBLANES, TQ), F32) for _ in range(n_used)]
        for k in range(n_used * SUBLANES):
            rowk = imp_ref[g, k:k + 1, :]
            kg = k // SUBLANES
            for r in range(n_used):
                if r < kg:
                    beats = jnp.where(rowk > grp[r], 1.0, 0.0)
                elif r > kg:
                    beats = jnp.where(rowk >= grp[r], 1.0, 0.0)
                else:
                    beats = jnp.where(blk8 > k - SUBLANES * kg,
                                      jnp.where(rowk >= grp[r], 1.0, 0.0),
                                      jnp.where(rowk > grp[r], 1.0, 0.0))
                rank[r] = rank[r] + beats
        rank = jnp.concatenate(rank, axis=0)
        flags = jnp.where(rank < float(N_SEL), 0.0, -1.0).astype(BF16)
        pad = HEAD_DIM - n_used * SUBLANES
        if pad:
            flags = jnp.concatenate([flags, jnp.zeros((pad, TQ), BF16)], axis=0)
        return flags

    for hh in heads:
        qa_ref[hh, 0:HEAD_DIM, :] = qt_ref[0, hh, 0:HEAD_DIM, :]
    per_tile = TQ // L_SLC
    no_rank_tiles = N_SEL // per_tile

    @pl.when(i < no_rank_tiles)
    def _():
        for hh in heads:
            qa_ref[hh, HEAD_DIM:, :] = jnp.zeros((HEAD_DIM, TQ), BF16)

    lo = no_rank_tiles
    while lo * per_tile < ns:
        hi = min(lo + RANK_TILE_SPAN, ns // per_tile)
        n_used = -(-hi * per_tile // SUBLANES)

        @pl.when((i >= lo) & (i < hi))
        def _(n_used=n_used):
            for g in range(N_KV_HEADS):
                flags = block_flags(g, n_used)
                for hh in range(g * HPG, (g + 1) * HPG):
                    qa_ref[hh, HEAD_DIM:, :] = flags
        lo = hi

    n_far = jnp.maximum(i - 1, 0)

    def far_group(jj, carry):
        tasks = []
        for u in range(FAR_GROUP):
            tasks += slc_tasks(FAR_GROUP * jj + u)
        run(tasks)
        return carry

    lax.fori_loop(0, n_far // FAR_GROUP, far_group, 0)
    for rem in range(FAR_GROUP):
        @pl.when(n_far % FAR_GROUP == rem)
        def _(rem=rem):
            tasks = []
            for u in range(rem):
                tasks += slc_tasks(n_far - rem + u)
            run(tasks + slc_tasks(jnp.maximum(i - 1, 0), rows1) + slc_tasks(i, 2 * TQ))

    gates_t = jnp.transpose(ng_ref[...])
    outs = []
    for hh in heads:
        a_s, a_w = accs_ref[hh], accw_ref[hh]
        o_s = a_s[0:HEAD_DIM] * (1.0 / a_s[HEAD_DIM:HEAD_DIM + 1])
        o_w = a_w[0:HEAD_DIM] * (1.0 / a_w[HEAD_DIM:HEAD_DIM + 1])
        row = (hh // HPG) * LANES + 3 * (hh % HPG)
        outs.append(gates_t[row:row + 1] * oc_ref[hh] + gates_t[row + 1:row + 2] * o_s
                    + gates_t[row + 2:row + 3] * o_w)
    o_ref[...] = jnp.transpose(jnp.concatenate(outs, axis=0)).astype(BF16)


def _bias_of(thr_ref, rb_ref, dist, h):
    b = jnp.full(dist.shape, rb_ref[h], F32)
    for k in range(1, N_BUCKETS):
        b = jnp.where(dist >= thr_ref[k], rb_ref[k * N_HEADS + h], b)
    return b


def _cmp_bias_kernel(thr_ref, rb_ref, biasc_ref, *, nc, ncp):
    i = pl.program_id(0)
    per_tile = TQ // D_STRIDE
    band = 2 * per_tile
    assert TQ + D_STRIDE - (L_CMP - 1) >= MAX_DISTANCE

    def dist_of(c0, rows):
        c = c0 + lax.broadcasted_iota(jnp.int32, (rows, TQ), 0)
        r = lax.broadcasted_iota(jnp.int32, (rows, TQ), 1)
        dist = i * TQ + r - (c * D_STRIDE + L_CMP - 1)
        return dist, (dist >= 0) & (c < nc)

    _, ok_all = dist_of(0, ncp)
    band0 = pl.multiple_of(jnp.maximum(i * per_tile - per_tile, 0), per_tile)
    dist_b, ok_b = dist_of(band0, band)
    for h in range(N_HEADS):
        biasc_ref[h] = jnp.where(ok_all, rb_ref[(N_BUCKETS - 1) * N_HEADS + h], NEG)
        biasc_ref[h, pl.ds(band0, band), :] = jnp.where(
            ok_b, _bias_of(thr_ref, rb_ref, dist_b, h), NEG)


def _diag_bias_kernel(thr_ref, rb_ref, bd_ref):
    d0 = (lax.broadcasted_iota(jnp.int32, (TQ, TQ), 1)
          - lax.broadcasted_iota(jnp.int32, (TQ, TQ), 0))
    for h in range(N_HEADS):
        far = rb_ref[(N_BUCKETS - 1) * N_HEADS + h]
        bd_ref[h, 0:TQ, :] = jnp.where(d0 < 0, _bias_of(thr_ref, rb_ref, d0 + 2 * TQ, h) - far, NEG)
        bd_ref[h, TQ:2 * TQ, :] = _bias_of(thr_ref, rb_ref, d0 + TQ, h) - far
        bd_ref[h, 2 * TQ:3 * TQ, :] = jnp.where(d0 >= 0, _bias_of(thr_ref, rb_ref, d0, h) - far, NEG)
        bd_ref[h, 3 * TQ:, :] = jnp.full((TQ, TQ), NEG, F32)


def _attention_tables(rel_bias, seq):
    ncp = seq // D_STRIDE
    nc = (seq - L_CMP) // D_STRIDE + 1
    ns = seq // L_SLC
    nt = seq // TQ
    n_probe = 2 * MAX_DISTANCE
    buckets = _t5_bucket(jnp.arange(n_probe))
    thr = jnp.sum(buckets[None, :] < jnp.arange(N_BUCKETS)[:, None], axis=1).astype(jnp.int32)
    assert WINDOW == 2 * TQ
    rb = (rel_bias.astype(F32) * LOG2E).reshape(-1)
    bias_c = pl.pallas_call(
        functools.partial(_cmp_bias_kernel, nc=nc, ncp=ncp),
        out_shape=jax.ShapeDtypeStruct((N_HEADS, ncp, seq), F32),
        grid_spec=pltpu.PrefetchScalarGridSpec(
            num_scalar_prefetch=2, grid=(nt,), in_specs=[],
            out_specs=pl.BlockSpec((N_HEADS, ncp, TQ), lambda i, *_: (0, 0, i))),
        compiler_params=_cparams(1), name="cmp_bias")(thr, rb)
    bd = pl.pallas_call(
        _diag_bias_kernel, out_shape=jax.ShapeDtypeStruct((N_HEADS, 4 * TQ, TQ), F32),
        grid_spec=pltpu.PrefetchScalarGridSpec(
            num_scalar_prefetch=2, grid=(1,), in_specs=[],
            out_specs=pl.BlockSpec((N_HEADS, 4 * TQ, TQ), lambda i, *_: (0, 0, 0))),
        compiler_params=_cparams(1), name="diag_bias")(thr, rb)
    c_start = jnp.arange(ncp) * D_STRIDE
    c_end = c_start + L_CMP - 1
    s_start = jnp.arange(ns) * L_SLC
    aggt = ((c_end[None, :] >= s_start[:, None]) & (c_start[None, :] <= s_start[:, None] + L_SLC - 1)
            & (jnp.arange(ncp)[None, :] < nc)).astype(BF16)
    return bias_c, bd, aggt


def _nsa(qt, kcmp, vcmpt, ks, vst, kw, vwt, ng, rel_bias, batch, seq):
    first_far = TQ + 1
    assert MAX_EXACT + int(math.log(first_far / MAX_EXACT) / math.log(MAX_DISTANCE / MAX_EXACT)
                           * (N_BUCKETS - MAX_EXACT)) >= N_BUCKETS - 1, "far tiles need one bucket"
    ncp = seq // D_STRIDE
    ns = seq // L_SLC
    nt = seq // TQ
    bias_c, bd, aggt = _attention_tables(rel_bias, seq)
    kspec = pl.BlockSpec((1, N_KV_HEADS, seq, LANES), lambda b, i: (b, 0, 0, 0))
    vtspec = pl.BlockSpec((1, N_KV_HEADS, LANES, seq), lambda b, i: (b, 0, 0, 0))
    state = lambda rows: pltpu.VMEM((N_HEADS, rows, TQ), F32)
    const = lambda shape: pl.BlockSpec(shape, lambda b, i: (0,) * len(shape),
                                       pipeline_mode=pl.Buffered(1))
    grid_spec = pltpu.PrefetchScalarGridSpec(
        num_scalar_prefetch=0, grid=(batch, nt),
        in_specs=[pl.BlockSpec((1, N_HEADS, LANES, TQ), lambda b, i: (b, 0, 0, i)),
                  pl.BlockSpec((1, ncp, N_KV_HEADS * LANES), lambda b, i: (b, 0, 0)),
                  pl.BlockSpec((1, N_KV_HEADS, LANES, ncp), lambda b, i: (b, 0, 0, 0)),
                  pl.BlockSpec((N_HEADS, ncp, TQ), lambda b, i: (0, 0, i)),
                  kspec, vtspec, kspec, vtspec,
                  pl.BlockSpec((TQ, N_KV_HEADS * LANES), lambda b, i: (b * nt + i, 0)),
                  const((ns, ncp)), const((N_HEADS, 4 * TQ, TQ))],
        out_specs=pl.BlockSpec((TQ, N_HEADS * HEAD_DIM), lambda b, i: (b * nt + i, 0)),
        scratch_shapes=[pltpu.VMEM((N_KV_HEADS, ns, TQ), F32),
                        pltpu.VMEM((N_HEADS, LANES, TQ), BF16),
                        state(HEAD_DIM),
                        state(1), state(PV_ROWS),
                        state(1), state(PV_ROWS)])
    return pl.pallas_call(
        functools.partial(_nsa_kernel, ns=ns),
        out_shape=jax.ShapeDtypeStruct((batch * seq, N_HEADS * HEAD_DIM), BF16),
        grid_spec=grid_spec, compiler_params=_cparams(2), name="nsa",
    )(qt, kcmp, vcmpt, bias_c, ks, vst, kw, vwt, ng, aggt, bd)


def _ffn_kernel(x_ref, ya_ref, o_ref, gb_ref, p_ref, wao_ref, wout_ref, nf_ref, wup_ref, cw_ref,
                cb_ref, wd_ref, np_ref, wpg_ref, wpe_ref, nfin_ref, out_ref, carry_ref,
                *, tm, tiles_per_seq, d_ff, chunks):
    r = pl.program_id(0)

    @pl.when((r % tiles_per_seq) == 0)
    def _():
        carry_ref[...] = jnp.zeros_like(carry_ref)

    n = tm // FFN_SPLIT
    groups = [slice(k * n, (k + 1) * n) for k in range(FFN_SPLIT)]

    x1, hf = [], []
    for rows in groups:
        y = (ya_ref[rows, :].astype(F32)
             + gb_ref[rows, :].astype(F32) * _dot(o_ref[rows, :], wao_ref[...]))
        x1.append(x_ref[rows, :] + _dot(y.astype(BF16), wout_ref[...]))
        hf.append(_rms(x1[-1], nf_ref[...]).astype(BF16))

    def conv3(up, prev, c0, width):
        rowi = lax.broadcasted_iota(jnp.int32, up.shape, 0)
        s1 = jnp.where(rowi == 0, prev[7:8, :], pltpu.roll(up, 1, 0))
        s2 = jnp.where(rowi == 0, prev[6:7, :],
                       jnp.where(rowi == 1, prev[7:8, :], pltpu.roll(up, 2, 0)))
        return (cw_ref[0:1, c0:c0 + width] * s2 + cw_ref[1:2, c0:c0 + width] * s1
                + cw_ref[2:3, c0:c0 + width] * up + cb_ref[:, c0:c0 + width])

    acc = [None] * FFN_SPLIT
    for c0, width in chunks:
        cols = (slice(c0, c0 + width), slice(d_ff + c0, d_ff + c0 + width))
        prev = [carry_ref[:, cs] for cs in cols]
        for k in range(FFN_SPLIT):
            ups = [_dot(hf[k], wup_ref[:, cs]) for cs in cols]
            gate, val = [conv3(ups[t], prev[t], cols[t].start, width) for t in range(2)]
            prev = [u[n - 8:, :] for u in ups]
            part = _dot((_gelu(gate) * val).astype(BF16), wd_ref[c0:c0 + width, :])
            acc[k] = part if acc[k] is None else acc[k] + part
        for t in range(2):
            carry_ref[:, cols[t]] = prev[t]

    for k, rows in enumerate(groups):
        x2 = x1[k] + acc[k]
        pg = _sigmoid(_dot(_rms(x2, np_ref[...]).astype(BF16), wpg_ref[...]))
        pe = _dot(p_ref[rows, :].astype(BF16), wpe_ref[...])
        out_ref[rows, :] = _rms(x2 + pg * pe, nfin_ref[...])


def _ffn(x2, ya, o, gb, p2, w_attn_out, w_out, norm_ffn, w_up, ffn_dw_w, ffn_dw_b, w_down,
         norm_ple, w_ple_gate, w_ple, norm_final, seq, p_tile0=0, tm=512):
    t, d = x2.shape
    d_ff = w_down.shape[0]
    assert d_ff % MXU_DEPTH == 0
    chunks, c0 = [], 0
    while c0 < d_ff:
        width = min(FFN_CHUNK, d_ff - c0)
        chunks.append((c0, width))
        c0 += width
    tiles_per_seq = seq // tm
    cw = jnp.concatenate([ffn_dw_w, jnp.zeros((8 - FFN_CONV_WIDTH, 2 * d_ff), ffn_dw_w.dtype)], 0)
    vec = lambda v: v.reshape(1, -1)
    const = lambda a: pl.BlockSpec(a.shape, lambda r: (0, 0), pipeline_mode=pl.Buffered(1))
    rows = lambda n: pl.BlockSpec((tm, n), lambda r: (r, 0))
    kern = functools.partial(_ffn_kernel, tm=tm, tiles_per_seq=tiles_per_seq, d_ff=d_ff,
                             chunks=tuple(chunks))
    operands = (x2, ya, o, gb, p2, w_attn_out, w_out, vec(norm_ffn), w_up, cw, vec(ffn_dw_b),
                w_down, vec(norm_ple), w_ple_gate, w_ple, vec(norm_final))
    return pl.pallas_call(
        kern, out_shape=jax.ShapeDtypeStruct((t, d), F32), grid=(t // tm,),
        in_specs=[rows(d), rows(d), rows(o.shape[1]), rows(d),
                  pl.BlockSpec((tm, p2.shape[1]), lambda r: (p_tile0 + r, 0))]
        + [const(a) for a in operands[5:]],
        out_specs=rows(d),
        scratch_shapes=[pltpu.VMEM((8, 2 * d_ff), F32)],
        compiler_params=_cparams(1), name="ffn",
    )(*operands)


def kernel(x, p, rel_bias, norm_mix, w_in, conv_dw_w, conv_dw_b, conv_ln_g, conv_ln_b, w_conv_out,
           cmp_pe_k, cmp_pe_v, w_ck1, w_ck2, w_cv1, w_cv2, w_attn_out, w_out, norm_ffn, w_up,
           ffn_dw_w, ffn_dw_b, w_down, norm_ple, w_ple_gate, w_ple, norm_final):
    batch, seq, d = x.shape
    depth = w_in.shape[0]
    x2 = x.reshape(batch * seq, d)
    for i in range(depth):
        ya, qt, kc, vc, ks, vst, kw, vwt, ng, gb = _inproj(
            x2, norm_mix[i], _prep_inproj_weights(w_in[i], d), conv_dw_w[i], conv_dw_b[i],
            conv_ln_g[i], conv_ln_b[i], w_conv_out[i].astype(BF16), batch, seq)
        kcmp, vcmpt = _compress(kc, vc,
                               _prep_compress_weights(cmp_pe_k[i], w_ck1[i], w_ck2[i]),
                               _prep_compress_weights(cmp_pe_v[i], w_cv1[i], w_cv2[i]),
                               batch, seq)
        o = _nsa(qt, kcmp, vcmpt, ks, vst, kw, vwt, ng, rel_bias, batch, seq)
        assert i == depth - 1, "the final RMSNorm is fused into the (single) layer's MLP kernel"
        ffn_tm = 512
        x2 = _ffn(x2, ya, o, gb, p.reshape(depth * batch * seq, -1), w_attn_out[i].astype(BF16),
                  w_out[i].astype(BF16), norm_ffn[i], w_up[i].astype(BF16), ffn_dw_w[i],
                  ffn_dw_b[i], w_down[i].astype(BF16), norm_ple[i], w_ple_gate[i].astype(BF16),
                  w_ple[i].astype(BF16), norm_final, seq,
                  p_tile0=i * (batch * seq // ffn_tm), tm=ffn_tm)
    return x2.reshape(batch, seq, d)
```

```python
import functools
import math

import jax
import jax.numpy as jnp
from jax import lax
from jax.experimental import pallas as pl
from jax.experimental.pallas import tpu as pltpu

N_HEADS = 8
HEAD_DIM = 64
N_KV_HEADS = 2
HPG = N_HEADS // N_KV_HEADS
L_CMP = 32
D_STRIDE = 16
CMP_HIDDEN = 256
L_SLC = 64
N_SEL = 16
WINDOW = 512
N_BUCKETS = 32
MAX_EXACT = N_BUCKETS // 2
MAX_DISTANCE = 128
CONV_WIDTH = 31
FFN_CONV_WIDTH = 3
EPS = 1e-6
FORCE = 1e4

LANES = 128
MXU_DEPTH = 256
FFN_CHUNK = 6 * MXU_DEPTH
FFN_SPLIT = 2
NEG = -1e30
M_INIT = -1e29
LOG2E = 1.4426950408889634
PIPE_DEPTH = 6
RANK_TILE_SPAN = 4
FAR_GROUP = 4
TQ = 256
PV_ROWS = HEAD_DIM + 16
SUBLANES = 8
CONV_HALO = 32
CONV_ROWS = 64
VMEM_LIMIT = 56 * 1024 * 1024

F32 = jnp.float32
BF16 = jnp.bfloat16


def _cparams(n_axes):
    return pltpu.CompilerParams(dimension_semantics=("arbitrary",) * n_axes,
                                vmem_limit_bytes=VMEM_LIMIT)


def _dot(a, b):
    return jnp.dot(a, b, preferred_element_type=F32)


def _rms(xf, g):
    return xf * lax.rsqrt(jnp.mean(xf * xf, axis=-1, keepdims=True) + EPS) * g


def _sigmoid(x):
    return 1.0 / (1.0 + jnp.exp(-x))


def _gelu(x):
    return 0.5 * x * (1.0 + jnp.tanh(0.7978845608028654 * (x + 0.044715 * x * x * x)))


def _t5_bucket(dist):
    n = jnp.maximum(dist, 0)
    nf = jnp.maximum(n, MAX_EXACT).astype(F32)
    large = MAX_EXACT + (jnp.log(nf / MAX_EXACT) / math.log(MAX_DISTANCE / MAX_EXACT)
                         * (N_BUCKETS - MAX_EXACT)).astype(jnp.int32)
    large = jnp.minimum(large, N_BUCKETS - 1)
    return jnp.where(n < MAX_EXACT, n, large)


def _inproj_kernel(x_ref, g_ref, wu_ref, wq_ref, wkv_ref, wng_ref, wmg_ref,
                   cw_ref, cb_ref, lg_ref, lb_ref, wco_ref,
                   ya_ref, qt_ref, kc_ref, vc_ref, ks_ref, vst_ref, kw_ref, vwt_ref, ng_ref, gb_ref,
                   cs_ref, ext_ref, sh_ref, ga_ref, a_stage_ref, ga_stage_ref,
                   *, tm, n_tiles, tiles_per_seq, d_conv):
    step = pl.program_id(0)
    r = jnp.minimum(step, n_tiles - 1)

    @pl.when(step == 0)
    def _():
        ext_ref[...] = jnp.zeros(ext_ref.shape, F32)
        ga_ref[...] = jnp.zeros(ga_ref.shape, BF16)

    span = sh_ref.shape[1]
    for ph in range(1, SUBLANES):
        sh_ref[ph - 1] = ext_ref[ph:ph + span, :]
    off = CONV_HALO - (CONV_WIDTH - 1)
    conv = []
    for r0 in range(0, tm, CONV_ROWS):
        acc = jnp.zeros((CONV_ROWS, d_conv), F32) + cb_ref[...]
        for j in range(CONV_WIDTH):
            ph = (off + j) % SUBLANES
            base = r0 + off + j - ph
            if ph == 0:
                xs = ext_ref[base:base + CONV_ROWS, :]
            else:
                xs = sh_ref[ph - 1, base:base + CONV_ROWS, :]
            acc = acc + jnp.tile(cw_ref[j], (CONV_ROWS // SUBLANES, 1)) * xs
        conv.append(acc)
    conv = jnp.concatenate(conv, axis=0)
    mu = jnp.mean(conv, axis=-1, keepdims=True)
    cen = conv - mu
    var = jnp.mean(cen * cen, axis=-1, keepdims=True)
    y = cen * lax.rsqrt(var + EPS) * lg_ref[...] + lb_ref[...]
    y = (y * _sigmoid(y)).astype(BF16)

    h = _rms(x_ref[...], g_ref[...]).astype(BF16)
    u = _dot(h, wu_ref[...])
    a_stage_ref[...] = u[:, :d_conv] * _sigmoid(u[:, d_conv:])
    zmg = _dot(h, wmg_ref[...])
    d_model = zmg.shape[1] // 2
    ga_stage_ref[...] = _sigmoid(zmg[:, :d_model]).astype(BF16)
    gb_ref[...] = _sigmoid(zmg[:, d_model:]).astype(BF16)

    zqt = jnp.transpose(_dot(h, wq_ref[...]))
    zero_rows = jnp.zeros((HEAD_DIM, tm), BF16)
    for hh in range(N_HEADS):
        qt_ref[0, hh, 0:HEAD_DIM, :] = zqt[hh * HEAD_DIM:(hh + 1) * HEAD_DIM].astype(BF16)
        qt_ref[0, hh, HEAD_DIM:, :] = zero_rows

    zkv = _dot(h, wkv_ref[...])
    for which, out_ref in enumerate((kc_ref, vc_ref)):
        cs_ref[which] = zkv[:, which * LANES:(which + 1) * LANES]
        for tok in range(D_STRIDE):
            rows = cs_ref[which, pl.ds(tok, tm // D_STRIDE, stride=D_STRIDE), :]
            out_ref[:, tok * LANES:(tok + 1) * LANES] = rows.astype(BF16)
    lane = lax.broadcasted_iota(jnp.int32, (tm, LANES), 1)
    row = lax.broadcasted_iota(jnp.int32, (tm, LANES), 0)
    spos = (r % tiles_per_seq) * tm + row
    lo = lane < HEAD_DIM
    blk_tag = jnp.where((lane - HEAD_DIM) == spos // L_SLC, -NEG, 0.0)
    ones_rows = jnp.where(lax.broadcasted_iota(jnp.int32, (HEAD_DIM, tm), 0) < 8, 1.0, 0.0)
    ones_rows = ones_rows.astype(BF16)
    for g in range(N_KV_HEADS):
        for k_ref, vt_ref, base, tag in ((ks_ref, vst_ref, (2 + g) * LANES, blk_tag),
                                         (kw_ref, vwt_ref, (4 + g) * LANES, 0.0)):
            pair = zkv[:, base:base + LANES]
            k_ref[0, g] = jnp.where(lo, pair, tag).astype(BF16)
            vt_ref[0, g, 0:HEAD_DIM, :] = jnp.transpose(pair)[HEAD_DIM:].astype(BF16)
            vt_ref[0, g, HEAD_DIM:, :] = ones_rows

    ng_ref[...] = _sigmoid(_dot(h, wng_ref[...]))

    ya_ref[...] = (ga_ref[...].astype(F32) * _dot(y, wco_ref[...])).astype(BF16)

    seq_start = (r % tiles_per_seq) == 0
    ext_ref[0:CONV_HALO, :] = jnp.where(seq_start, 0.0, ext_ref[tm:tm + CONV_HALO, :])
    ext_ref[CONV_HALO:, :] = a_stage_ref[...]
    ga_ref[...] = ga_stage_ref[...]


def _prep_inproj_weights(w_in, d_model):
    d_conv = d_model // 2
    n_conv = 2 * d_conv
    n_q = N_HEADS * HEAD_DIM
    n_kv = 6 * N_KV_HEADS * HEAD_DIM
    n_ng = 3 * N_HEADS
    o = 0
    wu = w_in[:, o:o + n_conv]; o += n_conv
    wq = w_in[:, o:o + n_q]; o += n_q
    wkv = w_in[:, o:o + n_kv]; o += n_kv
    wng = w_in[:, o:o + n_ng]; o += n_ng
    wmg = w_in[:, o:]
    wq_p = wq * (LOG2E / math.sqrt(HEAD_DIM))
    kvcols = [wkv[:, 0:LANES], wkv[:, LANES:2 * LANES]]
    for k_kind in (2, 4):
        for g in range(N_KV_HEADS):
            for kind in (k_kind, k_kind + 1):
                c0 = kind * N_KV_HEADS * HEAD_DIM + g * HEAD_DIM
                kvcols.append(wkv[:, c0:c0 + HEAD_DIM])
    wkv_p = jnp.concatenate(kvcols, axis=1)
    per_g = HPG * 3
    ngcols = []
    for g in range(N_KV_HEADS):
        ngcols += [wng[:, g * per_g:(g + 1) * per_g],
                   jnp.zeros((d_model, LANES - per_g), w_in.dtype)]
    wng_p = jnp.concatenate(ngcols, axis=1)
    return tuple(w.astype(BF16) for w in (wu, wq_p, wkv_p, wng_p, wmg))


def _inproj(x2, norm_g, weights, conv_w, conv_b, ln_g, ln_b, w_conv_out, batch, seq, tm=512):
    t, d = x2.shape
    wu, wq, wkv, wng, wmg = weights
    d_conv = d // 2
    tiles_per_seq = seq // tm
    n_tiles = t // tm
    assert tm % CONV_ROWS == 0 and CONV_HALO >= CONV_WIDTH - 1 and CONV_HALO % SUBLANES == 0
    cw = jnp.broadcast_to(conv_w[:, None, :], (CONV_WIDTH, SUBLANES, d_conv))
    vec = lambda v: v.reshape(1, -1)
    full = lambda w: pl.BlockSpec(w.shape, lambda s: (0,) * w.ndim, pipeline_mode=pl.Buffered(1))
    tile = lambda s: jnp.minimum(s, n_tiles - 1)
    row = lambda n: pl.BlockSpec((tm, n), lambda s: (tile(s), 0))
    row_prev = lambda n: pl.BlockSpec((tm, n), lambda s: (jnp.maximum(s - 1, 0), 0))
    headed = lambda nh: pl.BlockSpec(
        (1, nh, tm, LANES), lambda s: (tile(s) // tiles_per_seq, 0, tile(s) % tiles_per_seq, 0))
    headed_t = lambda nh: pl.BlockSpec(
        (1, nh, LANES, tm), lambda s: (tile(s) // tiles_per_seq, 0, 0, tile(s) % tiles_per_seq))
    k_shape = jax.ShapeDtypeStruct((batch, N_KV_HEADS, seq, LANES), BF16)
    vt_shape = jax.ShapeDtypeStruct((batch, N_KV_HEADS, LANES, seq), BF16)
    out_shape = (
        jax.ShapeDtypeStruct((t, d), BF16),
        jax.ShapeDtypeStruct((batch, N_HEADS, LANES, seq), BF16),
        jax.ShapeDtypeStruct((t // D_STRIDE, D_STRIDE * LANES), BF16),
        jax.ShapeDtypeStruct((t // D_STRIDE, D_STRIDE * LANES), BF16),
        k_shape, vt_shape, k_shape, vt_shape,
        jax.ShapeDtypeStruct((t, N_KV_HEADS * LANES), F32),
        jax.ShapeDtypeStruct((t, d), BF16),
    )
    blocks = pl.BlockSpec((tm // D_STRIDE, D_STRIDE * LANES), lambda s: (tile(s), 0))
    out_specs = (row_prev(d), headed_t(N_HEADS), blocks, blocks,
                 headed(N_KV_HEADS), headed_t(N_KV_HEADS), headed(N_KV_HEADS),
                 headed_t(N_KV_HEADS), row(N_KV_HEADS * LANES), row(d))
    kern = functools.partial(_inproj_kernel, tm=tm, n_tiles=n_tiles, tiles_per_seq=tiles_per_seq,
                             d_conv=d_conv)
    consts = (vec(norm_g), wu, wq, wkv, wng, wmg, cw, vec(conv_b), vec(ln_g), vec(ln_b), w_conv_out)
    return pl.pallas_call(
        kern, out_shape=out_shape, grid=(n_tiles + 1,),
        in_specs=[row(d)] + [full(w) for w in consts],
        out_specs=out_specs,
        scratch_shapes=[pltpu.VMEM((2, tm, LANES), F32),
                        pltpu.VMEM((tm + CONV_HALO, d_conv), F32),
                        pltpu.VMEM((SUBLANES - 1, tm + CONV_HALO - SUBLANES, d_conv), F32),
                        pltpu.VMEM((tm, d), BF16),
                        pltpu.VMEM((tm, d_conv), F32), pltpu.VMEM((tm, d), BF16)],
        compiler_params=_cparams(1), name="inproj",
    )(x2, *consts)


def _compress_kernel(rk_ref, rv_ref, pek_ref, pev_ref, w1k_ref, w1v_ref, w2k_ref, w2v_ref,
                     kcmp_ref, vcmp_ref, *, ncp):
    def one(r_ref, pe_ref, w1_ref, w2_ref):
        r = r_ref[0]
        top = _dot(r, w1_ref[0])
        bot = _dot(r, w1_ref[1])
        pe_h = _dot(pe_ref[0], w1_ref[0]) + _dot(pe_ref[1], w1_ref[1])
        nxt = pltpu.roll(bot, ncp - 1, 0)
        rowi = lax.broadcasted_iota(jnp.int32, top.shape, 0)
        hid = top + jnp.where(rowi == ncp - 1, 0.0, nxt) + pe_h[0:1, :]
        return _dot(_gelu(hid).astype(BF16), w2_ref[...])

    kcmp_ref[0] = one(rk_ref, pek_ref, w1k_ref, w2k_ref).astype(BF16)
    v = one(rv_ref, pev_ref, w1v_ref, w2v_ref)
    rowt = lax.broadcasted_iota(jnp.int32, (LANES, ncp), 0)
    ones_rows = (rowt >= HEAD_DIM) & (rowt < HEAD_DIM + 8)
    for g in range(N_KV_HEADS):
        vt = jnp.transpose(v[:, g * LANES:(g + 1) * LANES])
        vcmp_ref[0, g] = jnp.where(ones_rows, 1.0, vt).astype(BF16)


def _prep_compress_weights(pe, w1, w2):
    half = L_CMP // 2
    w1r = w1.reshape(L_CMP, HEAD_DIM, CMP_HIDDEN).astype(BF16)
    w2p = jnp.pad(w2, ((0, 0), (0, LANES - HEAD_DIM))).astype(BF16)

    def per_group(w, axis):
        z = jnp.zeros_like(w)
        return jnp.stack([jnp.concatenate([w if k == g else z for k in range(N_KV_HEADS)], axis=-1)
                          for g in range(N_KV_HEADS)], axis=axis)

    w1big = per_group(w1r, 1).reshape(2, half * N_KV_HEADS * HEAD_DIM, N_KV_HEADS * CMP_HIDDEN)
    w2big = per_group(w2p, 0).reshape(N_KV_HEADS * CMP_HIDDEN, N_KV_HEADS * LANES)
    per = pe.reshape(2, half, 1, HEAD_DIM)
    pebig = jnp.broadcast_to(per, (2, half, N_KV_HEADS, HEAD_DIM)).reshape(2, 1, -1)
    pebig = jnp.broadcast_to(pebig, (2, 8, pebig.shape[-1])).astype(BF16)
    return pebig, w1big, w2big


def _compress(kc, vc, wk, wv, batch, seq):
    ncp = seq // D_STRIDE
    width = D_STRIDE * LANES
    rk = kc.reshape(batch, ncp, width)
    rv = vc.reshape(batch, ncp, width)
    pek, w1k, w2k = wk
    pev, w1v, w2v = wv
    c3 = lambda a: pl.BlockSpec(a.shape, lambda b: (0, 0, 0))
    c2 = lambda a: pl.BlockSpec(a.shape, lambda b: (0, 0))
    rspec = pl.BlockSpec((1, ncp, width), lambda b: (b, 0, 0))
    kspec = pl.BlockSpec((1, ncp, N_KV_HEADS * LANES), lambda b: (b, 0, 0))
    kshape = jax.ShapeDtypeStruct((batch, ncp, N_KV_HEADS * LANES), BF16)
    vspec = pl.BlockSpec((1, N_KV_HEADS, LANES, ncp), lambda b: (b, 0, 0, 0))
    vshape = jax.ShapeDtypeStruct((batch, N_KV_HEADS, LANES, ncp), BF16)
    return pl.pallas_call(
        functools.partial(_compress_kernel, ncp=ncp), out_shape=(kshape, vshape), grid=(batch,),
        in_specs=[rspec, rspec, c3(pek), c3(pev), c3(w1k), c3(w1v), c2(w2k), c2(w2v)],
        out_specs=(kspec, vspec), compiler_params=_cparams(1), name="compress",
    )(rk, rv, pek, pev, w1k, w1v, w2k, w2v)


def _nsa_kernel(qt_ref, kcmp_ref, vcmpt_ref, biasc_ref, ks_ref, vst_ref, kw_ref, vwt_ref,
                ng_ref, aggt_ref, bd_ref, o_ref,
                imp_ref, qa_ref, oc_ref, ms_ref, accs_ref, mw_ref, accw_ref, *, ns):
    i = pl.program_id(1)
    heads = range(N_HEADS)

    def k_tile(ref, g, j):
        return ref[0, g, pl.ds(pl.multiple_of(j * TQ, TQ), TQ), :]

    def vt_tile(ref, g, j):
        return ref[0, g, 0:PV_ROWS, pl.ds(pl.multiple_of(j * TQ, TQ), TQ)]

    def pipelined(tasks, scores, update):
        pending = [scores(t) for t in tasks[:PIPE_DEPTH]]
        for t, task in enumerate(tasks):
            update(task, pending.pop(0))
            if t + PIPE_DEPTH < len(tasks):
                pending.append(scores(tasks[t + PIPE_DEPTH]))

    def run(tasks):
        def scores(task):
            hh, kt, _, qmat, _, _, row0 = task
            s = _dot(kt, qmat)
            if row0 is None:
                return s
            if not isinstance(row0, int):
                row0 = pl.multiple_of(row0, TQ)
            return s + bd_ref[hh, pl.ds(row0, TQ), :]

        def update(task, s):
            hh, _, vt, _, m_ref, acc_ref, _ = task
            m_old = m_ref[hh]
            m_new, p = [], []
            for c0 in range(0, TQ, LANES):
                s_h = s[:, c0:c0 + LANES]
                m_h = jnp.maximum(m_old[:, c0:c0 + LANES], jnp.max(s_h, axis=0, keepdims=True))
                m_new.append(m_h)
                p.append(jnp.exp2(s_h - m_h).astype(BF16))
            m_new = jnp.concatenate(m_new, axis=1)
            p = jnp.concatenate(p, axis=1)
            acc_ref[hh] = jnp.exp2(m_old - m_new) * acc_ref[hh] + _dot(vt, p)
            m_ref[hh] = m_new

        pipelined(tasks, scores, update)

    def slc_tasks(j, row0=None):
        tiles = [(k_tile(ks_ref, g, j), vt_tile(vst_ref, g, j)) for g in range(N_KV_HEADS)]
        return [(hh,) + tiles[hh // HPG] + (qa_ref[hh], ms_ref, accs_ref, row0) for hh in heads]

    def win_tasks(j, row0):
        tiles = [(k_tile(kw_ref, g, j), vt_tile(vwt_ref, g, j)) for g in range(N_KV_HEADS)]
        return [(hh,) + tiles[hh // HPG] + (qt_ref[0, hh], mw_ref, accw_ref, row0) for hh in heads]

    for hh in heads:
        ms_ref[hh] = jnp.full((1, TQ), M_INIT, F32)
        mw_ref[hh] = jnp.full((1, TQ), M_INIT, F32)
        accs_ref[hh] = jnp.zeros((PV_ROWS, TQ), F32)
        accw_ref[hh] = jnp.zeros((PV_ROWS, TQ), F32)

    p_parts = []

    def cmp_scores(hh):
        g = hh // HPG
        return _dot(kcmp_ref[0, :, g * LANES:(g + 1) * LANES], qt_ref[0, hh]) + biasc_ref[hh]

    def cmp_update(hh, s_c):
        m_c = jnp.maximum(jnp.max(s_c, axis=0, keepdims=True), M_INIT)
        p_c = jnp.exp2(s_c - m_c)
        l_c = jnp.sum(p_c, axis=0, keepdims=True)
        p_c = p_c * jnp.where(l_c > 0.0, 1.0 / l_c, 0.0)
        oc_ref[hh] = _dot(vcmpt_ref[0, hh // HPG, 0:HEAD_DIM, :], p_c.astype(BF16))
        p_parts.append(p_c)

    pipelined(list(heads), cmp_scores, cmp_update)

    blk = lax.broadcasted_iota(jnp.int32, (ns, TQ), 0)
    cur = (i * TQ + lax.broadcasted_iota(jnp.int32, (ns, TQ), 1)) // L_SLC
    forced = (blk == 0) | (blk == cur) | (blk == cur - 1)
    for g in range(N_KV_HEADS):
        p4 = p_parts[g * HPG:(g + 1) * HPG]
        p_sum = (p4[0] + p4[1]) + (p4[2] + p4[3])
        p_hi = p_sum.astype(BF16)
        p_lo = (p_sum - p_hi.astype(F32)).astype(BF16)
        imp = _dot(aggt_ref[...], p_hi) + _dot(aggt_ref[...], p_lo)
        imp_ref[g] = jnp.where(forced, FORCE, jnp.where(blk <= cur, imp, -FORCE))

    rows2 = jnp.where(i >= 2, 0, 3 * TQ)
    rows1 = jnp.where(i >= 1, TQ, 3 * TQ)
    run(win_tasks(jnp.maximum(i - 2, 0), rows2) + win_tasks(jnp.maximum(i - 1, 0), rows1)
        + win_tasks(i, 2 * TQ))

    blk8 = lax.broadcasted_iota(jnp.int32, (SUBLANES, TQ), 0)

    def block_flags(g, n_used):
        grp = [imp_ref[g, SUBLANES * r:SUBLANES * (r + 1), :] for r in range(n_used)]
        rank = [jnp.zeros((SUBLANES, TQ), F32) for _ in range(n_used)]
        for k in range(n_used * SUBLANES):
            rowk = imp_ref[g, k:k + 1, :]
            kg = k // SUBLANES
            for r in range(n_used):
                if r < kg:
                    beats = jnp.where(rowk > grp[r], 1.0, 0.0)
                elif r > kg:
                    beats = jnp.where(rowk >= grp[r], 1.0, 0.0)
                else:
                    beats = jnp.where(blk8 > k - SUBLANES * kg,
                                      jnp.where(rowk >= grp[r], 1.0, 0.0),
                                      jnp.where(rowk > grp[r], 1.0, 0.0))
                rank[r] = rank[r] + beats
        rank = jnp.concatenate(rank, axis=0)
        flags = jnp.where(rank < float(N_SEL), 0.0, -1.0).astype(BF16)
        pad = HEAD_DIM - n_used * SUBLANES
        if pad:
            flags = jnp.concatenate([flags, jnp.zeros((pad, TQ), BF16)], axis=0)
        return flags

    for hh in heads:
        qa_ref[hh, 0:HEAD_DIM, :] = qt_ref[0, hh, 0:HEAD_DIM, :]
    per_tile = TQ // L_SLC
    no_rank_tiles = N_SEL // per_tile

    @pl.when(i < no_rank_tiles)
    def _():
        for hh in heads:
            qa_ref[hh, HEAD_DIM:, :] = jnp.zeros((HEAD_DIM, TQ), BF16)

    lo = no_rank_tiles
    while lo * per_tile < ns:
        hi = min(lo + RANK_TILE_SPAN, ns // per_tile)
        n_used = -(-hi * per_tile // SUBLANES)

        @pl.when((i >= lo) & (i < hi))
        def _(n_used=n_used):
            for g in range(N_KV_HEADS):
                flags = block_flags(g, n_used)
                for hh in range(g * HPG, (g + 1) * HPG):
                    qa_ref[hh, HEAD_DIM:, :] = flags
        lo = hi

    n_far = jnp.maximum(i - 1, 0)

    def far_group(jj, carry):
        tasks = []
        for u in range(FAR_GROUP):
            tasks += slc_tasks(FAR_GROUP * jj + u)
        run(tasks)
        return carry

    lax.fori_loop(0, n_far // FAR_GROUP, far_group, 0)
    for rem in range(FAR_GROUP):
        @pl.when(n_far % FAR_GROUP == rem)
        def _(rem=rem):
            tasks = []
            for u in range(rem):
                tasks += slc_tasks(n_far - rem + u)
            run(tasks + slc_tasks(jnp.maximum(i - 1, 0), rows1) + slc_tasks(i, 2 * TQ))

    gates_t = jnp.transpose(ng_ref[...])
    outs = []
    for hh in heads:
        a_s, a_w = accs_ref[hh], accw_ref[hh]
        o_s = a_s[0:HEAD_DIM] * (1.0 / a_s[HEAD_DIM:HEAD_DIM + 1])
        o_w = a_w[0:HEAD_DIM] * (1.0 / a_w[HEAD_DIM:HEAD_DIM + 1])
        row = (hh // HPG) * LANES + 3 * (hh % HPG)
        outs.append(gates_t[row:row + 1] * oc_ref[hh] + gates_t[row + 1:row + 2] * o_s
                    + gates_t[row + 2:row + 3] * o_w)
    o_ref[...] = jnp.transpose(jnp.concatenate(outs, axis=0)).astype(BF16)


def _bias_of(thr_ref, rb_ref, dist, h):
    b = jnp.full(dist.shape, rb_ref[h], F32)
    for k in range(1, N_BUCKETS):
        b = jnp.where(dist >= thr_ref[k], rb_ref[k * N_HEADS + h], b)
    return b


def _cmp_bias_kernel(thr_ref, rb_ref, biasc_ref, *, nc, ncp):
    i = pl.program_id(0)
    per_tile = TQ // D_STRIDE
    band = 2 * per_tile
    assert TQ + D_STRIDE - (L_CMP - 1) >= MAX_DISTANCE

    def dist_of(c0, rows):
        c = c0 + lax.broadcasted_iota(jnp.int32, (rows, TQ), 0)
        r = lax.broadcasted_iota(jnp.int32, (rows, TQ), 1)
        dist = i * TQ + r - (c * D_STRIDE + L_CMP - 1)
        return dist, (dist >= 0) & (c < nc)

    _, ok_all = dist_of(0, ncp)
    band0 = pl.multiple_of(jnp.maximum(i * per_tile - per_tile, 0), per_tile)
    dist_b, ok_b = dist_of(band0, band)
    for h in range(N_HEADS):
        biasc_ref[h] = jnp.where(ok_all, rb_ref[(N_BUCKETS - 1) * N_HEADS + h], NEG)
        biasc_ref[h, pl.ds(band0, band), :] = jnp.where(
            ok_b, _bias_of(thr_ref, rb_ref, dist_b, h), NEG)


def _diag_bias_kernel(thr_ref, rb_ref, bd_ref):
    d0 = (lax.broadcasted_iota(jnp.int32, (TQ, TQ), 1)
          - lax.broadcasted_iota(jnp.int32, (TQ, TQ), 0))
    for h in range(N_HEADS):
        far = rb_ref[(N_BUCKETS - 1) * N_HEADS + h]
        bd_ref[h, 0:TQ, :] = jnp.where(d0 < 0, _bias_of(thr_ref, rb_ref, d0 + 2 * TQ, h) - far, NEG)
        bd_ref[h, TQ:2 * TQ, :] = _bias_of(thr_ref, rb_ref, d0 + TQ, h) - far
        bd_ref[h, 2 * TQ:3 * TQ, :] = jnp.where(d0 >= 0, _bias_of(thr_ref, rb_ref, d0, h) - far, NEG)
        bd_ref[h, 3 * TQ:, :] = jnp.full((TQ, TQ), NEG, F32)


def _attention_tables(rel_bias, seq):
    ncp = seq // D_STRIDE
    nc = (seq - L_CMP) // D_STRIDE + 1
    ns = seq // L_SLC
    nt = seq // TQ
    n_probe = 2 * MAX_DISTANCE
    buckets = _t5_bucket(jnp.arange(n_probe))
    thr = jnp.sum(buckets[None, :] < jnp.arange(N_BUCKETS)[:, None], axis=1).astype(jnp.int32)
    assert WINDOW == 2 * TQ
    rb = (rel_bias.astype(F32) * LOG2E).reshape(-1)
    bias_c = pl.pallas_call(
        functools.partial(_cmp_bias_kernel, nc=nc, ncp=ncp),
        out_shape=jax.ShapeDtypeStruct((N_HEADS, ncp, seq), F32),
        grid_spec=pltpu.PrefetchScalarGridSpec(
            num_scalar_prefetch=2, grid=(nt,), in_specs=[],
            out_specs=pl.BlockSpec((N_HEADS, ncp, TQ), lambda i, *_: (0, 0, i))),
        compiler_params=_cparams(1), name="cmp_bias")(thr, rb)
    bd = pl.pallas_call(
        _diag_bias_kernel, out_shape=jax.ShapeDtypeStruct((N_HEADS, 4 * TQ, TQ), F32),
        grid_spec=pltpu.PrefetchScalarGridSpec(
            num_scalar_prefetch=2, grid=(1,), in_specs=[],
            out_specs=pl.BlockSpec((N_HEADS, 4 * TQ, TQ), lambda i, *_: (0, 0, 0))),
        compiler_params=_cparams(1), name="diag_bias")(thr, rb)
    c_start = jnp.arange(ncp) * D_STRIDE
    c_end = c_start + L_CMP - 1
    s_start = jnp.arange(ns) * L_SLC
    aggt = ((c_end[None, :] >= s_start[:, None]) & (c_start[None, :] <= s_start[:, None] + L_SLC - 1)
            & (jnp.arange(ncp)[None, :] < nc)).astype(BF16)
    return bias_c, bd, aggt


def _nsa(qt, kcmp, vcmpt, ks, vst, kw, vwt, ng, rel_bias, batch, seq):
    first_far = TQ + 1
    assert MAX_EXACT + int(math.log(first_far / MAX_EXACT) / math.log(MAX_DISTANCE / MAX_EXACT)
                           * (N_BUCKETS - MAX_EXACT)) >= N_BUCKETS - 1, "far tiles need one bucket"
    ncp = seq // D_STRIDE
    ns = seq // L_SLC
    nt = seq // TQ
    bias_c, bd, aggt = _attention_tables(rel_bias, seq)
    kspec = pl.BlockSpec((1, N_KV_HEADS, seq, LANES), lambda b, i: (b, 0, 0, 0))
    vtspec = pl.BlockSpec((1, N_KV_HEADS, LANES, seq), lambda b, i: (b, 0, 0, 0))
    state = lambda rows: pltpu.VMEM((N_HEADS, rows, TQ), F32)
    const = lambda shape: pl.BlockSpec(shape, lambda b, i: (0,) * len(shape),
                                       pipeline_mode=pl.Buffered(1))
    grid_spec = pltpu.PrefetchScalarGridSpec(
        num_scalar_prefetch=0, grid=(batch, nt),
        in_specs=[pl.BlockSpec((1, N_HEADS, LANES, TQ), lambda b, i: (b, 0, 0, i)),
                  pl.BlockSpec((1, ncp, N_KV_HEADS * LANES), lambda b, i: (b, 0, 0)),
                  pl.BlockSpec((1, N_KV_HEADS, LANES, ncp), lambda b, i: (b, 0, 0, 0)),
                  pl.BlockSpec((N_HEADS, ncp, TQ), lambda b, i: (0, 0, i)),
                  kspec, vtspec, kspec, vtspec,
                  pl.BlockSpec((TQ, N_KV_HEADS * LANES), lambda b, i: (b * nt + i, 0)),
                  const((ns, ncp)), const((N_HEADS, 4 * TQ, TQ))],
        out_specs=pl.BlockSpec((TQ, N_HEADS * HEAD_DIM), lambda b, i: (b * nt + i, 0)),
        scratch_shapes=[pltpu.VMEM((N_KV_HEADS, ns, TQ), F32),
                        pltpu.VMEM((N_HEADS, LANES, TQ), BF16),
                        state(HEAD_DIM),
                        state(1), state(PV_ROWS),
                        state(1), state(PV_ROWS)])
    return pl.pallas_call(
        functools.partial(_nsa_kernel, ns=ns),
        out_shape=jax.ShapeDtypeStruct((batch * seq, N_HEADS * HEAD_DIM), BF16),
        grid_spec=grid_spec, compiler_params=_cparams(2), name="nsa",
    )(qt, kcmp, vcmpt, bias_c, ks, vst, kw, vwt, ng, aggt, bd)


def _ffn_kernel(x_ref, ya_ref, o_ref, gb_ref, p_ref, wao_ref, wout_ref, nf_ref, wup_ref, cw_ref,
                cb_ref, wd_ref, np_ref, wpg_ref, wpe_ref, nfin_ref, out_ref, carry_ref,
                *, tm, tiles_per_seq, d_ff, chunks):
    r = pl.program_id(0)

    @pl.when((r % tiles_per_seq) == 0)
    def _():
        carry_ref[...] = jnp.zeros_like(carry_ref)

    n = tm // FFN_SPLIT
    groups = [slice(k * n, (k + 1) * n) for k in range(FFN_SPLIT)]

    x1, hf = [], []
    for rows in groups:
        y = (ya_ref[rows, :].astype(F32)
             + gb_ref[rows, :].astype(F32) * _dot(o_ref[rows, :], wao_ref[...]))
        x1.append(x_ref[rows, :] + _dot(y.astype(BF16), wout_ref[...]))
        hf.append(_rms(x1[-1], nf_ref[...]).astype(BF16))

    def conv3(up, prev, c0, width):
        rowi = lax.broadcasted_iota(jnp.int32, up.shape, 0)
        s1 = jnp.where(rowi == 0, prev[7:8, :], pltpu.roll(up, 1, 0))
        s2 = jnp.where(rowi == 0, prev[6:7, :],
                       jnp.where(rowi == 1, prev[7:8, :], pltpu.roll(up, 2, 0)))
        return (cw_ref[0:1, c0:c0 + width] * s2 + cw_ref[1:2, c0:c0 + width] * s1
                + cw_ref[2:3, c0:c0 + width] * up + cb_ref[:, c0:c0 + width])

    acc = [None] * FFN_SPLIT
    for c0, width in chunks:
        cols = (slice(c0, c0 + width), slice(d_ff + c0, d_ff + c0 + width))
        prev = [carry_ref[:, cs] for cs in cols]
        for k in range(FFN_SPLIT):
            ups = [_dot(hf[k], wup_ref[:, cs]) for cs in cols]
            gate, val = [conv3(ups[t], prev[t], cols[t].start, width) for t in range(2)]
            prev = [u[n - 8:, :] for u in ups]
            part = _dot((_gelu(gate) * val).astype(BF16), wd_ref[c0:c0 + width, :])
            acc[k] = part if acc[k] is None else acc[k] + part
        for t in range(2):
            carry_ref[:, cols[t]] = prev[t]

    for k, rows in enumerate(groups):
        x2 = x1[k] + acc[k]
        pg = _sigmoid(_dot(_rms(x2, np_ref[...]).astype(BF16), wpg_ref[...]))
        pe = _dot(p_ref[rows, :].astype(BF16), wpe_ref[...])
        out_ref[rows, :] = _rms(x2 + pg * pe, nfin_ref[...])


def _ffn(x2, ya, o, gb, p2, w_attn_out, w_out, norm_ffn, w_up, ffn_dw_w, ffn_dw_b, w_down,
         norm_ple, w_ple_gate, w_ple, norm_final, seq, p_tile0=0, tm=512):
    t, d = x2.shape
    d_ff = w_down.shape[0]
    assert d_ff % MXU_DEPTH == 0
    chunks, c0 = [], 0
    while c0 < d_ff:
        width = min(FFN_CHUNK, d_ff - c0)
        chunks.append((c0, width))
        c0 += width
    tiles_per_seq = seq // tm
    cw = jnp.concatenate([ffn_dw_w, jnp.zeros((8 - FFN_CONV_WIDTH, 2 * d_ff), ffn_dw_w.dtype)], 0)
    vec = lambda v: v.reshape(1, -1)
    const = lambda a: pl.BlockSpec(a.shape, lambda r: (0, 0), pipeline_mode=pl.Buffered(1))
    rows = lambda n: pl.BlockSpec((tm, n), lambda r: (r, 0))
    kern = functools.partial(_ffn_kernel, tm=tm, tiles_per_seq=tiles_per_seq, d_ff=d_ff,
                             chunks=tuple(chunks))
    operands = (x2, ya, o, gb, p2, w_attn_out, w_out, vec(norm_ffn), w_up, cw, vec(ffn_dw_b),
                w_down, vec(norm_ple), w_ple_gate, w_ple, vec(norm_final))
    return pl.pallas_call(
        kern, out_shape=jax.ShapeDtypeStruct((t, d), F32), grid=(t // tm,),
        in_specs=[rows(d), rows(d), rows(o.shape[1]), rows(d),
                  pl.BlockSpec((tm, p2.shape[1]), lambda r: (p_tile0 + r, 0))]
        + [const(a) for a in operands[5:]],
        out_specs=rows(d),
        scratch_shapes=[pltpu.VMEM((8, 2 * d_ff), F32)],
        compiler_params=_cparams(1), name="ffn",
    )(*operands)


def kernel(x, p, rel_bias, norm_mix, w_in, conv_dw_w, conv_dw_b, conv_ln_g, conv_ln_b, w_conv_out,
           cmp_pe_k, cmp_pe_v, w_ck1, w_ck2, w_cv1, w_cv2, w_attn_out, w_out, norm_ffn, w_up,
           ffn_dw_w, ffn_dw_b, w_down, norm_ple, w_ple_gate, w_ple, norm_final):
    batch, seq, d = x.shape
    depth = w_in.shape[0]
    x2 = x.reshape(batch * seq, d)
    for i in range(depth):
        ya, qt, kc, vc, ks, vst, kw, vwt, ng, gb = _inproj(
            x2, norm_mix[i], _prep_inproj_weights(w_in[i], d), conv_dw_w[i], conv_dw_b[i],
            conv_ln_g[i], conv_ln_b[i], w_conv_out[i].astype(BF16), batch, seq)
        kcmp, vcmpt = _compress(kc, vc,
                               _prep_compress_weights(cmp_pe_k[i], w_ck1[i], w_ck2[i]),
                               _prep_compress_weights(cmp_pe_v[i], w_cv1[i], w_cv2[i]),
                               batch, seq)
        o = _nsa(qt, kcmp, vcmpt, ks, vst, kw, vwt, ng, rel_bias, batch, seq)
        assert i == depth - 1, "the final RMSNorm is fused into the (single) layer's MLP kernel"
        ffn_tm = 512
        x2 = _ffn(x2, ya, o, gb, p.reshape(depth * batch * seq, -1), w_attn_out[i].astype(BF16),
                  w_out[i].astype(BF16), norm_ffn[i], w_up[i].astype(BF16), ffn_dw_w[i],
                  ffn_dw_b[i], w_down[i].astype(BF16), norm_ple[i], w_ple_gate[i].astype(BF16),
                  w_ple[i].astype(BF16), norm_final, seq,
                  p_tile0=i * (batch * seq // ffn_tm), tm=ffn_tm)
    return x2.reshape(batch, seq, d)
```

```python
import functools
import math

import jax
import jax.numpy as jnp
from jax import lax
from jax.experimental import pallas as pl
from jax.experimental.pallas import tpu as pltpu

N_HEADS = 8
HEAD_DIM = 64
N_KV_HEADS = 2
HPG = N_HEADS // N_KV_HEADS
L_CMP = 32
D_STRIDE = 16
CMP_HIDDEN = 256
L_SLC = 64
N_SEL = 16
WINDOW = 512
N_BUCKETS = 32
MAX_EXACT = N_BUCKETS // 2
MAX_DISTANCE = 128
CONV_WIDTH = 31
FFN_CONV_WIDTH = 3
EPS = 1e-6
FORCE = 1e4

LANES = 128
MXU_DEPTH = 256
FFN_CHUNK = 11 * MXU_DEPTH
FFN_SPLIT = 2
NEG = -1e30
M_INIT = -1e29
LOG2E = 1.4426950408889634
PIPE_DEPTH = 6
RANK_TILE_SPAN = 4
FAR_GROUP = 4
TQ = 256
PV_ROWS = HEAD_DIM + 16
SUBLANES = 8
CONV_HALO = 32
CONV_ROWS = 64
VMEM_LIMIT = 56 * 1024 * 1024

F32 = jnp.float32
BF16 = jnp.bfloat16


def _cparams(n_axes):
    return pltpu.CompilerParams(dimension_semantics=("arbitrary",) * n_axes,
                                vmem_limit_bytes=VMEM_LIMIT)


def _dot(a, b):
    return jnp.dot(a, b, preferred_element_type=F32)


def _rms(xf, g):
    return xf * lax.rsqrt(jnp.mean(xf * xf, axis=-1, keepdims=True) + EPS) * g


def _sigmoid(x):
    return 1.0 / (1.0 + jnp.exp(-x))


def _gelu(x):
    return 0.5 * x * (1.0 + jnp.tanh(0.7978845608028654 * (x + 0.044715 * x * x * x)))


def _t5_bucket(dist):
    n = jnp.maximum(dist, 0)
    nf = jnp.maximum(n, MAX_EXACT).astype(F32)
    large = MAX_EXACT + (jnp.log(nf / MAX_EXACT) / math.log(MAX_DISTANCE / MAX_EXACT)
                         * (N_BUCKETS - MAX_EXACT)).astype(jnp.int32)
    large = jnp.minimum(large, N_BUCKETS - 1)
    return jnp.where(n < MAX_EXACT, n, large)


def _inproj_kernel(x_ref, g_ref, wu_ref, wq_ref, wkv_ref, wng_ref, wmg_ref,
                   cw_ref, cb_ref, lg_ref, lb_ref, wco_ref,
                   ya_ref, qt_ref, kc_ref, vc_ref, ks_ref, vst_ref, kw_ref, vwt_ref, ng_ref, gb_ref,
                   cs_ref, ext_ref, sh_ref, ga_ref, a_stage_ref, ga_stage_ref,
                   *, tm, n_tiles, tiles_per_seq, d_conv):
    step = pl.program_id(0)
    r = jnp.minimum(step, n_tiles - 1)

    @pl.when(step == 0)
    def _():
        ext_ref[...] = jnp.zeros(ext_ref.shape, F32)
        ga_ref[...] = jnp.zeros(ga_ref.shape, BF16)

    span = sh_ref.shape[1]
    for ph in range(1, SUBLANES):
        sh_ref[ph - 1] = ext_ref[ph:ph + span, :]
    off = CONV_HALO - (CONV_WIDTH - 1)
    conv = []
    for r0 in range(0, tm, CONV_ROWS):
        acc = jnp.zeros((CONV_ROWS, d_conv), F32) + cb_ref[...]
        for j in range(CONV_WIDTH):
            ph = (off + j) % SUBLANES
            base = r0 + off + j - ph
            if ph == 0:
                xs = ext_ref[base:base + CONV_ROWS, :]
            else:
                xs = sh_ref[ph - 1, base:base + CONV_ROWS, :]
            acc = acc + jnp.tile(cw_ref[j], (CONV_ROWS // SUBLANES, 1)) * xs
        conv.append(acc)
    conv = jnp.concatenate(conv, axis=0)
    mu = jnp.mean(conv, axis=-1, keepdims=True)
    cen = conv - mu
    var = jnp.mean(cen * cen, axis=-1, keepdims=True)
    y = cen * lax.rsqrt(var + EPS) * lg_ref[...] + lb_ref[...]
    y = (y * _sigmoid(y)).astype(BF16)

    h = _rms(x_ref[...], g_ref[...]).astype(BF16)
    u = _dot(h, wu_ref[...])
    a_stage_ref[...] = u[:, :d_conv] * _sigmoid(u[:, d_conv:])
    zmg = _dot(h, wmg_ref[...])
    d_model = zmg.shape[1] // 2
    ga_stage_ref[...] = _sigmoid(zmg[:, :d_model]).astype(BF16)
    gb_ref[...] = _sigmoid(zmg[:, d_model:]).astype(BF16)

    zqt = jnp.transpose(_dot(h, wq_ref[...]))
    zero_rows = jnp.zeros((HEAD_DIM, tm), BF16)
    for hh in range(N_HEADS):
        qt_ref[0, hh, 0:HEAD_DIM, :] = zqt[hh * HEAD_DIM:(hh + 1) * HEAD_DIM].astype(BF16)
        qt_ref[0, hh, HEAD_DIM:, :] = zero_rows

    zkv = _dot(h, wkv_ref[...])
    for which, out_ref in enumerate((kc_ref, vc_ref)):
        cs_ref[which] = zkv[:, which * LANES:(which + 1) * LANES]
        for tok in range(D_STRIDE):
            rows = cs_ref[which, pl.ds(tok, tm // D_STRIDE, stride=D_STRIDE), :]
            out_ref[:, tok * LANES:(tok + 1) * LANES] = rows.astype(BF16)
    lane = lax.broadcasted_iota(jnp.int32, (tm, LANES), 1)
    row = lax.broadcasted_iota(jnp.int32, (tm, LANES), 0)
    spos = (r % tiles_per_seq) * tm + row
    lo = lane < HEAD_DIM
    blk_tag = jnp.where((lane - HEAD_DIM) == spos // L_SLC, -NEG, 0.0)
    ones_rows = jnp.where(lax.broadcasted_iota(jnp.int32, (HEAD_DIM, tm), 0) < 8, 1.0, 0.0)
    ones_rows = ones_rows.astype(BF16)
    for g in range(N_KV_HEADS):
        for k_ref, vt_ref, base, tag in ((ks_ref, vst_ref, (2 + g) * LANES, blk_tag),
                                         (kw_ref, vwt_ref, (4 + g) * LANES, 0.0)):
            pair = zkv[:, base:base + LANES]
            k_ref[0, g] = jnp.where(lo, pair, tag).astype(BF16)
            vt_ref[0, g, 0:HEAD_DIM, :] = jnp.transpose(pair)[HEAD_DIM:].astype(BF16)
            vt_ref[0, g, HEAD_DIM:, :] = ones_rows

    ng_ref[...] = _sigmoid(_dot(h, wng_ref[...]))

    ya_ref[...] = (ga_ref[...].astype(F32) * _dot(y, wco_ref[...])).astype(BF16)

    seq_start = (r % tiles_per_seq) == 0
    ext_ref[0:CONV_HALO, :] = jnp.where(seq_start, 0.0, ext_ref[tm:tm + CONV_HALO, :])
    ext_ref[CONV_HALO:, :] = a_stage_ref[...]
    ga_ref[...] = ga_stage_ref[...]


def _prep_inproj_weights(w_in, d_model):
    d_conv = d_model // 2
    n_conv = 2 * d_conv
    n_q = N_HEADS * HEAD_DIM
    n_kv = 6 * N_KV_HEADS * HEAD_DIM
    n_ng = 3 * N_HEADS
    o = 0
    wu = w_in[:, o:o + n_conv]; o += n_conv
    wq = w_in[:, o:o + n_q]; o += n_q
    wkv = w_in[:, o:o + n_kv]; o += n_kv
    wng = w_in[:, o:o + n_ng]; o += n_ng
    wmg = w_in[:, o:]
    wq_p = wq * (LOG2E / math.sqrt(HEAD_DIM))
    kvcols = [wkv[:, 0:LANES], wkv[:, LANES:2 * LANES]]
    for k_kind in (2, 4):
        for g in range(N_KV_HEADS):
            for kind in (k_kind, k_kind + 1):
                c0 = kind * N_KV_HEADS * HEAD_DIM + g * HEAD_DIM
                kvcols.append(wkv[:, c0:c0 + HEAD_DIM])
    wkv_p = jnp.concatenate(kvcols, axis=1)
    per_g = HPG * 3
    ngcols = []
    for g in range(N_KV_HEADS):
        ngcols += [wng[:, g * per_g:(g + 1) * per_g],
                   jnp.zeros((d_model, LANES - per_g), w_in.dtype)]
    wng_p = jnp.concatenate(ngcols, axis=1)
    return tuple(w.astype(BF16) for w in (wu, wq_p, wkv_p, wng_p, wmg))


def _inproj(x2, norm_g, weights, conv_w, conv_b, ln_g, ln_b, w_conv_out, batch, seq, tm=512):
    t, d = x2.shape
    wu, wq, wkv, wng, wmg = weights
    d_conv = d // 2
    tiles_per_seq = seq // tm
    n_tiles = t // tm
    assert tm % CONV_ROWS == 0 and CONV_HALO >= CONV_WIDTH - 1 and CONV_HALO % SUBLANES == 0
    cw = jnp.broadcast_to(conv_w[:, None, :], (CONV_WIDTH, SUBLANES, d_conv))
    vec = lambda v: v.reshape(1, -1)
    full = lambda w: pl.BlockSpec(w.shape, lambda s: (0,) * w.ndim, pipeline_mode=pl.Buffered(1))
    tile = lambda s: jnp.minimum(s, n_tiles - 1)
    row = lambda n: pl.BlockSpec((tm, n), lambda s: (tile(s), 0))
    row_prev = lambda n: pl.BlockSpec((tm, n), lambda s: (jnp.maximum(s - 1, 0), 0))
    headed = lambda nh: pl.BlockSpec(
        (1, nh, tm, LANES), lambda s: (tile(s) // tiles_per_seq, 0, tile(s) % tiles_per_seq, 0))
    headed_t = lambda nh: pl.BlockSpec(
        (1, nh, LANES, tm), lambda s: (tile(s) // tiles_per_seq, 0, 0, tile(s) % tiles_per_seq))
    k_shape = jax.ShapeDtypeStruct((batch, N_KV_HEADS, seq, LANES), BF16)
    vt_shape = jax.ShapeDtypeStruct((batch, N_KV_HEADS, LANES, seq), BF16)
    out_shape = (
        jax.ShapeDtypeStruct((t, d), BF16),
        jax.ShapeDtypeStruct((batch, N_HEADS, LANES, seq), BF16),
        jax.ShapeDtypeStruct((t // D_STRIDE, D_STRIDE * LANES), BF16),
        jax.ShapeDtypeStruct((t // D_STRIDE, D_STRIDE * LANES), BF16),
        k_shape, vt_shape, k_shape, vt_shape,
        jax.ShapeDtypeStruct((t, N_KV_HEADS * LANES), F32),
        jax.ShapeDtypeStruct((t, d), BF16),
    )
    blocks = pl.BlockSpec((tm // D_STRIDE, D_STRIDE * LANES), lambda s: (tile(s), 0))
    out_specs = (row_prev(d), headed_t(N_HEADS), blocks, blocks,
                 headed(N_KV_HEADS), headed_t(N_KV_HEADS), headed(N_KV_HEADS),
                 headed_t(N_KV_HEADS), row(N_KV_HEADS * LANES), row(d))
    kern = functools.partial(_inproj_kernel, tm=tm, n_tiles=n_tiles, tiles_per_seq=tiles_per_seq,
                             d_conv=d_conv)
    consts = (vec(norm_g), wu, wq, wkv, wng, wmg, cw, vec(conv_b), vec(ln_g), vec(ln_b), w_conv_out)
    return pl.pallas_call(
        kern, out_shape=out_shape, grid=(n_tiles + 1,),
        in_specs=[row(d)] + [full(w) for w in consts],
        out_specs=out_specs,
        scratch_shapes=[pltpu.VMEM((2, tm, LANES), F32),
                        pltpu.VMEM((tm + CONV_HALO, d_conv), F32),
                        pltpu.VMEM((SUBLANES - 1, tm + CONV_HALO - SUBLANES, d_conv), F32),
                        pltpu.VMEM((tm, d), BF16),
                        pltpu.VMEM((tm, d_conv), F32), pltpu.VMEM((tm, d), BF16)],
        compiler_params=_cparams(1), name="inproj",
    )(x2, *consts)


def _compress_kernel(rk_ref, rv_ref, pek_ref, pev_ref, w1k_ref, w1v_ref, w2k_ref, w2v_ref,
                     kcmp_ref, vcmp_ref, *, ncp):
    def one(r_ref, pe_ref, w1_ref, w2_ref):
        r = r_ref[0]
        top = _dot(r, w1_ref[0])
        bot = _dot(r, w1_ref[1])
        pe_h = _dot(pe_ref[0], w1_ref[0]) + _dot(pe_ref[1], w1_ref[1])
        nxt = pltpu.roll(bot, ncp - 1, 0)
        rowi = lax.broadcasted_iota(jnp.int32, top.shape, 0)
        hid = top + jnp.where(rowi == ncp - 1, 0.0, nxt) + pe_h[0:1, :]
        return _dot(_gelu(hid).astype(BF16), w2_ref[...])

    kcmp_ref[0] = one(rk_ref, pek_ref, w1k_ref, w2k_ref).astype(BF16)
    v = one(rv_ref, pev_ref, w1v_ref, w2v_ref)
    rowt = lax.broadcasted_iota(jnp.int32, (LANES, ncp), 0)
    ones_rows = (rowt >= HEAD_DIM) & (rowt < HEAD_DIM + 8)
    for g in range(N_KV_HEADS):
        vt = jnp.transpose(v[:, g * LANES:(g + 1) * LANES])
        vcmp_ref[0, g] = jnp.where(ones_rows, 1.0, vt).astype(BF16)


def _prep_compress_weights(pe, w1, w2):
    half = L_CMP // 2
    w1r = w1.reshape(L_CMP, HEAD_DIM, CMP_HIDDEN).astype(BF16)
    w2p = jnp.pad(w2, ((0, 0), (0, LANES - HEAD_DIM))).astype(BF16)

    def per_group(w, axis):
        z = jnp.zeros_like(w)
        return jnp.stack([jnp.concatenate([w if k == g else z for k in range(N_KV_HEADS)], axis=-1)
                          for g in range(N_KV_HEADS)], axis=axis)

    w1big = per_group(w1r, 1).reshape(2, half * N_KV_HEADS * HEAD_DIM, N_KV_HEADS * CMP_HIDDEN)
    w2big = per_group(w2p, 0).reshape(N_KV_HEADS * CMP_HIDDEN, N_KV_HEADS * LANES)
    per = pe.reshape(2, half, 1, HEAD_DIM)
    pebig = jnp.broadcast_to(per, (2, half, N_KV_HEADS, HEAD_DIM)).reshape(2, 1, -1)
    pebig = jnp.broadcast_to(pebig, (2, 8, pebig.shape[-1])).astype(BF16)
    return pebig, w1big, w2big


def _compress(kc, vc, wk, wv, batch, seq):
    ncp = seq // D_STRIDE
    width = D_STRIDE * LANES
    rk = kc.reshape(batch, ncp, width)
    rv = vc.reshape(batch, ncp, width)
    pek, w1k, w2k = wk
    pev, w1v, w2v = wv
    c3 = lambda a: pl.BlockSpec(a.shape, lambda b: (0, 0, 0))
    c2 = lambda a: pl.BlockSpec(a.shape, lambda b: (0, 0))
    rspec = pl.BlockSpec((1, ncp, width), lambda b: (b, 0, 0))
    kspec = pl.BlockSpec((1, ncp, N_KV_HEADS * LANES), lambda b: (b, 0, 0))
    kshape = jax.ShapeDtypeStruct((batch, ncp, N_KV_HEADS * LANES), BF16)
    vspec = pl.BlockSpec((1, N_KV_HEADS, LANES, ncp), lambda b: (b, 0, 0, 0))
    vshape = jax.ShapeDtypeStruct((batch, N_KV_HEADS, LANES, ncp), BF16)
    return pl.pallas_call(
        functools.partial(_compress_kernel, ncp=ncp), out_shape=(kshape, vshape), grid=(batch,),
        in_specs=[rspec, rspec, c3(pek), c3(pev), c3(w1k), c3(w1v), c2(w2k), c2(w2v)],
        out_specs=(kspec, vspec), compiler_params=_cparams(1), name="compress",
    )(rk, rv, pek, pev, w1k, w1v, w2k, w2v)


def _nsa_kernel(qt_ref, kcmp_ref, vcmpt_ref, biasc_ref, ks_ref, vst_ref, kw_ref, vwt_ref,
                ng_ref, aggt_ref, bd_ref, o_ref,
                imp_ref, qa_ref, oc_ref, ms_ref, accs_ref, mw_ref, accw_ref, *, ns):
    i = pl.program_id(1)
    heads = range(N_HEADS)

    def k_tile(ref, g, j):
        return ref[0, g, pl.ds(pl.multiple_of(j * TQ, TQ), TQ), :]

    def vt_tile(ref, g, j):
        return ref[0, g, 0:PV_ROWS, pl.ds(pl.multiple_of(j * TQ, TQ), TQ)]

    def pipelined(tasks, scores, update):
        pending = [scores(t) for t in tasks[:PIPE_DEPTH]]
        for t, task in enumerate(tasks):
            update(task, pending.pop(0))
            if t + PIPE_DEPTH < len(tasks):
                pending.append(scores(tasks[t + PIPE_DEPTH]))

    def run(tasks):
        def scores(task):
            hh, kt, _, qmat, _, _, row0 = task
            s = _dot(kt, qmat)
            if row0 is None:
                return s
            if not isinstance(row0, int):
                row0 = pl.multiple_of(row0, TQ)
            return s + bd_ref[hh, pl.ds(row0, TQ), :]

        def update(task, s):
            hh, _, vt, _, m_ref, acc_ref, _ = task
            m_old = m_ref[hh]
            m_new = jnp.maximum(m_old, jnp.max(s, axis=0, keepdims=True))
            p = jnp.exp2(s - m_new).astype(BF16)
            acc_ref[hh] = jnp.exp2(m_old - m_new) * acc_ref[hh] + _dot(vt, p)
            m_ref[hh] = m_new

        pipelined(tasks, scores, update)

    def slc_tasks(j, row0=None):
        tiles = [(k_tile(ks_ref, g, j), vt_tile(vst_ref, g, j)) for g in range(N_KV_HEADS)]
        return [(hh,) + tiles[hh // HPG] + (qa_ref[hh], ms_ref, accs_ref, row0) for hh in heads]

    def win_tasks(j, row0):
        tiles = [(k_tile(kw_ref, g, j), vt_tile(vwt_ref, g, j)) for g in range(N_KV_HEADS)]
        return [(hh,) + tiles[hh // HPG] + (qt_ref[0, hh], mw_ref, accw_ref, row0) for hh in heads]

    for hh in heads:
        ms_ref[hh] = jnp.full((1, TQ), M_INIT, F32)
        mw_ref[hh] = jnp.full((1, TQ), M_INIT, F32)
        accs_ref[hh] = jnp.zeros((PV_ROWS, TQ), F32)
        accw_ref[hh] = jnp.zeros((PV_ROWS, TQ), F32)

    p_parts = []

    def cmp_scores(hh):
        g = hh // HPG
        return _dot(kcmp_ref[0, :, g * LANES:(g + 1) * LANES], qt_ref[0, hh]) + biasc_ref[hh]

    def cmp_update(hh, s_c):
        m_c = jnp.maximum(jnp.max(s_c, axis=0, keepdims=True), M_INIT)
        p_c = jnp.exp2(s_c - m_c)
        l_c = jnp.sum(p_c, axis=0, keepdims=True)
        p_c = p_c * jnp.where(l_c > 0.0, 1.0 / l_c, 0.0)
        oc_ref[hh] = _dot(vcmpt_ref[0, hh // HPG, 0:HEAD_DIM, :], p_c.astype(BF16))
        p_parts.append(p_c)

    pipelined(list(heads), cmp_scores, cmp_update)

    blk = lax.broadcasted_iota(jnp.int32, (ns, TQ), 0)
    cur = (i * TQ + lax.broadcasted_iota(jnp.int32, (ns, TQ), 1)) // L_SLC
    forced = (blk == 0) | (blk == cur) | (blk == cur - 1)
    for g in range(N_KV_HEADS):
        p4 = p_parts[g * HPG:(g + 1) * HPG]
        p_sum = (p4[0] + p4[1]) + (p4[2] + p4[3])
        p_hi = p_sum.astype(BF16)
        p_lo = (p_sum - p_hi.astype(F32)).astype(BF16)
        imp = _dot(aggt_ref[...], p_hi) + _dot(aggt_ref[...], p_lo)
        imp_ref[g] = jnp.where(forced, FORCE, jnp.where(blk <= cur, imp, -FORCE))

    rows2 = jnp.where(i >= 2, 0, 3 * TQ)
    rows1 = jnp.where(i >= 1, TQ, 3 * TQ)
    run(win_tasks(jnp.maximum(i - 2, 0), rows2) + win_tasks(jnp.maximum(i - 1, 0), rows1)
        + win_tasks(i, 2 * TQ))

    blk8 = lax.broadcasted_iota(jnp.int32, (SUBLANES, TQ), 0)

    def block_flags(g, n_used):
        grp = [imp_ref[g, SUBLANES * r:SUBLANES * (r + 1), :] for r in range(n_used)]
        rank = [jnp.zeros((SUBLANES, TQ), F32) for _ in range(n_used)]
        for k in range(n_used * SUBLANES):
            rowk = imp_ref[g, k:k + 1, :]
            kg = k // SUBLANES
            for r in range(n_used):
                if r < kg:
                    beats = jnp.where(rowk > grp[r], 1.0, 0.0)
                elif r > kg:
                    beats = jnp.where(rowk >= grp[r], 1.0, 0.0)
                else:
                    beats = jnp.where(blk8 > k - SUBLANES * kg,
                                      jnp.where(rowk >= grp[r], 1.0, 0.0),
                                      jnp.where(rowk > grp[r], 1.0, 0.0))
                rank[r] = rank[r] + beats
        rank = jnp.concatenate(rank, axis=0)
        flags = jnp.where(rank < float(N_SEL), 0.0, -1.0).astype(BF16)
        pad = HEAD_DIM - n_used * SUBLANES
        if pad:
            flags = jnp.concatenate([flags, jnp.zeros((pad, TQ), BF16)], axis=0)
        return flags

    for hh in heads:
        qa_ref[hh, 0:HEAD_DIM, :] = qt_ref[0, hh, 0:HEAD_DIM, :]
    per_tile = TQ // L_SLC
    no_rank_tiles = N_SEL // per_tile

    @pl.when(i < no_rank_tiles)
    def _():
        for hh in heads:
            qa_ref[hh, HEAD_DIM:, :] = jnp.zeros((HEAD_DIM, TQ), BF16)

    lo = no_rank_tiles
    while lo * per_tile < ns:
        hi = min(lo + RANK_TILE_SPAN, ns // per_tile)
        n_used = -(-hi * per_tile // SUBLANES)

        @pl.when((i >= lo) & (i < hi))
        def _(n_used=n_used):
            for g in range(N_KV_HEADS):
                flags = block_flags(g, n_used)
                for hh in range(g * HPG, (g + 1) * HPG):
                    qa_ref[hh, HEAD_DIM:, :] = flags
        lo = hi

    n_far = jnp.maximum(i - 1, 0)

    def far_group(jj, carry):
        tasks = []
        for u in range(FAR_GROUP):
            tasks += slc_tasks(FAR_GROUP * jj + u)
        run(tasks)
        return carry

    lax.fori_loop(0, n_far // FAR_GROUP, far_group, 0)
    for rem in range(FAR_GROUP):
        @pl.when(n_far % FAR_GROUP == rem)
        def _(rem=rem):
            tasks = []
            for u in range(rem):
                tasks += slc_tasks(n_far - rem + u)
            run(tasks + slc_tasks(jnp.maximum(i - 1, 0), rows1) + slc_tasks(i, 2 * TQ))

    gates_t = jnp.transpose(ng_ref[...])
    outs = []
    for hh in heads:
        a_s, a_w = accs_ref[hh], accw_ref[hh]
        o_s = a_s[0:HEAD_DIM] * (1.0 / a_s[HEAD_DIM:HEAD_DIM + 1])
        o_w = a_w[0:HEAD_DIM] * (1.0 / a_w[HEAD_DIM:HEAD_DIM + 1])
        row = (hh // HPG) * LANES + 3 * (hh % HPG)
        outs.append(gates_t[row:row + 1] * oc_ref[hh] + gates_t[row + 1:row + 2] * o_s
                    + gates_t[row + 2:row + 3] * o_w)
    o_ref[...] = jnp.transpose(jnp.concatenate(outs, axis=0)).astype(BF16)


def _bias_of(thr_ref, rb_ref, dist, h):
    b = jnp.full(dist.shape, rb_ref[h], F32)
    for k in range(1, N_BUCKETS):
        b = jnp.where(dist >= thr_ref[k], rb_ref[k * N_HEADS + h], b)
    return b


def _cmp_bias_kernel(thr_ref, rb_ref, biasc_ref, *, nc, ncp):
    i = pl.program_id(0)
    per_tile = TQ // D_STRIDE
    band = 2 * per_tile
    assert TQ + D_STRIDE - (L_CMP - 1) >= MAX_DISTANCE

    def dist_of(c0, rows):
        c = c0 + lax.broadcasted_iota(jnp.int32, (rows, TQ), 0)
        r = lax.broadcasted_iota(jnp.int32, (rows, TQ), 1)
        dist = i * TQ + r - (c * D_STRIDE + L_CMP - 1)
        return dist, (dist >= 0) & (c < nc)

    _, ok_all = dist_of(0, ncp)
    band0 = pl.multiple_of(jnp.maximum(i * per_tile - per_tile, 0), per_tile)
    dist_b, ok_b = dist_of(band0, band)
    for h in range(N_HEADS):
        biasc_ref[h] = jnp.where(ok_all, rb_ref[(N_BUCKETS - 1) * N_HEADS + h], NEG)
        biasc_ref[h, pl.ds(band0, band), :] = jnp.where(
            ok_b, _bias_of(thr_ref, rb_ref, dist_b, h), NEG)


def _diag_bias_kernel(thr_ref, rb_ref, bd_ref):
    d0 = (lax.broadcasted_iota(jnp.int32, (TQ, TQ), 1)
          - lax.broadcasted_iota(jnp.int32, (TQ, TQ), 0))
    for h in range(N_HEADS):
        far = rb_ref[(N_BUCKETS - 1) * N_HEADS + h]
        bd_ref[h, 0:TQ, :] = jnp.where(d0 < 0, _bias_of(thr_ref, rb_ref, d0 + 2 * TQ, h) - far, NEG)
        bd_ref[h, TQ:2 * TQ, :] = _bias_of(thr_ref, rb_ref, d0 + TQ, h) - far
        bd_ref[h, 2 * TQ:3 * TQ, :] = jnp.where(d0 >= 0, _bias_of(thr_ref, rb_ref, d0, h) - far, NEG)
        bd_ref[h, 3 * TQ:, :] = jnp.full((TQ, TQ), NEG, F32)


def _attention_tables(rel_bias, seq):
    ncp = seq // D_STRIDE
    nc = (seq - L_CMP) // D_STRIDE + 1
    ns = seq // L_SLC
    nt = seq // TQ
    n_probe = 2 * MAX_DISTANCE
    buckets = _t5_bucket(jnp.arange(n_probe))
    thr = jnp.sum(buckets[None, :] < jnp.arange(N_BUCKETS)[:, None], axis=1).astype(jnp.int32)
    assert WINDOW == 2 * TQ
    rb = (rel_bias.astype(F32) * LOG2E).reshape(-1)
    bias_c = pl.pallas_call(
        functools.partial(_cmp_bias_kernel, nc=nc, ncp=ncp),
        out_shape=jax.ShapeDtypeStruct((N_HEADS, ncp, seq), F32),
        grid_spec=pltpu.PrefetchScalarGridSpec(
            num_scalar_prefetch=2, grid=(nt,), in_specs=[],
            out_specs=pl.BlockSpec((N_HEADS, ncp, TQ), lambda i, *_: (0, 0, i))),
        compiler_params=_cparams(1), name="cmp_bias")(thr, rb)
    bd = pl.pallas_call(
        _diag_bias_kernel, out_shape=jax.ShapeDtypeStruct((N_HEADS, 4 * TQ, TQ), F32),
        grid_spec=pltpu.PrefetchScalarGridSpec(
            num_scalar_prefetch=2, grid=(1,), in_specs=[],
            out_specs=pl.BlockSpec((N_HEADS, 4 * TQ, TQ), lambda i, *_: (0, 0, 0))),
        compiler_params=_cparams(1), name="diag_bias")(thr, rb)
    c_start = jnp.arange(ncp) * D_STRIDE
    c_end = c_start + L_CMP - 1
    s_start = jnp.arange(ns) * L_SLC
    aggt = ((c_end[None, :] >= s_start[:, None]) & (c_start[None, :] <= s_start[:, None] + L_SLC - 1)
            & (jnp.arange(ncp)[None, :] < nc)).astype(BF16)
    return bias_c, bd, aggt


def _nsa(qt, kcmp, vcmpt, ks, vst, kw, vwt, ng, rel_bias, batch, seq):
    first_far = TQ + 1
    assert MAX_EXACT + int(math.log(first_far / MAX_EXACT) / math.log(MAX_DISTANCE / MAX_EXACT)
                           * (N_BUCKETS - MAX_EXACT)) >= N_BUCKETS - 1, "far tiles need one bucket"
    ncp = seq // D_STRIDE
    ns = seq // L_SLC
    nt = seq // TQ
    bias_c, bd, aggt = _attention_tables(rel_bias, seq)
    kspec = pl.BlockSpec((1, N_KV_HEADS, seq, LANES), lambda b, i: (b, 0, 0, 0))
    vtspec = pl.BlockSpec((1, N_KV_HEADS, LANES, seq), lambda b, i: (b, 0, 0, 0))
    state = lambda rows: pltpu.VMEM((N_HEADS, rows, TQ), F32)
    const = lambda shape: pl.BlockSpec(shape, lambda b, i: (0,) * len(shape),
                                       pipeline_mode=pl.Buffered(1))
    grid_spec = pltpu.PrefetchScalarGridSpec(
        num_scalar_prefetch=0, grid=(batch, nt),
        in_specs=[pl.BlockSpec((1, N_HEADS, LANES, TQ), lambda b, i: (b, 0, 0, i)),
                  pl.BlockSpec((1, ncp, N_KV_HEADS * LANES), lambda b, i: (b, 0, 0)),
                  pl.BlockSpec((1, N_KV_HEADS, LANES, ncp), lambda b, i: (b, 0, 0, 0)),
                  pl.BlockSpec((N_HEADS, ncp, TQ), lambda b, i: (0, 0, i)),
                  kspec, vtspec, kspec, vtspec,
                  pl.BlockSpec((TQ, N_KV_HEADS * LANES), lambda b, i: (b * nt + i, 0)),
                  const((ns, ncp)), const((N_HEADS, 4 * TQ, TQ))],
        out_specs=pl.BlockSpec((TQ, N_HEADS * HEAD_DIM), lambda b, i: (b * nt + i, 0)),
        scratch_shapes=[pltpu.VMEM((N_KV_HEADS, ns, TQ), F32),
                        pltpu.VMEM((N_HEADS, LANES, TQ), BF16),
                        state(HEAD_DIM),
                        state(1), state(PV_ROWS),
                        state(1), state(PV_ROWS)])
    return pl.pallas_call(
        functools.partial(_nsa_kernel, ns=ns),
        out_shape=jax.ShapeDtypeStruct((batch * seq, N_HEADS * HEAD_DIM), BF16),
        grid_spec=grid_spec, compiler_params=_cparams(2), name="nsa",
    )(qt, kcmp, vcmpt, bias_c, ks, vst, kw, vwt, ng, aggt, bd)


def _ffn_kernel(x_ref, ya_ref, o_ref, gb_ref, p_ref, wao_ref, wout_ref, nf_ref, wup_ref, cw_ref,
                cb_ref, wd_ref, np_ref, wpg_ref, wpe_ref, nfin_ref, out_ref, carry_ref,
                *, tm, tiles_per_seq, d_ff, chunks):
    r = pl.program_id(0)

    @pl.when((r % tiles_per_seq) == 0)
    def _():
        carry_ref[...] = jnp.zeros_like(carry_ref)

    n = tm // FFN_SPLIT
    groups = [slice(k * n, (k + 1) * n) for k in range(FFN_SPLIT)]

    x1, hf = [], []
    for rows in groups:
        y = (ya_ref[rows, :].astype(F32)
             + gb_ref[rows, :].astype(F32) * _dot(o_ref[rows, :], wao_ref[...]))
        x1.append(x_ref[rows, :] + _dot(y.astype(BF16), wout_ref[...]))
        hf.append(_rms(x1[-1], nf_ref[...]).astype(BF16))

    def conv3(up, prev, c0, width):
        rowi = lax.broadcasted_iota(jnp.int32, up.shape, 0)
        s1 = jnp.where(rowi == 0, prev[7:8, :], pltpu.roll(up, 1, 0))
        s2 = jnp.where(rowi == 0, prev[6:7, :],
                       jnp.where(rowi == 1, prev[7:8, :], pltpu.roll(up, 2, 0)))
        return (cw_ref[0:1, c0:c0 + width] * s2 + cw_ref[1:2, c0:c0 + width] * s1
                + cw_ref[2:3, c0:c0 + width] * up + cb_ref[:, c0:c0 + width])

    acc = [None] * FFN_SPLIT
    for c0, width in chunks:
        cols = (slice(c0, c0 + width), slice(d_ff + c0, d_ff + c0 + width))
        prev = [carry_ref[:, cs] for cs in cols]
        for k in range(FFN_SPLIT):
            ups = [_dot(hf[k], wup_ref[:, cs]) for cs in cols]
            gate, val = [conv3(ups[t], prev[t], cols[t].start, width) for t in range(2)]
            prev = [u[n - 8:, :] for u in ups]
            part = _dot((_gelu(gate) * val).astype(BF16), wd_ref[c0:c0 + width, :])
            acc[k] = part if acc[k] is None else acc[k] + part
        for t in range(2):
            carry_ref[:, cols[t]] = prev[t]

    for k, rows in enumerate(groups):
        x2 = x1[k] + acc[k]
        pg = _sigmoid(_dot(_rms(x2, np_ref[...]).astype(BF16), wpg_ref[...]))
        pe = _dot(p_ref[rows, :].astype(BF16), wpe_ref[...])
        out_ref[rows, :] = _rms(x2 + pg * pe, nfin_ref[...])


def _ffn(x2, ya, o, gb, p2, w_attn_out, w_out, norm_ffn, w_up, ffn_dw_w, ffn_dw_b, w_down,
         norm_ple, w_ple_gate, w_ple, norm_final, seq, p_tile0=0, tm=512):
    t, d = x2.shape
    d_ff = w_down.shape[0]
    assert d_ff % MXU_DEPTH == 0
    chunks, c0 = [], 0
    while c0 < d_ff:
        width = min(FFN_CHUNK, d_ff - c0)
        chunks.append((c0, width))
        c0 += width
    tiles_per_seq = seq // tm
    cw = jnp.concatenate([ffn_dw_w, jnp.zeros((8 - FFN_CONV_WIDTH, 2 * d_ff), ffn_dw_w.dtype)], 0)
    vec = lambda v: v.reshape(1, -1)
    const = lambda a: pl.BlockSpec(a.shape, lambda r: (0, 0), pipeline_mode=pl.Buffered(1))
    rows = lambda n: pl.BlockSpec((tm, n), lambda r: (r, 0))
    kern = functools.partial(_ffn_kernel, tm=tm, tiles_per_seq=tiles_per_seq, d_ff=d_ff,
                             chunks=tuple(chunks))
    operands = (x2, ya, o, gb, p2, w_attn_out, w_out, vec(norm_ffn), w_up, cw, vec(ffn_dw_b),
                w_down, vec(norm_ple), w_ple_gate, w_ple, vec(norm_final))
    return pl.pallas_call(
        kern, out_shape=jax.ShapeDtypeStruct((t, d), F32), grid=(t // tm,),
        in_specs=[rows(d), rows(d), rows(o.shape[1]), rows(d),
                  pl.BlockSpec((tm, p2.shape[1]), lambda r: (p_tile0 + r, 0))]
        + [const(a) for a in operands[5:]],
        out_specs=rows(d),
        scratch_shapes=[pltpu.VMEM((8, 2 * d_ff), F32)],
        compiler_params=_cparams(1), name="ffn",
    )(*operands)


def kernel(x, p, rel_bias, norm_mix, w_in, conv_dw_w, conv_dw_b, conv_ln_g, conv_ln_b, w_conv_out,
           cmp_pe_k, cmp_pe_v, w_ck1, w_ck2, w_cv1, w_cv2, w_attn_out, w_out, norm_ffn, w_up,
           ffn_dw_w, ffn_dw_b, w_down, norm_ple, w_ple_gate, w_ple, norm_final):
    batch, seq, d = x.shape
    depth = w_in.shape[0]
    x2 = x.reshape(batch * seq, d)
    for i in range(depth):
        ya, qt, kc, vc, ks, vst, kw, vwt, ng, gb = _inproj(
            x2, norm_mix[i], _prep_inproj_weights(w_in[i], d), conv_dw_w[i], conv_dw_b[i],
            conv_ln_g[i], conv_ln_b[i], w_conv_out[i].astype(BF16), batch, seq)
        kcmp, vcmpt = _compress(kc, vc,
                               _prep_compress_weights(cmp_pe_k[i], w_ck1[i], w_ck2[i]),
                               _prep_compress_weights(cmp_pe_v[i], w_cv1[i], w_cv2[i]),
                               batch, seq)
        o = _nsa(qt, kcmp, vcmpt, ks, vst, kw, vwt, ng, rel_bias, batch, seq)
        assert i == depth - 1, "the final RMSNorm is fused into the (single) layer's MLP kernel"
        ffn_tm = 512
        x2 = _ffn(x2, ya, o, gb, p.reshape(depth * batch * seq, -1), w_attn_out[i].astype(BF16),
                  w_out[i].astype(BF16), norm_ffn[i], w_up[i].astype(BF16), ffn_dw_w[i],
                  ffn_dw_b[i], w_down[i].astype(BF16), norm_ple[i], w_ple_gate[i].astype(BF16),
                  w_ple[i].astype(BF16), norm_final, seq,
                  p_tile0=i * (batch * seq // ffn_tm), tm=ffn_tm)
    return x2.reshape(batch, seq, d)
```
